```python
import jax, jax.numpy as jnp
from jax import lax
import numpy as np

D_MODEL = 2048
BATCH = 8
SEQ = 2048
DEPTH = 1

HEAD_DIM = 64
MIX_WIDTH = D_MODEL
N_HEADS_A = (MIX_WIDTH // 2) // HEAD_DIM
N_HEADS_B = (MIX_WIDTH // 2) // HEAD_DIM
WIDTH_A = N_HEADS_A * HEAD_DIM
WIDTH_B = N_HEADS_B * HEAD_DIM
IN_WIDTH = 4 * WIDTH_A + 4 * WIDTH_B
DILATED_CONFIGS = ((128, 1), (512, 4), (2048, 16))
NUM_BUCKETS = 32
T5_MAX_DISTANCE = 1024
GRID_W = 64
NA_ROWS = 8
NA_COLS = 16
NA_COL_BLOCKS = GRID_W // NA_COLS
NA_KEY_COLS = 2 * NA_COLS
DEEPNORM_ALPHA = (2.0 * DEPTH) ** 0.25
DEEPNORM_BETA = (8.0 * DEPTH) ** -0.25
LN_EPS = 1e-5
NEG_INF = -1e30

kernel_name = "hybrid_dilated_neighbourhood_encoder"


def _t5_buckets(rel):
    half = NUM_BUCKETS // 2
    max_exact = half // 2
    n = np.abs(rel)
    large = max_exact + (np.log(np.maximum(n, 1) / max_exact)
                         / np.log(T5_MAX_DISTANCE / max_exact)
                         * (half - max_exact)).astype(np.int64)
    large = np.minimum(large, half - 1)
    return ((rel > 0).astype(np.int64) * half + np.where(n < max_exact, n, large)).astype(np.int32)


def _layer_norm(x, g, b):
    xf = x.astype(jnp.float32)
    mu = xf.mean(-1, keepdims=True)
    var = jnp.square(xf - mu).mean(-1, keepdims=True)
    return ((xf - mu) * lax.rsqrt(var + LN_EPS) * g.astype(jnp.float32)
            + b.astype(jnp.float32)).astype(x.dtype)


def _dilated_branch(q, k, v, t5_table, window, dilation):
    B, H, S, hd = q.shape
    blk = window // (2 * dilation)
    L = S // dilation
    nb = -(-L // blk)
    Lp = nb * blk
    scale = HEAD_DIM ** -0.5

    def strided(a):
        return a.reshape(B, H, L, dilation, hd).transpose(0, 1, 3, 2, 4)

    qs = jnp.pad(strided(q), ((0, 0), (0, 0), (0, 0), (0, Lp - L), (0, 0)))
    qs = qs.reshape(B, H, dilation, nb, blk, hd)

    def key_blocks(a):
        a = jnp.pad(strided(a), ((0, 0), (0, 0), (0, 0), (blk, Lp - L + blk), (0, 0)))
        a = a.reshape(B, H, dilation, nb + 2, blk, hd)
        return jnp.concatenate([a[:, :, :, :nb], a[:, :, :, 1:nb + 1], a[:, :, :, 2:]], axis=4)

    ks, vs = key_blocks(k), key_blocks(v)
    rel = np.arange(3 * blk)[None, :] - blk - np.arange(blk)[:, None]
    band = np.abs(rel) <= blk
    key_abs = (np.arange(nb)[:, None] - 1) * blk + np.arange(3 * blk)[None, :]
    valid = (key_abs >= 0) & (key_abs < L)
    mask = band[None] & valid[:, None, :]
    bias = jnp.take(t5_table, _t5_buckets(rel * dilation), axis=0)
    bias = jnp.moveaxis(bias.astype(jnp.float32), -1, 0)

    s = jnp.einsum('bhrnqd,bhrnkd->bhrnqk', qs, ks).astype(jnp.float32) * scale
    s = jnp.where(mask, s + bias[None, :, None, None], NEG_INF)
    m = s.max(-1, keepdims=True)
    p = jnp.exp(s - m)
    den = p.sum(-1, keepdims=True)
    o = jnp.einsum('bhrnqk,bhrnkd->bhrnqd', p, vs.astype(jnp.float32)) / den
    lse = (m + jnp.log(den))[..., 0]
    o = o.reshape(B, H, dilation, Lp, hd)[:, :, :, :L].transpose(0, 1, 3, 2, 4).reshape(B, H, S, hd)
    lse = lse.reshape(B, H, dilation, Lp)[..., :L].transpose(0, 1, 3, 2).reshape(B, H, S)
    return o, lse


def _dilated_attention(q, k, v, t5_table):
    outs, lses = [], []
    for window, dilation in DILATED_CONFIGS:
        o, lse = _dilated_branch(q, k, v, t5_table, window, dilation)
        outs.append(o)
        lses.append(lse)
    wts = jax.nn.softmax(jnp.stack(lses, 0), axis=0)
    return jnp.einsum('cbhs,cbhsd->bhsd', wts, jnp.stack(outs, 0))


def _neighbourhood_attention(q, k, v, rpb):
    B, H, S, hd = q.shape
    rows = S // GRID_W
    kr = min(NA_ROWS, rows)
    scale = HEAD_DIM ** -0.5
    row_start = np.clip(np.arange(rows) - kr // 2, 0, rows - kr)
    row_idx = row_start[:, None] + np.arange(kr)
    qcol = np.arange(GRID_W).reshape(NA_COL_BLOCKS, NA_COLS)
    blk_start = np.clip(qcol[:, 0] - NA_COLS // 2, 0, GRID_W - NA_KEY_COLS)
    col_idx = blk_start[:, None] + np.arange(NA_KEY_COLS)
    q_start = np.clip(qcol - NA_COLS // 2, 0, GRID_W - NA_COLS)
    kcol = col_idx[:, None, :]
    col_mask = (kcol >= q_start[:, :, None]) & (kcol < q_start[:, :, None] + NA_COLS)
    dcol = np.clip(kcol - qcol[:, :, None], -(NA_COLS - 1), NA_COLS - 1)
    drow = row_idx - np.arange(rows)[:, None]
    ri = (drow + NA_ROWS - 1)[None, :, None, :, None]
    ci = (dcol + NA_COLS - 1)[:, None, :, None, :]
    bias = rpb.astype(jnp.float32)[:, ri, ci]
    bias = jnp.where(col_mask[:, None, :, None, :], bias, NEG_INF)
    bias = bias.transpose(1, 0, 2, 3, 4, 5).reshape(NA_COL_BLOCKS, H, rows, NA_COLS, kr * NA_KEY_COLS)

    qb = q.reshape(B, H, rows, NA_COL_BLOCKS, NA_COLS, hd).transpose(3, 0, 1, 2, 4, 5)
    kg = k.reshape(B, H, rows, GRID_W, hd)
    vg = v.reshape(B, H, rows, GRID_W, hd)

    def row_block(args):
        q_blk, cidx, bias_blk = args
        k_blk = jnp.take(kg, cidx, axis=3)[:, :, row_idx].reshape(B, H, rows, kr * NA_KEY_COLS, hd)
        v_blk = jnp.take(vg, cidx, axis=3)[:, :, row_idx].reshape(B, H, rows, kr * NA_KEY_COLS, hd)
        s = jnp.einsum('bhrqd,bhrkd->bhrqk', q_blk, k_blk).astype(jnp.float32) * scale + bias_blk[None]
        p = jax.nn.softmax(s, axis=-1)
        return jnp.einsum('bhrqk,bhrkd->bhrqd', p, v_blk.astype(jnp.float32))

    o = lax.map(row_block, (qb, jnp.asarray(col_idx, jnp.int32), bias))
    return o.transpose(1, 2, 3, 0, 4, 5).reshape(B, H, S, hd)


def setup_inputs(seed: int = 0) -> dict:
    key = jax.random.key(seed)
    ks = jax.random.split(key, 7)
    x = jax.random.normal(ks[0], (BATCH, SEQ, D_MODEL), jnp.float32)
    col_scale = np.ones((IN_WIDTH,), np.float32)
    col_scale[2 * WIDTH_A:3 * WIDTH_A] = DEEPNORM_BETA
    col_scale[4 * WIDTH_A + 2 * WIDTH_B:4 * WIDTH_A + 3 * WIDTH_B] = DEEPNORM_BETA
    w_in = (jax.random.normal(ks[1], (DEPTH, D_MODEL, IN_WIDTH), jnp.float32)
            * D_MODEL ** -0.5 * jnp.asarray(col_scale))
    w_out = jax.random.normal(ks[2], (DEPTH, MIX_WIDTH, D_MODEL), jnp.float32) * (MIX_WIDTH ** -0.5 * DEEPNORM_BETA)
    t5_bias = jax.random.normal(ks[3], (NUM_BUCKETS, N_HEADS_A), jnp.float32) * 0.5
    na_rpb = jax.random.normal(ks[4], (DEPTH, N_HEADS_B, 2 * NA_ROWS - 1, 2 * NA_COLS - 1), jnp.float32) * 0.5
    ln_gain = 1.0 + 0.05 * jax.random.normal(ks[5], (DEPTH, D_MODEL), jnp.float32)
    ln_bias = 0.02 * jax.random.normal(ks[6], (DEPTH, D_MODEL), jnp.float32)
    return {"x": x, "w_in": w_in, "w_out": w_out, "t5_bias": t5_bias,
            "na_rpb": na_rpb, "ln_gain": ln_gain, "ln_bias": ln_bias}


def reference(x, w_in, w_out, t5_bias, na_rpb, ln_gain, ln_bias):
    B, S, _ = x.shape
    splits = np.cumsum([WIDTH_A] * 4 + [WIDTH_B] * 3)

    def heads(a, n):
        return a.reshape(B, S, n, HEAD_DIM).transpose(0, 2, 1, 3)

    def merge(o):
        return o.transpose(0, 2, 1, 3).reshape(B, S, -1)

    for layer in range(DEPTH):
        h = jnp.einsum('bsd,de->bse', x, w_in[layer])
        qa, ka, va, ga, qb, kb, vb, gb = jnp.split(h, splits, axis=-1)
        ya = _dilated_attention(heads(qa, N_HEADS_A), heads(ka, N_HEADS_A), heads(va, N_HEADS_A), t5_bias)
        yb = _neighbourhood_attention(heads(qb, N_HEADS_B), heads(kb, N_HEADS_B), heads(vb, N_HEADS_B), na_rpb[layer])
        ya = merge(ya) * jax.nn.silu(ga.astype(jnp.float32))
        yb = merge(yb) * jax.nn.silu(gb.astype(jnp.float32))
        y = jnp.concatenate([ya, yb], axis=-1).astype(x.dtype)
        out = jnp.einsum('bse,ed->bsd', y, w_out[layer])
        x = _layer_norm(DEEPNORM_ALPHA * x + out, ln_gain[layer], ln_bias[layer])
    return x
```

```python
import functools

import numpy as np
import jax
import jax.numpy as jnp
from jax import lax
from jax.experimental import pallas as pl
from jax.experimental.pallas import tpu as pltpu

D_MODEL = 2048
SEQ = 2048
HEAD_DIM = 64
N_HEADS_A = 16
N_HEADS_B = 16
WIDTH_A = N_HEADS_A * HEAD_DIM
WIDTH_B = N_HEADS_B * HEAD_DIM
IN_WIDTH = 4 * WIDTH_A + 4 * WIDTH_B
DILATIONS = (1, 4, 16)
SIDE = 64
NUM_BUCKETS = 32
T5_MAX_DISTANCE = 1024
GRID_W = 64
GRID_ROWS = SEQ // GRID_W
NA_ROWS = 8
NA_COLS = 16
DEEPNORM_ALPHA = 2.0 ** 0.25
LN_EPS = 1e-5
NEG_INF = -1e30
SCALE = HEAD_DIM ** -0.5

LANES = 128
PAIR = LANES // HEAD_DIM
QBLK = 128
KBLK = QBLK + 2 * SIDE
VMEM_LIMIT = 48 * 1024 * 1024


def _proj_kernel(x_ref, w_ref, o_ref):
    o_ref[...] = jnp.dot(x_ref[...], w_ref[...],
                         preferred_element_type=jnp.float32).astype(o_ref.dtype)


def _input_projection(xb, wb, tn=512):
    B, S, D = xb.shape
    N = wb.shape[1]
    return pl.pallas_call(
        _proj_kernel,
        grid=(B, N // tn),
        in_specs=[pl.BlockSpec((None, S, D), lambda b, n: (b, 0, 0)),
                  pl.BlockSpec((D, tn), lambda b, n: (0, n))],
        out_specs=pl.BlockSpec((None, S, tn), lambda b, n: (b, 0, n)),
        out_shape=jax.ShapeDtypeStruct((B, S, N), jnp.bfloat16),
        compiler_params=pltpu.CompilerParams(
            dimension_semantics=("arbitrary", "arbitrary"), vmem_limit_bytes=VMEM_LIMIT),
        name="in_proj",
    )(xb, wb)


def _head0_lanes(rows):
    return lax.broadcasted_iota(jnp.int32, (rows, LANES), 1) < HEAD_DIM


def _softmax_block(lhs, kb, vb, bias, lane_lt):
    nq = lhs.shape[0] // 2
    s = lax.dot_general(lhs, kb, (((1,), (1,)), ((), ())),
                        preferred_element_type=jnp.float32) + bias

    def half(sh):
        m = jnp.max(sh, axis=-1, keepdims=True)
        p = jnp.exp(sh - m)
        return p.astype(jnp.bfloat16), m, jnp.sum(p, axis=-1, keepdims=True)

    p0, m0, d0 = half(s[:nq])
    p1, m1, d1 = half(s[nq:])
    pv = jnp.dot(jnp.concatenate([p0, p1], axis=0), vb, preferred_element_type=jnp.float32)
    return (jnp.where(lane_lt, pv[:nq], pv[nq:]), jnp.where(lane_lt, m0, m1),
            jnp.where(lane_lt, d0, d1))


def _silu(g):
    return g * (1.0 / (1.0 + jnp.exp(-g)))


def _attn_a_kernel(q_ref, k_ref, v_ref, g_ref, b1_ref, b4_ref, b16_ref, y_ref,
                   f32_scr, qm_scr, kp_scr, vp_scr, num_scr, m_scr, den_scr):
    S = q_ref.shape[0]
    CH = 256

    def permute(src_ref, c, d, store):
        L = S // d
        ch = min(CH, L)
        per_r = L // ch

        def body(i, _):
            r = i // per_r
            j0 = (i % per_r) * ch
            blk = src_ref[pl.ds(r + j0 * d, ch, stride=d), :] if d > 1 else \
                src_ref[pl.ds(pl.multiple_of(i * ch, ch), ch), :]
            store(c, pl.multiple_of(i * ch, ch), ch, blk)
            return 0
        lax.fori_loop(0, S // ch, body, 0)

    def store_q(c, row0, ch, blk):
        lt = _head0_lanes(ch)
        qm_scr[c, 0, pl.ds(row0, ch), :] = jnp.where(lt, blk, 0.0).astype(jnp.bfloat16)
        qm_scr[c, 1, pl.ds(row0, ch), :] = jnp.where(lt, 0.0, blk).astype(jnp.bfloat16)

    def store_to(dst):
        def store(c, row0, ch, blk):
            dst[c - 1, pl.ds(row0, ch), :] = blk.astype(jnp.bfloat16)
        return store

    f32_scr[...] = q_ref[...].astype(jnp.float32) * SCALE
    for c, d in enumerate(DILATIONS):
        permute(f32_scr, c, d, store_q)
    f32_scr[...] = k_ref[...].astype(jnp.float32)
    for c, d in enumerate(DILATIONS[1:], start=1):
        permute(f32_scr, c, d, store_to(kp_scr))
    f32_scr[...] = v_ref[...].astype(jnp.float32)
    for c, d in enumerate(DILATIONS[1:], start=1):
        permute(f32_scr, c, d, store_to(vp_scr))

    lane_q = _head0_lanes(QBLK)

    def run_branch(c, d, bias_ref):
        L = S // d
        nblk = L // QBLK
        nk = min(KBLK, L)

        def body(i, _):
            r = i // nblk
            bi = i % nblk
            q0 = pl.multiple_of(i * QBLK, QBLK)
            lhs = jnp.concatenate([qm_scr[c, 0, pl.ds(q0, QBLK), :],
                                   qm_scr[c, 1, pl.ds(q0, QBLK), :]], axis=0)
            if nk < KBLK:
                k0 = pl.multiple_of(r * L, QBLK)
                bias = bias_ref[...]
            else:
                k0 = pl.multiple_of(r * L + jnp.clip(bi * QBLK - SIDE, 0, L - KBLK), SIDE)
                variant = jnp.where(bi == 0, 0, jnp.where(bi == nblk - 1, 2, 1))
                bias = bias_ref[variant]
            if d == 1:
                kb = k_ref[pl.ds(k0, nk), :]
                vb = v_ref[pl.ds(k0, nk), :]
            else:
                kb = kp_scr[c - 1, pl.ds(k0, nk), :]
                vb = vp_scr[c - 1, pl.ds(k0, nk), :]
            o, mb, db = _softmax_block(lhs, kb, vb, bias, lane_q)
            if d == 1:
                rows = pl.ds(q0, QBLK)
            else:
                rows = pl.ds(r + bi * QBLK * d, QBLK, stride=d)
            num_scr[c, rows, :] = o
            m_scr[c, rows, :] = mb
            den_scr[c, rows, :] = db
            return 0
        lax.fori_loop(0, S // QBLK, body, 0)

    run_branch(0, DILATIONS[0], b1_ref)
    run_branch(1, DILATIONS[1], b4_ref)
    run_branch(2, DILATIONS[2], b16_ref)

    def merge(i, _):
        rows = pl.ds(pl.multiple_of(i * CH, CH), CH)
        m0, m1, m2 = m_scr[0, rows, :], m_scr[1, rows, :], m_scr[2, rows, :]
        mm = jnp.maximum(jnp.maximum(m0, m1), m2)
        w0, w1, w2 = jnp.exp(m0 - mm), jnp.exp(m1 - mm), jnp.exp(m2 - mm)
        num = w0 * num_scr[0, rows, :] + w1 * num_scr[1, rows, :] + w2 * num_scr[2, rows, :]
        den = w0 * den_scr[0, rows, :] + w1 * den_scr[1, rows, :] + w2 * den_scr[2, rows, :]
        g = g_ref[rows, :].astype(jnp.float32)
        y_ref[rows, :] = (num / den * _silu(g)).astype(y_ref.dtype)
        return 0
    lax.fori_loop(0, S // CH, merge, 0)


def _attention_a(h, b1, b4, b16):
    B, S, _ = h.shape
    nhp = N_HEADS_A // PAIR
    blk = lambda off: pl.BlockSpec((None, S, LANES), lambda b, p: (b, 0, off + p))
    return pl.pallas_call(
        _attn_a_kernel,
        grid=(B, nhp),
        in_specs=[blk(0), blk(nhp), blk(2 * nhp), blk(3 * nhp),
                  pl.BlockSpec((None,) + b1.shape[1:], lambda b, p: (p, 0, 0, 0)),
                  pl.BlockSpec((None,) + b4.shape[1:], lambda b, p: (p, 0, 0, 0)),
                  pl.BlockSpec((None,) + b16.shape[1:], lambda b, p: (p, 0, 0))],
        out_specs=pl.BlockSpec((None, S, LANES), lambda b, p: (b, 0, p)),
        out_shape=jax.ShapeDtypeStruct((B, S, WIDTH_A), jnp.bfloat16),
        scratch_shapes=[
            pltpu.VMEM((S, LANES), jnp.float32),
            pltpu.VMEM((3, PAIR, S, LANES), jnp.bfloat16),
            pltpu.VMEM((2, S, LANES), jnp.bfloat16),
            pltpu.VMEM((2, S, LANES), jnp.bfloat16),
            pltpu.VMEM((3, S, LANES), jnp.float32),
            pltpu.VMEM((3, S, LANES), jnp.float32),
            pltpu.VMEM((3, S, LANES), jnp.float32),
        ],
        compiler_params=pltpu.CompilerParams(
            dimension_semantics=("arbitrary", "arbitrary"), vmem_limit_bytes=VMEM_LIMIT),
        name="attn_dilated",
    )(h, h, h, h, b1, b4, b16)


def _attn_b_kernel(q_ref, k_ref, v_ref, g_ref, bias_ref, y_ref, qm_scr):
    S = q_ref.shape[0]
    CH = 256
    lane_lt = _head0_lanes(CH)

    def prep(i, _):
        rows = pl.ds(pl.multiple_of(i * CH, CH), CH)
        blk = q_ref[rows, :].astype(jnp.float32) * SCALE
        qm_scr[0, rows, :] = jnp.where(lane_lt, blk, 0.0).astype(jnp.bfloat16)
        qm_scr[1, rows, :] = jnp.where(lane_lt, 0.0, blk).astype(jnp.bfloat16)
        return 0
    lax.fori_loop(0, S // CH, prep, 0)

    lane_q = _head0_lanes(GRID_W)
    nk = NA_ROWS * GRID_W
    last = GRID_ROWS - NA_ROWS

    def body(r, _):
        rows = pl.ds(pl.multiple_of(r * GRID_W, GRID_W), GRID_W)
        rs = jnp.clip(r - NA_ROWS // 2, 0, last)
        variant = jnp.where(r < NA_ROWS // 2, r,
                            jnp.where(r <= last + NA_ROWS // 2, NA_ROWS // 2, r - last))
        lhs = jnp.concatenate([qm_scr[0, rows, :], qm_scr[1, rows, :]], axis=0)
        k0 = pl.multiple_of(rs * GRID_W, GRID_W)
        o, _, db = _softmax_block(lhs, k_ref[pl.ds(k0, nk), :], v_ref[pl.ds(k0, nk), :],
                                  bias_ref[variant], lane_q)
        g = g_ref[rows, :].astype(jnp.float32)
        y_ref[rows, :] = (o / db * _silu(g)).astype(y_ref.dtype)
        return 0
    lax.fori_loop(0, GRID_ROWS, body, 0)


def _attention_b(h, bias_b):
    B, S, _ = h.shape
    nhp = N_HEADS_B // PAIR
    base = 4 * WIDTH_A // LANES
    blk = lambda off: pl.BlockSpec((None, S, LANES), lambda b, p: (b, 0, base + off + p))
    return pl.pallas_call(
        _attn_b_kernel,
        grid=(B, nhp),
        in_specs=[blk(0), blk(nhp), blk(2 * nhp), blk(3 * nhp),
                  pl.BlockSpec((None,) + bias_b.shape[1:], lambda b, p: (p, 0, 0, 0))],
        out_specs=pl.BlockSpec((None, S, LANES), lambda b, p: (b, 0, p)),
        out_shape=jax.ShapeDtypeStruct((B, S, WIDTH_B), jnp.bfloat16),
        scratch_shapes=[pltpu.VMEM((PAIR, S, LANES), jnp.bfloat16)],
        compiler_params=pltpu.CompilerParams(
            dimension_semantics=("arbitrary", "arbitrary"), vmem_limit_bytes=VMEM_LIMIT),
        name="attn_neighbourhood",
    )(h, h, h, h, bias_b)


def _out_kernel(ya_ref, yb_ref, wa_ref, wb_ref, x_ref, gain_ref, bias_ref, o_ref):
    out = jnp.dot(ya_ref[...], wa_ref[...], preferred_element_type=jnp.float32)
    out = out + jnp.dot(yb_ref[...], wb_ref[...], preferred_element_type=jnp.float32)
    z = DEEPNORM_ALPHA * x_ref[...] + out
    mu = jnp.mean(z, axis=-1, keepdims=True)
    zc = z - mu
    var = jnp.mean(zc * zc, axis=-1, keepdims=True)
    o_ref[...] = zc * lax.rsqrt(var + LN_EPS) * gain_ref[...] + bias_ref[...]


def _output_projection(ya, yb, wa, wb, x2, gain, bias, tm=256):
    M, D = x2.shape
    return pl.pallas_call(
        _out_kernel,
        grid=(M // tm,),
        in_specs=[pl.BlockSpec((tm, WIDTH_A), lambda i: (i, 0)),
                  pl.BlockSpec((tm, WIDTH_B), lambda i: (i, 0)),
                  pl.BlockSpec((WIDTH_A, D), lambda i: (0, 0)),
                  pl.BlockSpec((WIDTH_B, D), lambda i: (0, 0)),
                  pl.BlockSpec((tm, D), lambda i: (i, 0)),
                  pl.BlockSpec((1, D), lambda i: (0, 0)),
                  pl.BlockSpec((1, D), lambda i: (0, 0))],
        out_specs=pl.BlockSpec((tm, D), lambda i: (i, 0)),
        out_shape=jax.ShapeDtypeStruct((M, D), jnp.float32),
        compiler_params=pltpu.CompilerParams(
            dimension_semantics=("arbitrary",), vmem_limit_bytes=VMEM_LIMIT),
        name="out_proj_ln",
    )(ya, yb, wa, wb, x2, gain, bias)


def _t5_bucket_index(rel):
    half = NUM_BUCKETS // 2
    max_exact = half // 2
    n = np.abs(rel)
    large = max_exact + (np.log(np.maximum(n, 1) / max_exact)
                         / np.log(T5_MAX_DISTANCE / max_exact)
                         * (half - max_exact)).astype(np.int64)
    large = np.minimum(large, half - 1)
    return ((rel > 0).astype(np.int64) * half + np.where(n < max_exact, n, large)).astype(np.int32)


def _pair_rows(a):
    H = a.shape[0]
    a = a.reshape((H // PAIR, PAIR) + a.shape[1:])
    a = jnp.moveaxis(a, 1, -3)
    return a.reshape(a.shape[:-3] + (PAIR * a.shape[-2], a.shape[-1]))


def _dilated_bias(t5_table, d, L):
    q = np.arange(QBLK)[:, None]
    if L < KBLK:
        offsets, nk = (0,), L
    else:
        offsets, nk = (0, -SIDE, -2 * SIDE), KBLK
    out = []
    for off in offsets:
        rel = off + np.arange(nk)[None, :] - q
        vals = jnp.moveaxis(jnp.take(t5_table, _t5_bucket_index(rel * d), axis=0), -1, 0)
        out.append(jnp.where(np.abs(rel) <= SIDE, vals, NEG_INF))
    bias = jnp.stack(out, axis=1)
    bias = _pair_rows(bias)
    return bias[:, 0] if L < KBLK else bias


def _neighbourhood_bias(rpb):
    half = NA_ROWS // 2
    last = GRID_ROWS - NA_ROWS
    rows = np.array(list(range(half)) + [half] + list(range(last + half + 1, GRID_ROWS)))
    rs = np.clip(rows - half, 0, last)
    ri = rs[:, None] + np.arange(NA_ROWS)[None, :] - rows[:, None] + NA_ROWS - 1
    qc = np.arange(GRID_W)
    kc = np.arange(GRID_W)
    qs = np.clip(qc - NA_COLS // 2, 0, GRID_W - NA_COLS)
    valid = (kc[None, :] >= qs[:, None]) & (kc[None, :] < qs[:, None] + NA_COLS)
    ci = np.clip(kc[None, :] - qc[:, None], -(NA_COLS - 1), NA_COLS - 1) + NA_COLS - 1
    vals = rpb[:, ri[:, None, :, None], ci[None, :, None, :]]
    bias = jnp.where(valid[None, None, :, None, :], vals, NEG_INF)
    H, V = bias.shape[:2]
    return _pair_rows(bias.reshape(H, V, GRID_W, NA_ROWS * GRID_W))


def kernel(x, w_in, w_out, t5_bias, na_rpb, ln_gain, ln_bias):
    B, S, D = x.shape
    depth = w_in.shape[0]
    t5 = t5_bias.astype(jnp.float32)
    b1 = _dilated_bias(t5, DILATIONS[0], S // DILATIONS[0])
    b4 = _dilated_bias(t5, DILATIONS[1], S // DILATIONS[1])
    b16 = _dilated_bias(t5, DILATIONS[2], S // DILATIONS[2])
    for layer in range(depth):
        h = _input_projection(x.astype(jnp.bfloat16), w_in[layer].astype(jnp.bfloat16))
        ya = _attention_a(h, b1, b4, b16)
        yb = _attention_b(h, _neighbourhood_bias(na_rpb[layer].astype(jnp.float32)))
        wo = w_out[layer].astype(jnp.bfloat16)
        out = _output_projection(ya.reshape(B * S, WIDTH_A), yb.reshape(B * S, WIDTH_B),
                                 wo[:WIDTH_A], wo[WIDTH_A:], x.reshape(B * S, D),
                                 ln_gain[layer].reshape(1, D), ln_bias[layer].reshape(1, D))
        x = out.reshape(B, S, D)
    return x
```

```python
import functools

import numpy as np
import jax
import jax.numpy as jnp
from jax import lax
from jax.experimental import pallas as pl
from jax.experimental.pallas import tpu as pltpu

D_MODEL = 2048
SEQ = 2048
HEAD_DIM = 64
N_HEADS_A = 16
N_HEADS_B = 16
WIDTH_A = N_HEADS_A * HEAD_DIM
WIDTH_B = N_HEADS_B * HEAD_DIM
IN_WIDTH = 4 * WIDTH_A + 4 * WIDTH_B
DILATIONS = (1, 4, 16)
SIDE = 64
NUM_BUCKETS = 32
T5_MAX_DISTANCE = 1024
GRID_W = 64
GRID_ROWS = SEQ // GRID_W
NA_ROWS = 8
NA_COLS = 16
DEEPNORM_ALPHA = 2.0 ** 0.25
LN_EPS = 1e-5
NEG_INF = -1e30
SCALE = HEAD_DIM ** -0.5

LANES = 128
PAIR = LANES // HEAD_DIM
QBLK = 128
KBLK = QBLK + 2 * SIDE
VMEM_LIMIT = 48 * 1024 * 1024


def _proj_kernel(x_ref, w_ref, o_ref):
    o_ref[...] = jnp.dot(x_ref[...], w_ref[...],
                         preferred_element_type=jnp.float32).astype(o_ref.dtype)


def _input_projection(xb, wb, tn=512):
    B, S, D = xb.shape
    N = wb.shape[1]
    return pl.pallas_call(
        _proj_kernel,
        grid=(B, N // tn),
        in_specs=[pl.BlockSpec((None, S, D), lambda b, n: (b, 0, 0)),
                  pl.BlockSpec((D, tn), lambda b, n: (0, n))],
        out_specs=pl.BlockSpec((None, S, tn), lambda b, n: (b, 0, n)),
        out_shape=jax.ShapeDtypeStruct((B, S, N), jnp.bfloat16),
        compiler_params=pltpu.CompilerParams(
            dimension_semantics=("arbitrary", "arbitrary"), vmem_limit_bytes=VMEM_LIMIT),
        name="in_proj",
    )(xb, wb)


def _head0_lanes(rows):
    return lax.broadcasted_iota(jnp.int32, (rows, LANES), 1) < HEAD_DIM


def _softmax_block(lhs, kb, vb, bias, lane_lt):
    nq = lhs.shape[0] // 2
    s = lax.dot_general(lhs, kb, (((1,), (1,)), ((), ())),
                        preferred_element_type=jnp.float32) + bias

    def half(sh):
        m = jnp.max(sh, axis=-1, keepdims=True)
        p = jnp.exp(sh - m)
        return p.astype(jnp.bfloat16), m, jnp.sum(p, axis=-1, keepdims=True)

    p0, m0, d0 = half(s[:nq])
    p1, m1, d1 = half(s[nq:])
    pv = jnp.dot(jnp.concatenate([p0, p1], axis=0), vb, preferred_element_type=jnp.float32)
    return (jnp.where(lane_lt, pv[:nq], pv[nq:]), jnp.where(lane_lt, m0, m1),
            jnp.where(lane_lt, d0, d1))


def _silu(g):
    return g * (1.0 / (1.0 + jnp.exp(-g)))


def _attn_a_kernel(q_ref, k_ref, v_ref, g_ref, b1_ref, b4_ref, b16_ref, y_ref,
                   f32_scr, qm_scr, kp_scr, vp_scr, num_scr, m_scr, den_scr):
    S = q_ref.shape[0]
    CH = 256

    def permute(src_ref, c, d, store):
        L = S // d
        ch = min(CH, L)
        per_r = L // ch

        def body(i, _):
            r = i // per_r
            j0 = (i % per_r) * ch
            blk = src_ref[pl.ds(r + j0 * d, ch, stride=d), :] if d > 1 else \
                src_ref[pl.ds(pl.multiple_of(i * ch, ch), ch), :]
            store(c, pl.multiple_of(i * ch, ch), ch, blk)
            return 0
        lax.fori_loop(0, S // ch, body, 0)

    def store_q(c, row0, ch, blk):
        lt = _head0_lanes(ch)
        qm_scr[c, 0, pl.ds(row0, ch), :] = jnp.where(lt, blk, 0.0).astype(jnp.bfloat16)
        qm_scr[c, 1, pl.ds(row0, ch), :] = jnp.where(lt, 0.0, blk).astype(jnp.bfloat16)

    def store_to(dst):
        def store(c, row0, ch, blk):
            dst[c - 1, pl.ds(row0, ch), :] = blk.astype(jnp.bfloat16)
        return store

    f32_scr[...] = q_ref[...].astype(jnp.float32) * SCALE
    for c, d in enumerate(DILATIONS):
        permute(f32_scr, c, d, store_q)
    f32_scr[...] = k_ref[...].astype(jnp.float32)
    for c, d in enumerate(DILATIONS[1:], start=1):
        permute(f32_scr, c, d, store_to(kp_scr))
    f32_scr[...] = v_ref[...].astype(jnp.float32)
    for c, d in enumerate(DILATIONS[1:], start=1):
        permute(f32_scr, c, d, store_to(vp_scr))

    lane_q = _head0_lanes(QBLK)

    def run_branch(c, d, bias_ref):
        L = S // d
        nblk = L // QBLK
        nk = min(KBLK, L)

        def body(i, _):
            r = i // nblk
            bi = i % nblk
            q0 = pl.multiple_of(i * QBLK, QBLK)
            lhs = jnp.concatenate([qm_scr[c, 0, pl.ds(q0, QBLK), :],
                                   qm_scr[c, 1, pl.ds(q0, QBLK), :]], axis=0)
            if nk < KBLK:
                k0 = pl.multiple_of(r * L, QBLK)
                bias = bias_ref[0]
            else:
                k0 = pl.multiple_of(r * L + jnp.clip(bi * QBLK - SIDE, 0, L - KBLK), SIDE)
                variant = jnp.where(bi == 0, 0, jnp.where(bi == nblk - 1, 2, 1))
                bias = bias_ref[variant]
            if d == 1:
                kb = k_ref[pl.ds(k0, nk), :]
                vb = v_ref[pl.ds(k0, nk), :]
            else:
                kb = kp_scr[c - 1, pl.ds(k0, nk), :]
                vb = vp_scr[c - 1, pl.ds(k0, nk), :]
            o, mb, db = _softmax_block(lhs, kb, vb, bias, lane_q)
            if d == 1:
                rows = pl.ds(q0, QBLK)
            else:
                rows = pl.ds(r + bi * QBLK * d, QBLK, stride=d)
            num_scr[c, rows, :] = o
            m_scr[c, rows, :] = mb
            den_scr[c, rows, :] = db
            return 0
        lax.fori_loop(0, S // QBLK, body, 0)

    run_branch(0, DILATIONS[0], b1_ref)
    run_branch(1, DILATIONS[1], b4_ref)
    run_branch(2, DILATIONS[2], b16_ref)

    def merge(i, _):
        rows = pl.ds(pl.multiple_of(i * CH, CH), CH)
        m0, m1, m2 = m_scr[0, rows, :], m_scr[1, rows, :], m_scr[2, rows, :]
        mm = jnp.maximum(jnp.maximum(m0, m1), m2)
        w0, w1, w2 = jnp.exp(m0 - mm), jnp.exp(m1 - mm), jnp.exp(m2 - mm)
        num = w0 * num_scr[0, rows, :] + w1 * num_scr[1, rows, :] + w2 * num_scr[2, rows, :]
        den = w0 * den_scr[0, rows, :] + w1 * den_scr[1, rows, :] + w2 * den_scr[2, rows, :]
        g = g_ref[rows, :].astype(jnp.float32)
        y_ref[rows, :] = (num / den * _silu(g)).astype(y_ref.dtype)
        return 0
    lax.fori_loop(0, S // CH, merge, 0)


def _attention_a(h, b1, b4, b16):
    B, S, _ = h.shape
    nhp = N_HEADS_A // PAIR
    blk = lambda off: pl.BlockSpec((None, S, LANES), lambda p, b: (b, 0, off + p))
    tab = lambda t: pl.BlockSpec((None,) + t.shape[1:], lambda p, b: (p, 0, 0, 0))
    return pl.pallas_call(
        _attn_a_kernel,
        grid=(nhp, B),
        in_specs=[blk(0), blk(nhp), blk(2 * nhp), blk(3 * nhp), tab(b1), tab(b4), tab(b16)],
        out_specs=pl.BlockSpec((None, S, LANES), lambda p, b: (b, 0, p)),
        out_shape=jax.ShapeDtypeStruct((B, S, WIDTH_A), jnp.bfloat16),
        scratch_shapes=[
            pltpu.VMEM((S, LANES), jnp.float32),
            pltpu.VMEM((3, PAIR, S, LANES), jnp.bfloat16),
            pltpu.VMEM((2, S, LANES), jnp.bfloat16),
            pltpu.VMEM((2, S, LANES), jnp.bfloat16),
            pltpu.VMEM((3, S, LANES), jnp.float32),
            pltpu.VMEM((3, S, LANES), jnp.float32),
            pltpu.VMEM((3, S, LANES), jnp.float32),
        ],
        compiler_params=pltpu.CompilerParams(
            dimension_semantics=("arbitrary", "arbitrary"), vmem_limit_bytes=VMEM_LIMIT),
        name="attn_dilated",
    )(h, h, h, h, b1, b4, b16)


def _attn_b_kernel(q_ref, k_ref, v_ref, g_ref, bias_ref, y_ref, qm_scr):
    S = q_ref.shape[0]
    CH = 256
    lane_lt = _head0_lanes(CH)

    def prep(i, _):
        rows = pl.ds(pl.multiple_of(i * CH, CH), CH)
        blk = q_ref[rows, :].astype(jnp.float32) * SCALE
        qm_scr[0, rows, :] = jnp.where(lane_lt, blk, 0.0).astype(jnp.bfloat16)
        qm_scr[1, rows, :] = jnp.where(lane_lt, 0.0, blk).astype(jnp.bfloat16)
        return 0
    lax.fori_loop(0, S // CH, prep, 0)

    lane_q = _head0_lanes(GRID_W)
    nk = NA_ROWS * GRID_W
    last = GRID_ROWS - NA_ROWS

    def body(r, _):
        rows = pl.ds(pl.multiple_of(r * GRID_W, GRID_W), GRID_W)
        rs = jnp.clip(r - NA_ROWS // 2, 0, last)
        variant = jnp.where(r < NA_ROWS // 2, r,
                            jnp.where(r <= last + NA_ROWS // 2, NA_ROWS // 2, r - last))
        lhs = jnp.concatenate([qm_scr[0, rows, :], qm_scr[1, rows, :]], axis=0)
        k0 = pl.multiple_of(rs * GRID_W, GRID_W)
        o, _, db = _softmax_block(lhs, k_ref[pl.ds(k0, nk), :], v_ref[pl.ds(k0, nk), :],
                                  bias_ref[variant], lane_q)
        g = g_ref[rows, :].astype(jnp.float32)
        y_ref[rows, :] = (o / db * _silu(g)).astype(y_ref.dtype)
        return 0
    lax.fori_loop(0, GRID_ROWS, body, 0)


def _attention_b(h, bias_b):
    B, S, _ = h.shape
    nhp = N_HEADS_B // PAIR
    base = 4 * WIDTH_A // LANES
    blk = lambda off: pl.BlockSpec((None, S, LANES), lambda p, b: (b, 0, base + off + p))
    return pl.pallas_call(
        _attn_b_kernel,
        grid=(nhp, B),
        in_specs=[blk(0), blk(nhp), blk(2 * nhp), blk(3 * nhp),
                  pl.BlockSpec((None,) + bias_b.shape[1:], lambda p, b: (p, 0, 0, 0))],
        out_specs=pl.BlockSpec((None, S, LANES), lambda p, b: (b, 0, p)),
        out_shape=jax.ShapeDtypeStruct((B, S, WIDTH_B), jnp.bfloat16),
        scratch_shapes=[pltpu.VMEM((PAIR, S, LANES), jnp.bfloat16)],
        compiler_params=pltpu.CompilerParams(
            dimension_semantics=("arbitrary", "arbitrary"), vmem_limit_bytes=VMEM_LIMIT),
        name="attn_neighbourhood",
    )(h, h, h, h, bias_b)


def _out_kernel(ya_ref, yb_ref, wa_ref, wb_ref, x_ref, gain_ref, bias_ref, o_ref):
    out = jnp.dot(ya_ref[...], wa_ref[...], preferred_element_type=jnp.float32)
    out = out + jnp.dot(yb_ref[...], wb_ref[...], preferred_element_type=jnp.float32)
    z = DEEPNORM_ALPHA * x_ref[...] + out
    mu = jnp.mean(z, axis=-1, keepdims=True)
    zc = z - mu
    var = jnp.mean(zc * zc, axis=-1, keepdims=True)
    o_ref[...] = zc * lax.rsqrt(var + LN_EPS) * gain_ref[...] + bias_ref[...]


def _output_projection(ya, yb, wa, wb, x2, gain, bias, tm=256):
    M, D = x2.shape
    return pl.pallas_call(
        _out_kernel,
        grid=(M // tm,),
        in_specs=[pl.BlockSpec((tm, WIDTH_A), lambda i: (i, 0)),
                  pl.BlockSpec((tm, WIDTH_B), lambda i: (i, 0)),
                  pl.BlockSpec((WIDTH_A, D), lambda i: (0, 0)),
                  pl.BlockSpec((WIDTH_B, D), lambda i: (0, 0)),
                  pl.BlockSpec((tm, D), lambda i: (i, 0)),
                  pl.BlockSpec((1, D), lambda i: (0, 0)),
                  pl.BlockSpec((1, D), lambda i: (0, 0))],
        out_specs=pl.BlockSpec((tm, D), lambda i: (i, 0)),
        out_shape=jax.ShapeDtypeStruct((M, D), jnp.float32),
        compiler_params=pltpu.CompilerParams(
            dimension_semantics=("arbitrary",), vmem_limit_bytes=VMEM_LIMIT),
        name="out_proj_ln",
    )(ya, yb, wa, wb, x2, gain, bias)


def _t5_bucket_index(rel):
    half = NUM_BUCKETS // 2
    max_exact = half // 2
    n = np.abs(rel)
    large = max_exact + (np.log(np.maximum(n, 1) / max_exact)
                         / np.log(T5_MAX_DISTANCE / max_exact)
                         * (half - max_exact)).astype(np.int64)
    large = np.minimum(large, half - 1)
    return ((rel > 0).astype(np.int64) * half + np.where(n < max_exact, n, large)).astype(np.int32)


def _dilated_bucket_tables():
    tables = []
    q = np.arange(QBLK)[:, None]
    for d in DILATIONS:
        L = SEQ // d
        offsets, nk = ((0,), L) if L < KBLK else ((0, -SIDE, -2 * SIDE), KBLK)
        variants = []
        for off in offsets:
            rel = off + np.arange(nk)[None, :] - q
            variants.append(np.where(np.abs(rel) <= SIDE, _t5_bucket_index(rel * d), -1))
        tables.append(np.stack(variants).astype(np.int32))
    return tables


def _dilated_bias_kernel(t5_ref, i1_ref, i4_ref, i16_ref, b1_ref, b4_ref, b16_ref):
    p = pl.program_id(0)
    for idx_ref, out_ref in ((i1_ref, b1_ref), (i4_ref, b4_ref), (i16_ref, b16_ref)):
        for v in range(idx_ref.shape[0]):
            idx = idx_ref[v]
            for hh in range(PAIR):
                head = p * PAIR + hh

                def pick(bkt, acc):
                    return jnp.where(idx == bkt, t5_ref[bkt, head], acc)
                acc = lax.fori_loop(0, NUM_BUCKETS, pick,
                                    jnp.full(idx.shape, NEG_INF, jnp.float32))
                out_ref[v, hh * QBLK:(hh + 1) * QBLK, :] = acc


def _dilated_bias(t5_table):
    tables = [jnp.asarray(t) for t in _dilated_bucket_tables()]
    nhp = N_HEADS_A // PAIR
    whole = lambda t: pl.BlockSpec(t.shape, lambda p: (0, 0, 0))
    out_shapes = [jax.ShapeDtypeStruct((nhp, t.shape[0], PAIR * QBLK, t.shape[2]), jnp.float32)
                  for t in tables]
    return pl.pallas_call(
        _dilated_bias_kernel,
        grid=(nhp,),
        in_specs=[pl.BlockSpec(memory_space=pltpu.SMEM)] + [whole(t) for t in tables],
        out_specs=[pl.BlockSpec((None,) + o.shape[1:], lambda p: (p, 0, 0, 0)) for o in out_shapes],
        out_shape=out_shapes,
        compiler_params=pltpu.CompilerParams(dimension_semantics=("arbitrary",)),
        name="dilated_bias",
    )(t5_table, *tables)


RPB_ROWS = 2 * NA_ROWS - 1
RPB_COLS = 2 * NA_COLS - 1
NBR_VARIANT_ROWS = (tuple(range(NA_ROWS // 2)) + (NA_ROWS // 2,)
                    + tuple(range(GRID_ROWS - NA_ROWS // 2 + 1, GRID_ROWS)))


def _nbr_bias_kernel(rpb_ref, out_ref, m_scr):
    p = pl.program_id(0)
    shape = (GRID_W, LANES)
    qc = lax.broadcasted_iota(jnp.int32, shape, 0)
    lane = lax.broadcasted_iota(jnp.int32, shape, 1)
    kc = lane & (GRID_W - 1)
    qs = jnp.clip(qc - NA_COLS // 2, 0, GRID_W - NA_COLS)
    ci = jnp.where((kc >= qs) & (kc < qs + NA_COLS), kc - qc + NA_COLS - 1, -1)
    first = lane < GRID_W
    for hh in range(PAIR):
        head = p * PAIR + hh

        def fill_row(ri, _):
            base = (head * RPB_ROWS + ri) * RPB_COLS

            def pick(t, acc):
                return jnp.where(ci == t, rpb_ref[base + t], acc)
            m_scr[ri] = lax.fori_loop(0, RPB_COLS, pick, jnp.full(shape, NEG_INF, jnp.float32))
            return 0
        lax.fori_loop(0, RPB_ROWS, fill_row, 0)
        for v, r in enumerate(NBR_VARIANT_ROWS):
            rs = min(max(r - NA_ROWS // 2, 0), GRID_ROWS - NA_ROWS)
            for j2 in range(NA_ROWS * GRID_W // LANES):
                ri = rs + 2 * j2 - r + NA_ROWS - 1
                out_ref[v, hh * GRID_W:(hh + 1) * GRID_W, j2 * LANES:(j2 + 1) * LANES] = \
                    jnp.where(first, m_scr[ri], m_scr[ri + 1])


def _neighbourhood_bias(rpb):
    nhp = N_HEADS_B // PAIR
    nv = len(NBR_VARIANT_ROWS)
    out = jax.ShapeDtypeStruct((nhp, nv, PAIR * GRID_W, NA_ROWS * GRID_W), jnp.float32)
    return pl.pallas_call(
        _nbr_bias_kernel,
        grid=(nhp,),
        in_specs=[pl.BlockSpec(memory_space=pltpu.SMEM)],
        out_specs=pl.BlockSpec((None,) + out.shape[1:], lambda p: (p, 0, 0, 0)),
        out_shape=out,
        scratch_shapes=[pltpu.VMEM((RPB_ROWS, GRID_W, LANES), jnp.float32)],
        compiler_params=pltpu.CompilerParams(dimension_semantics=("arbitrary",)),
        name="nbr_bias",
    )(rpb.reshape(-1))


def kernel(x, w_in, w_out, t5_bias, na_rpb, ln_gain, ln_bias):
    B, S, D = x.shape
    depth = w_in.shape[0]
    b1, b4, b16 = _dilated_bias(t5_bias.astype(jnp.float32))
    for layer in range(depth):
        h = _input_projection(x.astype(jnp.bfloat16), w_in[layer].astype(jnp.bfloat16))
        ya = _attention_a(h, b1, b4, b16)
        yb = _attention_b(h, _neighbourhood_bias(na_rpb[layer].astype(jnp.float32)))
        wo = w_out[layer].astype(jnp.bfloat16)
        out = _output_projection(ya.reshape(B * S, WIDTH_A), yb.reshape(B * S, WIDTH_B),
                                 wo[:WIDTH_A], wo[WIDTH_A:], x.reshape(B * S, D),
                                 ln_gain[layer].reshape(1, D), ln_bias[layer].reshape(1, D))
        x = out.reshape(B, S, D)
    return x
```

```python
import functools

import numpy as np
import jax
import jax.numpy as jnp
from jax import lax
from jax.experimental import pallas as pl
from jax.experimental.pallas import tpu as pltpu

D_MODEL = 2048
SEQ = 2048
HEAD_DIM = 64
N_HEADS_A = 16
N_HEADS_B = 16
WIDTH_A = N_HEADS_A * HEAD_DIM
WIDTH_B = N_HEADS_B * HEAD_DIM
IN_WIDTH = 4 * WIDTH_A + 4 * WIDTH_B
DILATIONS = (1, 4, 16)
SIDE = 64
NUM_BUCKETS = 32
T5_MAX_DISTANCE = 1024
GRID_W = 64
GRID_ROWS = SEQ // GRID_W
NA_ROWS = 8
NA_COLS = 16
DEEPNORM_ALPHA = 2.0 ** 0.25
LN_EPS = 1e-5
NEG_INF = -1e30
SCALE = HEAD_DIM ** -0.5

LANES = 128
PAIR = LANES // HEAD_DIM
QBLK = 128
KBLK = QBLK + 2 * SIDE
BLOCK_UNROLL = 4
VMEM_LIMIT = 48 * 1024 * 1024


def _proj_kernel(x_ref, w_ref, o_ref):
    o_ref[...] = jnp.dot(x_ref[...], w_ref[...],
                         preferred_element_type=jnp.float32).astype(o_ref.dtype)


def _input_projection(xb, wb, tn=512):
    B, S, D = xb.shape
    N = wb.shape[1]
    return pl.pallas_call(
        _proj_kernel,
        grid=(B, N // tn),
        in_specs=[pl.BlockSpec((None, S, D), lambda b, n: (b, 0, 0)),
                  pl.BlockSpec((D, tn), lambda b, n: (0, n))],
        out_specs=pl.BlockSpec((None, S, tn), lambda b, n: (b, 0, n)),
        out_shape=jax.ShapeDtypeStruct((B, S, N), jnp.bfloat16),
        compiler_params=pltpu.CompilerParams(
            dimension_semantics=("arbitrary", "arbitrary"), vmem_limit_bytes=VMEM_LIMIT),
        name="in_proj",
    )(xb, wb)


def _head0_lanes(rows):
    return lax.broadcasted_iota(jnp.int32, (rows, LANES), 1) < HEAD_DIM


def _softmax_block(lhs, kb, vb, bias, lane_lt):
    nq = lhs.shape[0] // 2
    s = lax.dot_general(lhs, kb, (((1,), (1,)), ((), ())),
                        preferred_element_type=jnp.float32) + bias

    def half(sh):
        m = jnp.max(sh, axis=-1, keepdims=True)
        return jnp.exp(sh - m).astype(jnp.bfloat16), m

    p0, m0 = half(s[:nq])
    p1, m1 = half(s[nq:])
    v_ones = jnp.concatenate([vb, jnp.ones_like(vb)], axis=1)
    pv = jnp.dot(jnp.concatenate([p0, p1], axis=0), v_ones, preferred_element_type=jnp.float32)
    return (jnp.where(lane_lt, pv[:nq, :LANES], pv[nq:, :LANES]), jnp.where(lane_lt, m0, m1),
            jnp.where(lane_lt, pv[:nq, LANES:], pv[nq:, LANES:]))


def _silu(g):
    return g * (1.0 / (1.0 + jnp.exp(-g)))


def _attn_a_kernel(q_ref, k_ref, v_ref, g_ref, b1_ref, b4_ref, b16_ref, y_ref,
                   f32_scr, qm_scr, kp_scr, vp_scr, num_scr, m_scr, den_scr):
    S = q_ref.shape[0]
    CH = 256

    def permute(src_ref, c, d, store):
        L = S // d
        ch = min(CH, L)
        per_r = L // ch

        def body(i, _):
            r = i // per_r
            j0 = (i % per_r) * ch
            blk = src_ref[pl.ds(r + j0 * d, ch, stride=d), :] if d > 1 else \
                src_ref[pl.ds(pl.multiple_of(i * ch, ch), ch), :]
            store(c, pl.multiple_of(i * ch, ch), ch, blk)
            return 0
        lax.fori_loop(0, S // ch, body, 0)

    def store_q(c, row0, ch, blk):
        lt = _head0_lanes(ch)
        qm_scr[c, 0, pl.ds(row0, ch), :] = jnp.where(lt, blk, 0.0).astype(jnp.bfloat16)
        qm_scr[c, 1, pl.ds(row0, ch), :] = jnp.where(lt, 0.0, blk).astype(jnp.bfloat16)

    def store_to(dst):
        def store(c, row0, ch, blk):
            dst[c - 1, pl.ds(row0, ch), :] = blk.astype(jnp.bfloat16)
        return store

    f32_scr[...] = q_ref[...].astype(jnp.float32) * SCALE
    for c, d in enumerate(DILATIONS):
        permute(f32_scr, c, d, store_q)
    f32_scr[...] = k_ref[...].astype(jnp.float32)
    for c, d in enumerate(DILATIONS[1:], start=1):
        permute(f32_scr, c, d, store_to(kp_scr))
    f32_scr[...] = v_ref[...].astype(jnp.float32)
    for c, d in enumerate(DILATIONS[1:], start=1):
        permute(f32_scr, c, d, store_to(vp_scr))

    lane_q = _head0_lanes(QBLK)

    def run_branch(c, d, bias_ref):
        L = S // d
        nblk = L // QBLK
        nk = min(KBLK, L)

        def body(i, _):
            r = i // nblk
            bi = i % nblk
            q0 = pl.multiple_of(i * QBLK, QBLK)
            lhs = jnp.concatenate([qm_scr[c, 0, pl.ds(q0, QBLK), :],
                                   qm_scr[c, 1, pl.ds(q0, QBLK), :]], axis=0)
            if nk < KBLK:
                k0 = pl.multiple_of(r * L, QBLK)
                bias = bias_ref[0]
            else:
                k0 = pl.multiple_of(r * L + jnp.clip(bi * QBLK - SIDE, 0, L - KBLK), SIDE)
                variant = jnp.where(bi == 0, 0, jnp.where(bi == nblk - 1, 2, 1))
                bias = bias_ref[variant]
            if d == 1:
                kb = k_ref[pl.ds(k0, nk), :]
                vb = v_ref[pl.ds(k0, nk), :]
            else:
                kb = kp_scr[c - 1, pl.ds(k0, nk), :]
                vb = vp_scr[c - 1, pl.ds(k0, nk), :]
            o, mb, db = _softmax_block(lhs, kb, vb, bias, lane_q)
            if d == 1:
                rows = pl.ds(q0, QBLK)
            else:
                rows = pl.ds(r + bi * QBLK * d, QBLK, stride=d)
            num_scr[c, rows, :] = o
            m_scr[c, rows, :] = mb
            den_scr[c, rows, :] = db
            return 0
        lax.fori_loop(0, S // QBLK, body, 0, unroll=BLOCK_UNROLL)

    run_branch(0, DILATIONS[0], b1_ref)
    run_branch(1, DILATIONS[1], b4_ref)
    run_branch(2, DILATIONS[2], b16_ref)

    def merge(i, _):
        rows = pl.ds(pl.multiple_of(i * CH, CH), CH)
        m0, m1, m2 = m_scr[0, rows, :], m_scr[1, rows, :], m_scr[2, rows, :]
        mm = jnp.maximum(jnp.maximum(m0, m1), m2)
        w0, w1, w2 = jnp.exp(m0 - mm), jnp.exp(m1 - mm), jnp.exp(m2 - mm)
        num = w0 * num_scr[0, rows, :] + w1 * num_scr[1, rows, :] + w2 * num_scr[2, rows, :]
        den = w0 * den_scr[0, rows, :] + w1 * den_scr[1, rows, :] + w2 * den_scr[2, rows, :]
        g = g_ref[rows, :].astype(jnp.float32)
        y_ref[rows, :] = (num / den * _silu(g)).astype(y_ref.dtype)
        return 0
    lax.fori_loop(0, S // CH, merge, 0)


def _attention_a(h, b1, b4, b16):
    B, S, _ = h.shape
    nhp = N_HEADS_A // PAIR
    blk = lambda off: pl.BlockSpec((None, S, LANES), lambda p, b: (b, 0, off + p))
    tab = lambda t: pl.BlockSpec((None,) + t.shape[1:], lambda p, b: (p, 0, 0, 0))
    return pl.pallas_call(
        _attn_a_kernel,
        grid=(nhp, B),
        in_specs=[blk(0), blk(nhp), blk(2 * nhp), blk(3 * nhp), tab(b1), tab(b4), tab(b16)],
        out_specs=pl.BlockSpec((None, S, LANES), lambda p, b: (b, 0, p)),
        out_shape=jax.ShapeDtypeStruct((B, S, WIDTH_A), jnp.bfloat16),
        scratch_shapes=[
            pltpu.VMEM((S, LANES), jnp.float32),
            pltpu.VMEM((3, PAIR, S, LANES), jnp.bfloat16),
            pltpu.VMEM((2, S, LANES), jnp.bfloat16),
            pltpu.VMEM((2, S, LANES), jnp.bfloat16),
            pltpu.VMEM((3, S, LANES), jnp.float32),
            pltpu.VMEM((3, S, LANES), jnp.float32),
            pltpu.VMEM((3, S, LANES), jnp.float32),
        ],
        compiler_params=pltpu.CompilerParams(
            dimension_semantics=("arbitrary", "arbitrary"), vmem_limit_bytes=VMEM_LIMIT),
        name="attn_dilated",
    )(h, h, h, h, b1, b4, b16)


def _attn_b_kernel(q_ref, k_ref, v_ref, g_ref, bias_ref, y_ref, qm_scr):
    S = q_ref.shape[0]
    CH = 256
    lane_lt = _head0_lanes(CH)

    def prep(i, _):
        rows = pl.ds(pl.multiple_of(i * CH, CH), CH)
        blk = q_ref[rows, :].astype(jnp.float32) * SCALE
        qm_scr[0, rows, :] = jnp.where(lane_lt, blk, 0.0).astype(jnp.bfloat16)
        qm_scr[1, rows, :] = jnp.where(lane_lt, 0.0, blk).astype(jnp.bfloat16)
        return 0
    lax.fori_loop(0, S // CH, prep, 0)

    lane_q = _head0_lanes(GRID_W)
    nk = NA_ROWS * GRID_W
    last = GRID_ROWS - NA_ROWS

    def body(r, _):
        rows = pl.ds(pl.multiple_of(r * GRID_W, GRID_W), GRID_W)
        rs = jnp.clip(r - NA_ROWS // 2, 0, last)
        variant = jnp.where(r < NA_ROWS // 2, r,
                            jnp.where(r <= last + NA_ROWS // 2, NA_ROWS // 2, r - last))
        lhs = jnp.concatenate([qm_scr[0, rows, :], qm_scr[1, rows, :]], axis=0)
        k0 = pl.multiple_of(rs * GRID_W, GRID_W)
        o, _, db = _softmax_block(lhs, k_ref[pl.ds(k0, nk), :], v_ref[pl.ds(k0, nk), :],
                                  bias_ref[variant], lane_q)
        g = g_ref[rows, :].astype(jnp.float32)
        y_ref[rows, :] = (o / db * _silu(g)).astype(y_ref.dtype)
        return 0
    lax.fori_loop(0, GRID_ROWS, body, 0, unroll=BLOCK_UNROLL)


def _attention_b(h, bias_b):
    B, S, _ = h.shape
    nhp = N_HEADS_B // PAIR
    base = 4 * WIDTH_A // LANES
    blk = lambda off: pl.BlockSpec((None, S, LANES), lambda p, b: (b, 0, base + off + p))
    return pl.pallas_call(
        _attn_b_kernel,
        grid=(nhp, B),
        in_specs=[blk(0), blk(nhp), blk(2 * nhp), blk(3 * nhp),
                  pl.BlockSpec((None,) + bias_b.shape[1:], lambda p, b: (p, 0, 0, 0))],
        out_specs=pl.BlockSpec((None, S, LANES), lambda p, b: (b, 0, p)),
        out_shape=jax.ShapeDtypeStruct((B, S, WIDTH_B), jnp.bfloat16),
        scratch_shapes=[pltpu.VMEM((PAIR, S, LANES), jnp.bfloat16)],
        compiler_params=pltpu.CompilerParams(
            dimension_semantics=("arbitrary", "arbitrary"), vmem_limit_bytes=VMEM_LIMIT),
        name="attn_neighbourhood",
    )(h, h, h, h, bias_b)


def _out_kernel(ya_ref, yb_ref, wa_ref, wb_ref, x_ref, gain_ref, bias_ref, o_ref):
    out = jnp.dot(ya_ref[...], wa_ref[...], preferred_element_type=jnp.float32)
    out = out + jnp.dot(yb_ref[...], wb_ref[...], preferred_element_type=jnp.float32)
    z = DEEPNORM_ALPHA * x_ref[...] + out
    mu = jnp.mean(z, axis=-1, keepdims=True)
    zc = z - mu
    var = jnp.mean(zc * zc, axis=-1, keepdims=True)
    o_ref[...] = zc * lax.rsqrt(var + LN_EPS) * gain_ref[...] + bias_ref[...]


def _output_projection(ya, yb, wa, wb, x2, gain, bias, tm=256):
    M, D = x2.shape
    return pl.pallas_call(
        _out_kernel,
        grid=(M // tm,),
        in_specs=[pl.BlockSpec((tm, WIDTH_A), lambda i: (i, 0)),
                  pl.BlockSpec((tm, WIDTH_B), lambda i: (i, 0)),
                  pl.BlockSpec((WIDTH_A, D), lambda i: (0, 0)),
                  pl.BlockSpec((WIDTH_B, D), lambda i: (0, 0)),
                  pl.BlockSpec((tm, D), lambda i: (i, 0)),
                  pl.BlockSpec((1, D), lambda i: (0, 0)),
                  pl.BlockSpec((1, D), lambda i: (0, 0))],
        out_specs=pl.BlockSpec((tm, D), lambda i: (i, 0)),
        out_shape=jax.ShapeDtypeStruct((M, D), jnp.float32),
        compiler_params=pltpu.CompilerParams(
            dimension_semantics=("arbitrary",), vmem_limit_bytes=VMEM_LIMIT),
        name="out_proj_ln",
    )(ya, yb, wa, wb, x2, gain, bias)


def _t5_bucket_index(rel):
    half = NUM_BUCKETS // 2
    max_exact = half // 2
    n = np.abs(rel)
    large = max_exact + (np.log(np.maximum(n, 1) / max_exact)
                         / np.log(T5_MAX_DISTANCE / max_exact)
                         * (half - max_exact)).astype(np.int64)
    large = np.minimum(large, half - 1)
    return ((rel > 0).astype(np.int64) * half + np.where(n < max_exact, n, large)).astype(np.int32)


def _dilated_bucket_tables():
    tables = []
    q = np.arange(QBLK)[:, None]
    for d in DILATIONS:
        L = SEQ // d
        offsets, nk = ((0,), L) if L < KBLK else ((0, -SIDE, -2 * SIDE), KBLK)
        variants = []
        for off in offsets:
            rel = off + np.arange(nk)[None, :] - q
            variants.append(np.where(np.abs(rel) <= SIDE, _t5_bucket_index(rel * d), -1))
        tables.append(np.stack(variants).astype(np.int32))
    return tables


def _dilated_bias_kernel(t5_ref, i1_ref, i4_ref, i16_ref, b1_ref, b4_ref, b16_ref):
    p = pl.program_id(0)
    for idx_ref, out_ref in ((i1_ref, b1_ref), (i4_ref, b4_ref), (i16_ref, b16_ref)):
        for v in range(idx_ref.shape[0]):
            idx = idx_ref[v]
            for hh in range(PAIR):
                head = p * PAIR + hh

                def pick(bkt, acc):
                    return jnp.where(idx == bkt, t5_ref[bkt, head], acc)
                acc = lax.fori_loop(0, NUM_BUCKETS, pick,
                                    jnp.full(idx.shape, NEG_INF, jnp.float32))
                out_ref[v, hh * QBLK:(hh + 1) * QBLK, :] = acc


def _dilated_bias(t5_table):
    tables = [jnp.asarray(t) for t in _dilated_bucket_tables()]
    nhp = N_HEADS_A // PAIR
    whole = lambda t: pl.BlockSpec(t.shape, lambda p: (0, 0, 0))
    out_shapes = [jax.ShapeDtypeStruct((nhp, t.shape[0], PAIR * QBLK, t.shape[2]), jnp.float32)
                  for t in tables]
    return pl.pallas_call(
        _dilated_bias_kernel,
        grid=(nhp,),
        in_specs=[pl.BlockSpec(memory_space=pltpu.SMEM)] + [whole(t) for t in tables],
        out_specs=[pl.BlockSpec((None,) + o.shape[1:], lambda p: (p, 0, 0, 0)) for o in out_shapes],
        out_shape=out_shapes,
        compiler_params=pltpu.CompilerParams(dimension_semantics=("arbitrary",)),
        name="dilated_bias",
    )(t5_table, *tables)


RPB_ROWS = 2 * NA_ROWS - 1
RPB_COLS = 2 * NA_COLS - 1
NBR_VARIANT_ROWS = (tuple(range(NA_ROWS // 2)) + (NA_ROWS // 2,)
                    + tuple(range(GRID_ROWS - NA_ROWS // 2 + 1, GRID_ROWS)))


def _nbr_bias_kernel(rpb_ref, out_ref, m_scr):
    p = pl.program_id(0)
    shape = (GRID_W, LANES)
    qc = lax.broadcasted_iota(jnp.int32, shape, 0)
    lane = lax.broadcasted_iota(jnp.int32, shape, 1)
    kc = lane & (GRID_W - 1)
    qs = jnp.clip(qc - NA_COLS // 2, 0, GRID_W - NA_COLS)
    ci = jnp.where((kc >= qs) & (kc < qs + NA_COLS), kc - qc + NA_COLS - 1, -1)
    first = lane < GRID_W
    for hh in range(PAIR):
        head = p * PAIR + hh

        def fill_row(ri, _):
            base = (head * RPB_ROWS + ri) * RPB_COLS

            def pick(t, acc):
                return jnp.where(ci == t, rpb_ref[base + t], acc)
            m_scr[ri] = lax.fori_loop(0, RPB_COLS, pick, jnp.full(shape, NEG_INF, jnp.float32))
            return 0
        lax.fori_loop(0, RPB_ROWS, fill_row, 0)
        for v, r in enumerate(NBR_VARIANT_ROWS):
            rs = min(max(r - NA_ROWS // 2, 0), GRID_ROWS - NA_ROWS)
            for j2 in range(NA_ROWS * GRID_W // LANES):
                ri = rs + 2 * j2 - r + NA_ROWS - 1
                out_ref[v, hh * GRID_W:(hh + 1) * GRID_W, j2 * LANES:(j2 + 1) * LANES] = \
                    jnp.where(first, m_scr[ri], m_scr[ri + 1])


def _neighbourhood_bias(rpb):
    nhp = N_HEADS_B // PAIR
    nv = len(NBR_VARIANT_ROWS)
    out = jax.ShapeDtypeStruct((nhp, nv, PAIR * GRID_W, NA_ROWS * GRID_W), jnp.float32)
    return pl.pallas_call(
        _nbr_bias_kernel,
        grid=(nhp,),
        in_specs=[pl.BlockSpec(memory_space=pltpu.SMEM)],
        out_specs=pl.BlockSpec((None,) + out.shape[1:], lambda p: (p, 0, 0, 0)),
        out_shape=out,
        scratch_shapes=[pltpu.VMEM((RPB_ROWS, GRID_W, LANES), jnp.float32)],
        compiler_params=pltpu.CompilerParams(dimension_semantics=("arbitrary",)),
        name="nbr_bias",
    )(rpb.reshape(-1))


def kernel(x, w_in, w_out, t5_bias, na_rpb, ln_gain, ln_bias):
    B, S, D = x.shape
    depth = w_in.shape[0]
    b1, b4, b16 = _dilated_bias(t5_bias.astype(jnp.float32))
    for layer in range(depth):
        h = _input_projection(x.astype(jnp.bfloat16), w_in[layer].astype(jnp.bfloat16))
        ya = _attention_a(h, b1, b4, b16)
        yb = _attention_b(h, _neighbourhood_bias(na_rpb[layer].astype(jnp.float32)))
        wo = w_out[layer].astype(jnp.bfloat16)
        out = _output_projection(ya.reshape(B * S, WIDTH_A), yb.reshape(B * S, WIDTH_B),
                                 wo[:WIDTH_A], wo[WIDTH_A:], x.reshape(B * S, D),
                                 ln_gain[layer].reshape(1, D), ln_bias[layer].reshape(1, D))
        x = out.reshape(B, S, D)
    return x
```

```python
import functools

import numpy as np
import jax
import jax.numpy as jnp
from jax import lax
from jax.experimental import pallas as pl
from jax.experimental.pallas import tpu as pltpu

D_MODEL = 2048
SEQ = 2048
HEAD_DIM = 64
N_HEADS_A = 16
N_HEADS_B = 16
WIDTH_A = N_HEADS_A * HEAD_DIM
WIDTH_B = N_HEADS_B * HEAD_DIM
IN_WIDTH = 4 * WIDTH_A + 4 * WIDTH_B
DILATIONS = (1, 4, 16)
SIDE = 64
NUM_BUCKETS = 32
T5_MAX_DISTANCE = 1024
GRID_W = 64
GRID_ROWS = SEQ // GRID_W
NA_ROWS = 8
NA_COLS = 16
DEEPNORM_ALPHA = 2.0 ** 0.25
LN_EPS = 1e-5
NEG_INF = -1e30
SCALE = HEAD_DIM ** -0.5

LANES = 128
PAIR = LANES // HEAD_DIM
QBLK = 128
KBLK = QBLK + 2 * SIDE
BLOCK_UNROLL = 16
VMEM_LIMIT = 48 * 1024 * 1024


def _proj_kernel(x_ref, w_ref, o_ref):
    o_ref[...] = jnp.dot(x_ref[...], w_ref[...],
                         preferred_element_type=jnp.float32).astype(o_ref.dtype)


def _input_projection(xb, wb, tn=512):
    B, S, D = xb.shape
    N = wb.shape[1]
    return pl.pallas_call(
        _proj_kernel,
        grid=(B, N // tn),
        in_specs=[pl.BlockSpec((None, S, D), lambda b, n: (b, 0, 0)),
                  pl.BlockSpec((D, tn), lambda b, n: (0, n))],
        out_specs=pl.BlockSpec((None, S, tn), lambda b, n: (b, 0, n)),
        out_shape=jax.ShapeDtypeStruct((B, S, N), jnp.bfloat16),
        compiler_params=pltpu.CompilerParams(
            dimension_semantics=("arbitrary", "arbitrary"), vmem_limit_bytes=VMEM_LIMIT),
        name="in_proj",
    )(xb, wb)


def _head0_lanes(rows):
    return lax.broadcasted_iota(jnp.int32, (rows, LANES), 1) < HEAD_DIM


def _softmax_block(lhs, kb, vb, bias, lane_lt):
    nq = lhs.shape[0] // 2
    s = lax.dot_general(lhs, kb, (((1,), (1,)), ((), ())),
                        preferred_element_type=jnp.float32) + bias

    def half(sh):
        m = jnp.max(sh, axis=-1, keepdims=True)
        return jnp.exp(sh - m).astype(jnp.bfloat16), m

    p0, m0 = half(s[:nq])
    p1, m1 = half(s[nq:])
    v_ones = jnp.concatenate([vb, jnp.ones_like(vb)], axis=1)
    pv = jnp.dot(jnp.concatenate([p0, p1], axis=0), v_ones, preferred_element_type=jnp.float32)
    return (jnp.where(lane_lt, pv[:nq, :LANES], pv[nq:, :LANES]), jnp.where(lane_lt, m0, m1),
            jnp.where(lane_lt, pv[:nq, LANES:], pv[nq:, LANES:]))


def _silu(g):
    return g * (1.0 / (1.0 + jnp.exp(-g)))


def _attn_a_kernel(q_ref, k_ref, v_ref, g_ref, b1_ref, b4_ref, b16_ref, y_ref,
                   f32_scr, qm_scr, kp_scr, vp_scr, num_scr, m_scr, den_scr):
    S = q_ref.shape[0]
    CH = 256

    def permute(src_ref, c, d, store):
        L = S // d
        ch = min(CH, L)
        per_r = L // ch

        def body(i, _):
            r = i // per_r
            j0 = (i % per_r) * ch
            blk = src_ref[pl.ds(r + j0 * d, ch, stride=d), :] if d > 1 else \
                src_ref[pl.ds(pl.multiple_of(i * ch, ch), ch), :]
            store(c, pl.multiple_of(i * ch, ch), ch, blk)
            return 0
        lax.fori_loop(0, S // ch, body, 0)

    def store_q(c, row0, ch, blk):
        lt = _head0_lanes(ch)
        qm_scr[c, 0, pl.ds(row0, ch), :] = jnp.where(lt, blk, 0.0).astype(jnp.bfloat16)
        qm_scr[c, 1, pl.ds(row0, ch), :] = jnp.where(lt, 0.0, blk).astype(jnp.bfloat16)

    def store_to(dst):
        def store(c, row0, ch, blk):
            dst[c - 1, pl.ds(row0, ch), :] = blk.astype(jnp.bfloat16)
        return store

    f32_scr[...] = q_ref[...].astype(jnp.float32) * SCALE
    for c, d in enumerate(DILATIONS):
        permute(f32_scr, c, d, store_q)
    f32_scr[...] = k_ref[...].astype(jnp.float32)
    for c, d in enumerate(DILATIONS[1:], start=1):
        permute(f32_scr, c, d, store_to(kp_scr))
    f32_scr[...] = v_ref[...].astype(jnp.float32)
    for c, d in enumerate(DILATIONS[1:], start=1):
        permute(f32_scr, c, d, store_to(vp_scr))

    lane_q = _head0_lanes(QBLK)

    def run_branch(c, d, bias_ref):
        L = S // d
        nblk = L // QBLK
        nk = min(KBLK, L)

        def body(i, _):
            r = i // nblk
            bi = i % nblk
            q0 = pl.multiple_of(i * QBLK, QBLK)
            lhs = jnp.concatenate([qm_scr[c, 0, pl.ds(q0, QBLK), :],
                                   qm_scr[c, 1, pl.ds(q0, QBLK), :]], axis=0)
            if nk < KBLK:
                k0 = pl.multiple_of(r * L, QBLK)
                bias = bias_ref[0]
            else:
                k0 = pl.multiple_of(r * L + jnp.clip(bi * QBLK - SIDE, 0, L - KBLK), SIDE)
                variant = jnp.where(bi == 0, 0, jnp.where(bi == nblk - 1, 2, 1))
                bias = bias_ref[variant]
            if d == 1:
                kb = k_ref[pl.ds(k0, nk), :]
                vb = v_ref[pl.ds(k0, nk), :]
            else:
                kb = kp_scr[c - 1, pl.ds(k0, nk), :]
                vb = vp_scr[c - 1, pl.ds(k0, nk), :]
            o, mb, db = _softmax_block(lhs, kb, vb, bias, lane_q)
            if d == 1:
                rows = pl.ds(q0, QBLK)
            else:
                rows = pl.ds(r + bi * QBLK * d, QBLK, stride=d)
            num_scr[c, rows, :] = o
            m_scr[c, rows, :] = mb
            den_scr[c, rows, :] = db
            return 0
        lax.fori_loop(0, S // QBLK, body, 0, unroll=BLOCK_UNROLL)

    run_branch(0, DILATIONS[0], b1_ref)
    run_branch(1, DILATIONS[1], b4_ref)
    run_branch(2, DILATIONS[2], b16_ref)

    def merge(i, _):
        rows = pl.ds(pl.multiple_of(i * CH, CH), CH)
        m0, m1, m2 = m_scr[0, rows, :], m_scr[1, rows, :], m_scr[2, rows, :]
        mm = jnp.maximum(jnp.maximum(m0, m1), m2)
        w0, w1, w2 = jnp.exp(m0 - mm), jnp.exp(m1 - mm), jnp.exp(m2 - mm)
        num = w0 * num_scr[0, rows, :] + w1 * num_scr[1, rows, :] + w2 * num_scr[2, rows, :]
        den = w0 * den_scr[0, rows, :] + w1 * den_scr[1, rows, :] + w2 * den_scr[2, rows, :]
        g = g_ref[rows, :].astype(jnp.float32)
        y_ref[rows, :] = (num / den * _silu(g)).astype(y_ref.dtype)
        return 0
    lax.fori_loop(0, S // CH, merge, 0)


def _attention_a(h, b1, b4, b16):
    B, S, _ = h.shape
    nhp = N_HEADS_A // PAIR
    blk = lambda off: pl.BlockSpec((None, S, LANES), lambda p, b: (b, 0, off + p))
    tab = lambda t: pl.BlockSpec((None,) + t.shape[1:], lambda p, b: (p, 0, 0, 0))
    return pl.pallas_call(
        _attn_a_kernel,
        grid=(nhp, B),
        in_specs=[blk(0), blk(nhp), blk(2 * nhp), blk(3 * nhp), tab(b1), tab(b4), tab(b16)],
        out_specs=pl.BlockSpec((None, S, LANES), lambda p, b: (b, 0, p)),
        out_shape=jax.ShapeDtypeStruct((B, S, WIDTH_A), jnp.bfloat16),
        scratch_shapes=[
            pltpu.VMEM((S, LANES), jnp.float32),
            pltpu.VMEM((3, PAIR, S, LANES), jnp.bfloat16),
            pltpu.VMEM((2, S, LANES), jnp.bfloat16),
            pltpu.VMEM((2, S, LANES), jnp.bfloat16),
            pltpu.VMEM((3, S, LANES), jnp.float32),
            pltpu.VMEM((3, S, LANES), jnp.float32),
            pltpu.VMEM((3, S, LANES), jnp.float32),
        ],
        compiler_params=pltpu.CompilerParams(
            dimension_semantics=("arbitrary", "arbitrary"), vmem_limit_bytes=VMEM_LIMIT),
        name="attn_dilated",
    )(h, h, h, h, b1, b4, b16)


def _attn_b_kernel(q_ref, k_ref, v_ref, g_ref, bias_ref, y_ref, qm_scr):
    S = q_ref.shape[0]
    CH = 256
    lane_lt = _head0_lanes(CH)

    def prep(i, _):
        rows = pl.ds(pl.multiple_of(i * CH, CH), CH)
        blk = q_ref[rows, :].astype(jnp.float32) * SCALE
        qm_scr[0, rows, :] = jnp.where(lane_lt, blk, 0.0).astype(jnp.bfloat16)
        qm_scr[1, rows, :] = jnp.where(lane_lt, 0.0, blk).astype(jnp.bfloat16)
        return 0
    lax.fori_loop(0, S // CH, prep, 0)

    lane_q = _head0_lanes(GRID_W)
    nk = NA_ROWS * GRID_W
    last = GRID_ROWS - NA_ROWS

    def body(r, _):
        rows = pl.ds(pl.multiple_of(r * GRID_W, GRID_W), GRID_W)
        rs = jnp.clip(r - NA_ROWS // 2, 0, last)
        variant = jnp.where(r < NA_ROWS // 2, r,
                            jnp.where(r <= last + NA_ROWS // 2, NA_ROWS // 2, r - last))
        lhs = jnp.concatenate([qm_scr[0, rows, :], qm_scr[1, rows, :]], axis=0)
        k0 = pl.multiple_of(rs * GRID_W, GRID_W)
        o, _, db = _softmax_block(lhs, k_ref[pl.ds(k0, nk), :], v_ref[pl.ds(k0, nk), :],
                                  bias_ref[variant], lane_q)
        g = g_ref[rows, :].astype(jnp.float32)
        y_ref[rows, :] = (o / db * _silu(g)).astype(y_ref.dtype)
        return 0
    lax.fori_loop(0, GRID_ROWS, body, 0, unroll=2 * BLOCK_UNROLL)


def _attention_b(h, bias_b):
    B, S, _ = h.shape
    nhp = N_HEADS_B // PAIR
    base = 4 * WIDTH_A // LANES
    blk = lambda off: pl.BlockSpec((None, S, LANES), lambda p, b: (b, 0, base + off + p))
    return pl.pallas_call(
        _attn_b_kernel,
        grid=(nhp, B),
        in_specs=[blk(0), blk(nhp), blk(2 * nhp), blk(3 * nhp),
                  pl.BlockSpec((None,) + bias_b.shape[1:], lambda p, b: (p, 0, 0, 0))],
        out_specs=pl.BlockSpec((None, S, LANES), lambda p, b: (b, 0, p)),
        out_shape=jax.ShapeDtypeStruct((B, S, WIDTH_B), jnp.bfloat16),
        scratch_shapes=[pltpu.VMEM((PAIR, S, LANES), jnp.bfloat16)],
        compiler_params=pltpu.CompilerParams(
            dimension_semantics=("arbitrary", "arbitrary"), vmem_limit_bytes=VMEM_LIMIT),
        name="attn_neighbourhood",
    )(h, h, h, h, bias_b)


def _out_kernel(ya_ref, yb_ref, wa_ref, wb_ref, x_ref, gain_ref, bias_ref, o_ref):
    out = jnp.dot(ya_ref[...], wa_ref[...], preferred_element_type=jnp.float32)
    out = out + jnp.dot(yb_ref[...], wb_ref[...], preferred_element_type=jnp.float32)
    z = DEEPNORM_ALPHA * x_ref[...] + out
    mu = jnp.mean(z, axis=-1, keepdims=True)
    zc = z - mu
    var = jnp.mean(zc * zc, axis=-1, keepdims=True)
    o_ref[...] = zc * lax.rsqrt(var + LN_EPS) * gain_ref[...] + bias_ref[...]


def _output_projection(ya, yb, wa, wb, x2, gain, bias, tm=256):
    M, D = x2.shape
    return pl.pallas_call(
        _out_kernel,
        grid=(M // tm,),
        in_specs=[pl.BlockSpec((tm, WIDTH_A), lambda i: (i, 0)),
                  pl.BlockSpec((tm, WIDTH_B), lambda i: (i, 0)),
                  pl.BlockSpec((WIDTH_A, D), lambda i: (0, 0)),
                  pl.BlockSpec((WIDTH_B, D), lambda i: (0, 0)),
                  pl.BlockSpec((tm, D), lambda i: (i, 0)),
                  pl.BlockSpec((1, D), lambda i: (0, 0)),
                  pl.BlockSpec((1, D), lambda i: (0, 0))],
        out_specs=pl.BlockSpec((tm, D), lambda i: (i, 0)),
        out_shape=jax.ShapeDtypeStruct((M, D), jnp.float32),
        compiler_params=pltpu.CompilerParams(
            dimension_semantics=("arbitrary",), vmem_limit_bytes=VMEM_LIMIT),
        name="out_proj_ln",
    )(ya, yb, wa, wb, x2, gain, bias)


def _t5_bucket_index(rel):
    half = NUM_BUCKETS // 2
    max_exact = half // 2
    n = np.abs(rel)
    large = max_exact + (np.log(np.maximum(n, 1) / max_exact)
                         / np.log(T5_MAX_DISTANCE / max_exact)
                         * (half - max_exact)).astype(np.int64)
    large = np.minimum(large, half - 1)
    return ((rel > 0).astype(np.int64) * half + np.where(n < max_exact, n, large)).astype(np.int32)


def _dilated_bucket_tables():
    tables = []
    q = np.arange(QBLK)[:, None]
    for d in DILATIONS:
        L = SEQ // d
        offsets, nk = ((0,), L) if L < KBLK else ((0, -SIDE, -2 * SIDE), KBLK)
        variants = []
        for off in offsets:
            rel = off + np.arange(nk)[None, :] - q
            variants.append(np.where(np.abs(rel) <= SIDE, _t5_bucket_index(rel * d), -1))
        tables.append(np.stack(variants).astype(np.int32))
    return tables


def _dilated_bias_kernel(t5_ref, i1_ref, i4_ref, i16_ref, b1_ref, b4_ref, b16_ref):
    p = pl.program_id(0)
    for idx_ref, out_ref in ((i1_ref, b1_ref), (i4_ref, b4_ref), (i16_ref, b16_ref)):
        for v in range(idx_ref.shape[0]):
            idx = idx_ref[v]
            for hh in range(PAIR):
                head = p * PAIR + hh

                def pick(bkt, acc):
                    return jnp.where(idx == bkt, t5_ref[bkt, head], acc)
                acc = lax.fori_loop(0, NUM_BUCKETS, pick,
                                    jnp.full(idx.shape, NEG_INF, jnp.float32))
                out_ref[v, hh * QBLK:(hh + 1) * QBLK, :] = acc


def _dilated_bias(t5_table):
    tables = [jnp.asarray(t) for t in _dilated_bucket_tables()]
    nhp = N_HEADS_A // PAIR
    whole = lambda t: pl.BlockSpec(t.shape, lambda p: (0, 0, 0))
    out_shapes = [jax.ShapeDtypeStruct((nhp, t.shape[0], PAIR * QBLK, t.shape[2]), jnp.float32)
                  for t in tables]
    return pl.pallas_call(
        _dilated_bias_kernel,
        grid=(nhp,),
        in_specs=[pl.BlockSpec(memory_space=pltpu.SMEM)] + [whole(t) for t in tables],
        out_specs=[pl.BlockSpec((None,) + o.shape[1:], lambda p: (p, 0, 0, 0)) for o in out_shapes],
        out_shape=out_shapes,
        compiler_params=pltpu.CompilerParams(dimension_semantics=("arbitrary",)),
        name="dilated_bias",
    )(t5_table, *tables)


RPB_ROWS = 2 * NA_ROWS - 1
RPB_COLS = 2 * NA_COLS - 1
NBR_VARIANT_ROWS = (tuple(range(NA_ROWS // 2)) + (NA_ROWS // 2,)
                    + tuple(range(GRID_ROWS - NA_ROWS // 2 + 1, GRID_ROWS)))


def _nbr_bias_kernel(rpb_ref, out_ref, m_scr):
    p = pl.program_id(0)
    shape = (GRID_W, LANES)
    qc = lax.broadcasted_iota(jnp.int32, shape, 0)
    lane = lax.broadcasted_iota(jnp.int32, shape, 1)
    kc = lane & (GRID_W - 1)
    qs = jnp.clip(qc - NA_COLS // 2, 0, GRID_W - NA_COLS)
    ci = jnp.where((kc >= qs) & (kc < qs + NA_COLS), kc - qc + NA_COLS - 1, -1)
    first = lane < GRID_W
    for hh in range(PAIR):
        head = p * PAIR + hh

        def fill_row(ri, _):
            base = (head * RPB_ROWS + ri) * RPB_COLS

            def pick(t, acc):
                return jnp.where(ci == t, rpb_ref[base + t], acc)
            m_scr[ri] = lax.fori_loop(0, RPB_COLS, pick, jnp.full(shape, NEG_INF, jnp.float32))
            return 0
        lax.fori_loop(0, RPB_ROWS, fill_row, 0)
        for v, r in enumerate(NBR_VARIANT_ROWS):
            rs = min(max(r - NA_ROWS // 2, 0), GRID_ROWS - NA_ROWS)
            for j2 in range(NA_ROWS * GRID_W // LANES):
                ri = rs + 2 * j2 - r + NA_ROWS - 1
                out_ref[v, hh * GRID_W:(hh + 1) * GRID_W, j2 * LANES:(j2 + 1) * LANES] = \
                    jnp.where(first, m_scr[ri], m_scr[ri + 1])


def _neighbourhood_bias(rpb):
    nhp = N_HEADS_B // PAIR
    nv = len(NBR_VARIANT_ROWS)
    out = jax.ShapeDtypeStruct((nhp, nv, PAIR * GRID_W, NA_ROWS * GRID_W), jnp.float32)
    return pl.pallas_call(
        _nbr_bias_kernel,
        grid=(nhp,),
        in_specs=[pl.BlockSpec(memory_space=pltpu.SMEM)],
        out_specs=pl.BlockSpec((None,) + out.shape[1:], lambda p: (p, 0, 0, 0)),
        out_shape=out,
        scratch_shapes=[pltpu.VMEM((RPB_ROWS, GRID_W, LANES), jnp.float32)],
        compiler_params=pltpu.CompilerParams(dimension_semantics=("arbitrary",)),
        name="nbr_bias",
    )(rpb.reshape(-1))


def kernel(x, w_in, w_out, t5_bias, na_rpb, ln_gain, ln_bias):
    B, S, D = x.shape
    depth = w_in.shape[0]
    b1, b4, b16 = _dilated_bias(t5_bias.astype(jnp.float32))
    for layer in range(depth):
        h = _input_projection(x.astype(jnp.bfloat16), w_in[layer].astype(jnp.bfloat16))
        ya = _attention_a(h, b1, b4, b16)
        yb = _attention_b(h, _neighbourhood_bias(na_rpb[layer].astype(jnp.float32)))
        wo = w_out[layer].astype(jnp.bfloat16)
        out = _output_projection(ya.reshape(B * S, WIDTH_A), yb.reshape(B * S, WIDTH_B),
                                 wo[:WIDTH_A], wo[WIDTH_A:], x.reshape(B * S, D),
                                 ln_gain[layer].reshape(1, D), ln_bias[layer].reshape(1, D))
        x = out.reshape(B, S, D)
    return x
```

```python
import functools

import numpy as np
import jax
import jax.numpy as jnp
from jax import lax
from jax.experimental import pallas as pl
from jax.experimental.pallas import tpu as pltpu

D_MODEL = 2048
SEQ = 2048
HEAD_DIM = 64
N_HEADS_A = 16
N_HEADS_B = 16
WIDTH_A = N_HEADS_A * HEAD_DIM
WIDTH_B = N_HEADS_B * HEAD_DIM
IN_WIDTH = 4 * WIDTH_A + 4 * WIDTH_B
DILATIONS = (1, 4, 16)
SIDE = 64
NUM_BUCKETS = 32
T5_MAX_DISTANCE = 1024
GRID_W = 64
GRID_ROWS = SEQ // GRID_W
NA_ROWS = 8
NA_COLS = 16
DEEPNORM_ALPHA = 2.0 ** 0.25
LN_EPS = 1e-5
NEG_INF = -1e30
SCALE = HEAD_DIM ** -0.5

LANES = 128
PAIR = LANES // HEAD_DIM
QBLK = 128
KBLK = QBLK + 2 * SIDE
BLOCK_UNROLL = 16
VMEM_LIMIT = 48 * 1024 * 1024


def _proj_kernel(x_ref, w_ref, o_ref, xb_scr):
    @pl.when(pl.program_id(1) == 0)
    def _():
        xb_scr[...] = x_ref[...].astype(jnp.bfloat16)
    o_ref[...] = jnp.dot(xb_scr[...], w_ref[...],
                         preferred_element_type=jnp.float32).astype(o_ref.dtype)


def _input_projection(x2, wb, tm=1024, tn=512):
    M, D = x2.shape
    N = wb.shape[1]
    return pl.pallas_call(
        _proj_kernel,
        grid=(M // tm, N // tn),
        in_specs=[pl.BlockSpec((tm, D), lambda m, n: (m, 0)),
                  pl.BlockSpec((D, tn), lambda m, n: (0, n))],
        out_specs=pl.BlockSpec((tm, tn), lambda m, n: (m, n)),
        out_shape=jax.ShapeDtypeStruct((M, N), jnp.bfloat16),
        scratch_shapes=[pltpu.VMEM((tm, D), jnp.bfloat16)],
        compiler_params=pltpu.CompilerParams(
            dimension_semantics=("arbitrary", "arbitrary"), vmem_limit_bytes=VMEM_LIMIT),
        name="in_proj",
    )(x2, wb)


def _head0_lanes(rows):
    return lax.broadcasted_iota(jnp.int32, (rows, LANES), 1) < HEAD_DIM


def _softmax_block(lhs, kb, vb, bias, lane_lt):
    nq = lhs.shape[0] // 2
    s = lax.dot_general(lhs, kb, (((1,), (1,)), ((), ())),
                        preferred_element_type=jnp.float32) + bias

    def half(sh):
        m = jnp.max(sh, axis=-1, keepdims=True)
        return jnp.exp(sh - m).astype(jnp.bfloat16), m

    p0, m0 = half(s[:nq])
    p1, m1 = half(s[nq:])
    v_ones = jnp.concatenate([vb, jnp.ones_like(vb)], axis=1)
    pv = jnp.dot(jnp.concatenate([p0, p1], axis=0), v_ones, preferred_element_type=jnp.float32)
    return (jnp.where(lane_lt, pv[:nq, :LANES], pv[nq:, :LANES]), jnp.where(lane_lt, m0, m1),
            jnp.where(lane_lt, pv[:nq, LANES:], pv[nq:, LANES:]))


def _silu(g):
    return g * (1.0 / (1.0 + jnp.exp(-g)))


def _attn_a_kernel(q_ref, k_ref, v_ref, g_ref, b1_ref, b4_ref, b16_ref, y_ref,
                   f32_scr, qm_scr, kp_scr, vp_scr, num_scr, m_scr, den_scr):
    S = q_ref.shape[0]
    CH = 256

    def permute(src_ref, c, d, store):
        L = S // d
        ch = min(CH, L)
        per_r = L // ch

        def body(i, _):
            r = i // per_r
            j0 = (i % per_r) * ch
            blk = src_ref[pl.ds(r + j0 * d, ch, stride=d), :] if d > 1 else \
                src_ref[pl.ds(pl.multiple_of(i * ch, ch), ch), :]
            store(c, pl.multiple_of(i * ch, ch), ch, blk)
            return 0
        lax.fori_loop(0, S // ch, body, 0)

    def store_q(c, row0, ch, blk):
        lt = _head0_lanes(ch)
        qm_scr[c, 0, pl.ds(row0, ch), :] = jnp.where(lt, blk, 0.0).astype(jnp.bfloat16)
        qm_scr[c, 1, pl.ds(row0, ch), :] = jnp.where(lt, 0.0, blk).astype(jnp.bfloat16)

    def store_to(dst):
        def store(c, row0, ch, blk):
            dst[c - 1, pl.ds(row0, ch), :] = blk.astype(jnp.bfloat16)
        return store

    f32_scr[...] = q_ref[...].astype(jnp.float32) * SCALE
    for c, d in enumerate(DILATIONS):
        permute(f32_scr, c, d, store_q)
    f32_scr[...] = k_ref[...].astype(jnp.float32)
    for c, d in enumerate(DILATIONS[1:], start=1):
        permute(f32_scr, c, d, store_to(kp_scr))
    f32_scr[...] = v_ref[...].astype(jnp.float32)
    for c, d in enumerate(DILATIONS[1:], start=1):
        permute(f32_scr, c, d, store_to(vp_scr))

    lane_q = _head0_lanes(QBLK)

    def run_branch(c, d, bias_ref):
        L = S // d
        nblk = L // QBLK
        nk = min(KBLK, L)

        def body(i, _):
            r = i // nblk
            bi = i % nblk
            q0 = pl.multiple_of(i * QBLK, QBLK)
            lhs = jnp.concatenate([qm_scr[c, 0, pl.ds(q0, QBLK), :],
                                   qm_scr[c, 1, pl.ds(q0, QBLK), :]], axis=0)
            if nk < KBLK:
                k0 = pl.multiple_of(r * L, QBLK)
                bias = bias_ref[0]
            else:
                k0 = pl.multiple_of(r * L + jnp.clip(bi * QBLK - SIDE, 0, L - KBLK), SIDE)
                variant = jnp.where(bi == 0, 0, jnp.where(bi == nblk - 1, 2, 1))
                bias = bias_ref[variant]
            if d == 1:
                kb = k_ref[pl.ds(k0, nk), :]
                vb = v_ref[pl.ds(k0, nk), :]
            else:
                kb = kp_scr[c - 1, pl.ds(k0, nk), :]
                vb = vp_scr[c - 1, pl.ds(k0, nk), :]
            o, mb, db = _softmax_block(lhs, kb, vb, bias, lane_q)
            if d == 1:
                rows = pl.ds(q0, QBLK)
            else:
                rows = pl.ds(r + bi * QBLK * d, QBLK, stride=d)
            num_scr[c, rows, :] = o
            m_scr[c, rows, :] = mb
            den_scr[c, rows, :] = db
            return 0
        lax.fori_loop(0, S // QBLK, body, 0, unroll=BLOCK_UNROLL)

    run_branch(0, DILATIONS[0], b1_ref)
    run_branch(1, DILATIONS[1], b4_ref)
    run_branch(2, DILATIONS[2], b16_ref)

    def merge(i, _):
        rows = pl.ds(pl.multiple_of(i * CH, CH), CH)
        m0, m1, m2 = m_scr[0, rows, :], m_scr[1, rows, :], m_scr[2, rows, :]
        mm = jnp.maximum(jnp.maximum(m0, m1), m2)
        w0, w1, w2 = jnp.exp(m0 - mm), jnp.exp(m1 - mm), jnp.exp(m2 - mm)
        num = w0 * num_scr[0, rows, :] + w1 * num_scr[1, rows, :] + w2 * num_scr[2, rows, :]
        den = w0 * den_scr[0, rows, :] + w1 * den_scr[1, rows, :] + w2 * den_scr[2, rows, :]
        g = g_ref[rows, :].astype(jnp.float32)
        y_ref[rows, :] = (num / den * _silu(g)).astype(y_ref.dtype)
        return 0
    lax.fori_loop(0, S // CH, merge, 0)


def _attention_a(h, b1, b4, b16):
    B, S, _ = h.shape
    nhp = N_HEADS_A // PAIR
    blk = lambda off: pl.BlockSpec((None, S, LANES), lambda p, b: (b, 0, off + p))
    tab = lambda t: pl.BlockSpec((None,) + t.shape[1:], lambda p, b: (p, 0, 0, 0))
    return pl.pallas_call(
        _attn_a_kernel,
        grid=(nhp, B),
        in_specs=[blk(0), blk(nhp), blk(2 * nhp), blk(3 * nhp), tab(b1), tab(b4), tab(b16)],
        out_specs=pl.BlockSpec((None, S, LANES), lambda p, b: (b, 0, p)),
        out_shape=jax.ShapeDtypeStruct((B, S, WIDTH_A), jnp.bfloat16),
        scratch_shapes=[
            pltpu.VMEM((S, LANES), jnp.float32),
            pltpu.VMEM((3, PAIR, S, LANES), jnp.bfloat16),
            pltpu.VMEM((2, S, LANES), jnp.bfloat16),
            pltpu.VMEM((2, S, LANES), jnp.bfloat16),
            pltpu.VMEM((3, S, LANES), jnp.float32),
            pltpu.VMEM((3, S, LANES), jnp.float32),
            pltpu.VMEM((3, S, LANES), jnp.float32),
        ],
        compiler_params=pltpu.CompilerParams(
            dimension_semantics=("arbitrary", "arbitrary"), vmem_limit_bytes=VMEM_LIMIT),
        name="attn_dilated",
    )(h, h, h, h, b1, b4, b16)


def _attn_b_kernel(q_ref, k_ref, v_ref, g_ref, bias_ref, y_ref, qm_scr):
    S = q_ref.shape[0]
    CH = 256
    lane_lt = _head0_lanes(CH)

    def prep(i, _):
        rows = pl.ds(pl.multiple_of(i * CH, CH), CH)
        blk = q_ref[rows, :].astype(jnp.float32) * SCALE
        qm_scr[0, rows, :] = jnp.where(lane_lt, blk, 0.0).astype(jnp.bfloat16)
        qm_scr[1, rows, :] = jnp.where(lane_lt, 0.0, blk).astype(jnp.bfloat16)
        return 0
    lax.fori_loop(0, S // CH, prep, 0)

    lane_q = _head0_lanes(GRID_W)
    nk = NA_ROWS * GRID_W
    last = GRID_ROWS - NA_ROWS

    def body(r, _):
        rows = pl.ds(pl.multiple_of(r * GRID_W, GRID_W), GRID_W)
        rs = jnp.clip(r - NA_ROWS // 2, 0, last)
        variant = jnp.where(r < NA_ROWS // 2, r,
                            jnp.where(r <= last + NA_ROWS // 2, NA_ROWS // 2, r - last))
        lhs = jnp.concatenate([qm_scr[0, rows, :], qm_scr[1, rows, :]], axis=0)
        k0 = pl.multiple_of(rs * GRID_W, GRID_W)
        o, _, db = _softmax_block(lhs, k_ref[pl.ds(k0, nk), :], v_ref[pl.ds(k0, nk), :],
                                  bias_ref[variant], lane_q)
        g = g_ref[rows, :].astype(jnp.float32)
        y_ref[rows, :] = (o / db * _silu(g)).astype(y_ref.dtype)
        return 0
    lax.fori_loop(0, GRID_ROWS, body, 0, unroll=2 * BLOCK_UNROLL)


def _attention_b(h, bias_b):
    B, S, _ = h.shape
    nhp = N_HEADS_B // PAIR
    base = 4 * WIDTH_A // LANES
    blk = lambda off: pl.BlockSpec((None, S, LANES), lambda p, b: (b, 0, base + off + p))
    return pl.pallas_call(
        _attn_b_kernel,
        grid=(nhp, B),
        in_specs=[blk(0), blk(nhp), blk(2 * nhp), blk(3 * nhp),
                  pl.BlockSpec((None,) + bias_b.shape[1:], lambda p, b: (p, 0, 0, 0))],
        out_specs=pl.BlockSpec((None, S, LANES), lambda p, b: (b, 0, p)),
        out_shape=jax.ShapeDtypeStruct((B, S, WIDTH_B), jnp.bfloat16),
        scratch_shapes=[pltpu.VMEM((PAIR, S, LANES), jnp.bfloat16)],
        compiler_params=pltpu.CompilerParams(
            dimension_semantics=("arbitrary", "arbitrary"), vmem_limit_bytes=VMEM_LIMIT),
        name="attn_neighbourhood",
    )(h, h, h, h, bias_b)


def _out_kernel(ya_ref, yb_ref, wa_ref, wb_ref, x_ref, gain_ref, bias_ref, o_ref):
    out = jnp.dot(ya_ref[...], wa_ref[...], preferred_element_type=jnp.float32)
    out = out + jnp.dot(yb_ref[...], wb_ref[...], preferred_element_type=jnp.float32)
    z = DEEPNORM_ALPHA * x_ref[...] + out
    mu = jnp.mean(z, axis=-1, keepdims=True)
    zc = z - mu
    var = jnp.mean(zc * zc, axis=-1, keepdims=True)
    o_ref[...] = zc * lax.rsqrt(var + LN_EPS) * gain_ref[...] + bias_ref[...]


def _output_projection(ya, yb, wa, wb, x2, gain, bias, tm=512):
    M, D = x2.shape
    return pl.pallas_call(
        _out_kernel,
        grid=(M // tm,),
        in_specs=[pl.BlockSpec((tm, WIDTH_A), lambda i: (i, 0)),
                  pl.BlockSpec((tm, WIDTH_B), lambda i: (i, 0)),
                  pl.BlockSpec((WIDTH_A, D), lambda i: (0, 0)),
                  pl.BlockSpec((WIDTH_B, D), lambda i: (0, 0)),
                  pl.BlockSpec((tm, D), lambda i: (i, 0)),
                  pl.BlockSpec((1, D), lambda i: (0, 0)),
                  pl.BlockSpec((1, D), lambda i: (0, 0))],
        out_specs=pl.BlockSpec((tm, D), lambda i: (i, 0)),
        out_shape=jax.ShapeDtypeStruct((M, D), jnp.float32),
        compiler_params=pltpu.CompilerParams(
            dimension_semantics=("arbitrary",), vmem_limit_bytes=VMEM_LIMIT),
        name="out_proj_ln",
    )(ya, yb, wa, wb, x2, gain, bias)


def _t5_bucket_index(rel):
    half = NUM_BUCKETS // 2
    max_exact = half // 2
    n = np.abs(rel)
    large = max_exact + (np.log(np.maximum(n, 1) / max_exact)
                         / np.log(T5_MAX_DISTANCE / max_exact)
                         * (half - max_exact)).astype(np.int64)
    large = np.minimum(large, half - 1)
    return ((rel > 0).astype(np.int64) * half + np.where(n < max_exact, n, large)).astype(np.int32)


def _dilated_bucket_tables():
    tables = []
    q = np.arange(QBLK)[:, None]
    for d in DILATIONS:
        L = SEQ // d
        offsets, nk = ((0,), L) if L < KBLK else ((0, -SIDE, -2 * SIDE), KBLK)
        variants = []
        for off in offsets:
            rel = off + np.arange(nk)[None, :] - q
            variants.append(np.where(np.abs(rel) <= SIDE, _t5_bucket_index(rel * d), -1))
        tables.append(np.stack(variants).astype(np.int32))
    return tables


def _dilated_bias_kernel(t5_ref, i1_ref, i4_ref, i16_ref, b1_ref, b4_ref, b16_ref):
    p = pl.program_id(0)
    for idx_ref, out_ref in ((i1_ref, b1_ref), (i4_ref, b4_ref), (i16_ref, b16_ref)):
        for v in range(idx_ref.shape[0]):
            idx = idx_ref[v]
            for hh in range(PAIR):
                head = p * PAIR + hh

                def pick(bkt, acc):
                    return jnp.where(idx == bkt, t5_ref[bkt, head], acc)
                acc = lax.fori_loop(0, NUM_BUCKETS, pick,
                                    jnp.full(idx.shape, NEG_INF, jnp.float32))
                out_ref[v, hh * QBLK:(hh + 1) * QBLK, :] = acc


def _dilated_bias(t5_table):
    tables = [jnp.asarray(t) for t in _dilated_bucket_tables()]
    nhp = N_HEADS_A // PAIR
    whole = lambda t: pl.BlockSpec(t.shape, lambda p: (0, 0, 0))
    out_shapes = [jax.ShapeDtypeStruct((nhp, t.shape[0], PAIR * QBLK, t.shape[2]), jnp.float32)
                  for t in tables]
    return pl.pallas_call(
        _dilated_bias_kernel,
        grid=(nhp,),
        in_specs=[pl.BlockSpec(memory_space=pltpu.SMEM)] + [whole(t) for t in tables],
        out_specs=[pl.BlockSpec((None,) + o.shape[1:], lambda p: (p, 0, 0, 0)) for o in out_shapes],
        out_shape=out_shapes,
        compiler_params=pltpu.CompilerParams(dimension_semantics=("arbitrary",)),
        name="dilated_bias",
    )(t5_table, *tables)


RPB_ROWS = 2 * NA_ROWS - 1
RPB_COLS = 2 * NA_COLS - 1
NBR_VARIANT_ROWS = (tuple(range(NA_ROWS // 2)) + (NA_ROWS // 2,)
                    + tuple(range(GRID_ROWS - NA_ROWS // 2 + 1, GRID_ROWS)))


def _nbr_bias_kernel(rpb_ref, out_ref, m_scr):
    p = pl.program_id(0)
    shape = (GRID_W, LANES)
    qc = lax.broadcasted_iota(jnp.int32, shape, 0)
    lane = lax.broadcasted_iota(jnp.int32, shape, 1)
    kc = lane & (GRID_W - 1)
    qs = jnp.clip(qc - NA_COLS // 2, 0, GRID_W - NA_COLS)
    ci = jnp.where((kc >= qs) & (kc < qs + NA_COLS), kc - qc + NA_COLS - 1, -1)
    first = lane < GRID_W
    for hh in range(PAIR):
        head = p * PAIR + hh

        def fill_row(ri, _):
            base = (head * RPB_ROWS + ri) * RPB_COLS

            def pick(t, acc):
                return jnp.where(ci == t, rpb_ref[base + t], acc)
            m_scr[ri] = lax.fori_loop(0, RPB_COLS, pick, jnp.full(shape, NEG_INF, jnp.float32))
            return 0
        lax.fori_loop(0, RPB_ROWS, fill_row, 0)
        for v, r in enumerate(NBR_VARIANT_ROWS):
            rs = min(max(r - NA_ROWS // 2, 0), GRID_ROWS - NA_ROWS)
            for j2 in range(NA_ROWS * GRID_W // LANES):
                ri = rs + 2 * j2 - r + NA_ROWS - 1
                out_ref[v, hh * GRID_W:(hh + 1) * GRID_W, j2 * LANES:(j2 + 1) * LANES] = \
                    jnp.where(first, m_scr[ri], m_scr[ri + 1])


def _neighbourhood_bias(rpb):
    nhp = N_HEADS_B // PAIR
    nv = len(NBR_VARIANT_ROWS)
    out = jax.ShapeDtypeStruct((nhp, nv, PAIR * GRID_W, NA_ROWS * GRID_W), jnp.float32)
    return pl.pallas_call(
        _nbr_bias_kernel,
        grid=(nhp,),
        in_specs=[pl.BlockSpec(memory_space=pltpu.SMEM)],
        out_specs=pl.BlockSpec((None,) + out.shape[1:], lambda p: (p, 0, 0, 0)),
        out_shape=out,
        scratch_shapes=[pltpu.VMEM((RPB_ROWS, GRID_W, LANES), jnp.float32)],
        compiler_params=pltpu.CompilerParams(dimension_semantics=("arbitrary",)),
        name="nbr_bias",
    )(rpb.reshape(-1))


def kernel(x, w_in, w_out, t5_bias, na_rpb, ln_gain, ln_bias):
    B, S, D = x.shape
    depth = w_in.shape[0]
    b1, b4, b16 = _dilated_bias(t5_bias.astype(jnp.float32))
    for layer in range(depth):
        h = _input_projection(x.reshape(B * S, D), w_in[layer].astype(jnp.bfloat16))
        h = h.reshape(B, S, IN_WIDTH)
        ya = _attention_a(h, b1, b4, b16)
        yb = _attention_b(h, _neighbourhood_bias(na_rpb[layer].astype(jnp.float32)))
        wo = w_out[layer].astype(jnp.bfloat16)
        out = _output_projection(ya.reshape(B * S, WIDTH_A), yb.reshape(B * S, WIDTH_B),
                                 wo[:WIDTH_A], wo[WIDTH_A:], x.reshape(B * S, D),
                                 ln_gain[layer].reshape(1, D), ln_bias[layer].reshape(1, D))
        x = out.reshape(B, S, D)
    return x
```

```python
import functools

import numpy as np
import jax
import jax.numpy as jnp
from jax import lax
from jax.experimental import pallas as pl
from jax.experimental.pallas import tpu as pltpu

D_MODEL = 2048
SEQ = 2048
HEAD_DIM = 64
N_HEADS_A = 16
N_HEADS_B = 16
WIDTH_A = N_HEADS_A * HEAD_DIM
WIDTH_B = N_HEADS_B * HEAD_DIM
IN_WIDTH = 4 * WIDTH_A + 4 * WIDTH_B
DILATIONS = (1, 4, 16)
SIDE = 64
NUM_BUCKETS = 32
T5_MAX_DISTANCE = 1024
GRID_W = 64
GRID_ROWS = SEQ // GRID_W
NA_ROWS = 8
NA_COLS = 16
DEEPNORM_ALPHA = 2.0 ** 0.25
LN_EPS = 1e-5
NEG_INF = -1e30
SCALE = HEAD_DIM ** -0.5

LANES = 128
PAIR = LANES // HEAD_DIM
QBLK = 128
KBLK = QBLK + 2 * SIDE
BLOCK_UNROLL = 16
VMEM_LIMIT = 56 * 1024 * 1024


def _proj_kernel(x_ref, w_ref, o_ref, xb_scr):
    @pl.when(pl.program_id(1) == 0)
    def _():
        xb_scr[...] = x_ref[...].astype(jnp.bfloat16)
    o_ref[...] = jnp.dot(xb_scr[...], w_ref[...],
                         preferred_element_type=jnp.float32).astype(o_ref.dtype)


def _input_projection(x2, wb, tm=2048, tn=512):
    M, D = x2.shape
    N = wb.shape[1]
    return pl.pallas_call(
        _proj_kernel,
        grid=(M // tm, N // tn),
        in_specs=[pl.BlockSpec((tm, D), lambda m, n: (m, 0)),
                  pl.BlockSpec((D, tn), lambda m, n: (0, n))],
        out_specs=pl.BlockSpec((tm, tn), lambda m, n: (m, n)),
        out_shape=jax.ShapeDtypeStruct((M, N), jnp.bfloat16),
        scratch_shapes=[pltpu.VMEM((tm, D), jnp.bfloat16)],
        compiler_params=pltpu.CompilerParams(
            dimension_semantics=("arbitrary", "arbitrary"), vmem_limit_bytes=VMEM_LIMIT),
        name="in_proj",
    )(x2, wb)


def _head0_lanes(rows):
    return lax.broadcasted_iota(jnp.int32, (rows, LANES), 1) < HEAD_DIM


def _softmax_block(lhs, kb, vb, bias, lane_lt):
    nq = lhs.shape[0] // 2
    s = lax.dot_general(lhs, kb, (((1,), (1,)), ((), ())),
                        preferred_element_type=jnp.float32) + bias

    def half(sh):
        m = jnp.max(sh, axis=-1, keepdims=True)
        return jnp.exp(sh - m).astype(jnp.bfloat16), m

    p0, m0 = half(s[:nq])
    p1, m1 = half(s[nq:])
    v_ones = jnp.concatenate([vb, jnp.ones_like(vb)], axis=1)
    pv = jnp.dot(jnp.concatenate([p0, p1], axis=0), v_ones, preferred_element_type=jnp.float32)
    return (jnp.where(lane_lt, pv[:nq, :LANES], pv[nq:, :LANES]), jnp.where(lane_lt, m0, m1),
            jnp.where(lane_lt, pv[:nq, LANES:], pv[nq:, LANES:]))


def _silu(g):
    return g * (1.0 / (1.0 + jnp.exp(-g)))


def _attn_a_kernel(q_ref, k_ref, v_ref, g_ref, b1_ref, b4_ref, b16_ref, y_ref,
                   f32_scr, qm_scr, kp_scr, vp_scr, num_scr, m_scr, den_scr):
    S = q_ref.shape[0]
    CH = 256

    def permute(src_ref, c, d, store):
        L = S // d
        ch = min(CH, L)
        per_r = L // ch

        def body(i, _):
            r = i // per_r
            j0 = (i % per_r) * ch
            blk = src_ref[pl.ds(r + j0 * d, ch, stride=d), :] if d > 1 else \
                src_ref[pl.ds(pl.multiple_of(i * ch, ch), ch), :]
            store(c, pl.multiple_of(i * ch, ch), ch, blk)
            return 0
        lax.fori_loop(0, S // ch, body, 0)

    def store_q(c, row0, ch, blk):
        lt = _head0_lanes(ch)
        qm_scr[c, 0, pl.ds(row0, ch), :] = jnp.where(lt, blk, 0.0).astype(jnp.bfloat16)
        qm_scr[c, 1, pl.ds(row0, ch), :] = jnp.where(lt, 0.0, blk).astype(jnp.bfloat16)

    def store_to(dst):
        def store(c, row0, ch, blk):
            dst[c - 1, pl.ds(row0, ch), :] = blk.astype(jnp.bfloat16)
        return store

    f32_scr[...] = q_ref[...].astype(jnp.float32) * SCALE
    for c, d in enumerate(DILATIONS):
        permute(f32_scr, c, d, store_q)
    f32_scr[...] = k_ref[...].astype(jnp.float32)
    for c, d in enumerate(DILATIONS[1:], start=1):
        permute(f32_scr, c, d, store_to(kp_scr))
    f32_scr[...] = v_ref[...].astype(jnp.float32)
    for c, d in enumerate(DILATIONS[1:], start=1):
        permute(f32_scr, c, d, store_to(vp_scr))

    lane_q = _head0_lanes(QBLK)

    def run_branch(c, d, bias_ref):
        L = S // d
        nblk = L // QBLK
        nk = min(KBLK, L)

        def body(i, _):
            r = i // nblk
            bi = i % nblk
            q0 = pl.multiple_of(i * QBLK, QBLK)
            lhs = jnp.concatenate([qm_scr[c, 0, pl.ds(q0, QBLK), :],
                                   qm_scr[c, 1, pl.ds(q0, QBLK), :]], axis=0)
            if nk < KBLK:
                k0 = pl.multiple_of(r * L, QBLK)
                bias = bias_ref[0]
            else:
                k0 = pl.multiple_of(r * L + jnp.clip(bi * QBLK - SIDE, 0, L - KBLK), SIDE)
                variant = jnp.where(bi == 0, 0, jnp.where(bi == nblk - 1, 2, 1))
                bias = bias_ref[variant]
            if d == 1:
                kb = k_ref[pl.ds(k0, nk), :]
                vb = v_ref[pl.ds(k0, nk), :]
            else:
                kb = kp_scr[c - 1, pl.ds(k0, nk), :]
                vb = vp_scr[c - 1, pl.ds(k0, nk), :]
            o, mb, db = _softmax_block(lhs, kb, vb, bias, lane_q)
            if d == 1:
                rows = pl.ds(q0, QBLK)
            else:
                rows = pl.ds(r + bi * QBLK * d, QBLK, stride=d)
            num_scr[c, rows, :] = o
            m_scr[c, rows, :] = mb
            den_scr[c, rows, :] = db
            return 0
        lax.fori_loop(0, S // QBLK, body, 0, unroll=BLOCK_UNROLL)

    run_branch(0, DILATIONS[0], b1_ref)
    run_branch(1, DILATIONS[1], b4_ref)
    run_branch(2, DILATIONS[2], b16_ref)

    def merge(i, _):
        rows = pl.ds(pl.multiple_of(i * CH, CH), CH)
        m0, m1, m2 = m_scr[0, rows, :], m_scr[1, rows, :], m_scr[2, rows, :]
        mm = jnp.maximum(jnp.maximum(m0, m1), m2)
        w0, w1, w2 = jnp.exp(m0 - mm), jnp.exp(m1 - mm), jnp.exp(m2 - mm)
        num = w0 * num_scr[0, rows, :] + w1 * num_scr[1, rows, :] + w2 * num_scr[2, rows, :]
        den = w0 * den_scr[0, rows, :] + w1 * den_scr[1, rows, :] + w2 * den_scr[2, rows, :]
        g = g_ref[rows, :].astype(jnp.float32)
        y_ref[rows, :] = (num / den * _silu(g)).astype(y_ref.dtype)
        return 0
    lax.fori_loop(0, S // CH, merge, 0)


def _attention_a(h, b1, b4, b16):
    B, S, _ = h.shape
    nhp = N_HEADS_A // PAIR
    blk = lambda off: pl.BlockSpec((None, S, LANES), lambda p, b: (b, 0, off + p))
    tab = lambda t: pl.BlockSpec((None,) + t.shape[1:], lambda p, b: (p, 0, 0, 0))
    return pl.pallas_call(
        _attn_a_kernel,
        grid=(nhp, B),
        in_specs=[blk(0), blk(nhp), blk(2 * nhp), blk(3 * nhp), tab(b1), tab(b4), tab(b16)],
        out_specs=pl.BlockSpec((None, S, LANES), lambda p, b: (b, 0, p)),
        out_shape=jax.ShapeDtypeStruct((B, S, WIDTH_A), jnp.bfloat16),
        scratch_shapes=[
            pltpu.VMEM((S, LANES), jnp.float32),
            pltpu.VMEM((3, PAIR, S, LANES), jnp.bfloat16),
            pltpu.VMEM((2, S, LANES), jnp.bfloat16),
            pltpu.VMEM((2, S, LANES), jnp.bfloat16),
            pltpu.VMEM((3, S, LANES), jnp.float32),
            pltpu.VMEM((3, S, LANES), jnp.float32),
            pltpu.VMEM((3, S, LANES), jnp.float32),
        ],
        compiler_params=pltpu.CompilerParams(
            dimension_semantics=("arbitrary", "arbitrary"), vmem_limit_bytes=VMEM_LIMIT),
        name="attn_dilated",
    )(h, h, h, h, b1, b4, b16)


def _attn_b_kernel(q_ref, k_ref, v_ref, g_ref, bias_ref, y_ref, qm_scr):
    S = q_ref.shape[0]
    CH = 256
    lane_lt = _head0_lanes(CH)

    def prep(i, _):
        rows = pl.ds(pl.multiple_of(i * CH, CH), CH)
        blk = q_ref[rows, :].astype(jnp.float32) * SCALE
        qm_scr[0, rows, :] = jnp.where(lane_lt, blk, 0.0).astype(jnp.bfloat16)
        qm_scr[1, rows, :] = jnp.where(lane_lt, 0.0, blk).astype(jnp.bfloat16)
        return 0
    lax.fori_loop(0, S // CH, prep, 0)

    lane_q = _head0_lanes(GRID_W)
    nk = NA_ROWS * GRID_W
    last = GRID_ROWS - NA_ROWS

    def body(r, _):
        rows = pl.ds(pl.multiple_of(r * GRID_W, GRID_W), GRID_W)
        rs = jnp.clip(r - NA_ROWS // 2, 0, last)
        variant = jnp.where(r < NA_ROWS // 2, r,
                            jnp.where(r <= last + NA_ROWS // 2, NA_ROWS // 2, r - last))
        lhs = jnp.concatenate([qm_scr[0, rows, :], qm_scr[1, rows, :]], axis=0)
        k0 = pl.multiple_of(rs * GRID_W, GRID_W)
        o, _, db = _softmax_block(lhs, k_ref[pl.ds(k0, nk), :], v_ref[pl.ds(k0, nk), :],
                                  bias_ref[variant], lane_q)
        g = g_ref[rows, :].astype(jnp.float32)
        y_ref[rows, :] = (o / db * _silu(g)).astype(y_ref.dtype)
        return 0
    lax.fori_loop(0, GRID_ROWS, body, 0, unroll=2 * BLOCK_UNROLL)


def _attention_b(h, bias_b):
    B, S, _ = h.shape
    nhp = N_HEADS_B // PAIR
    base = 4 * WIDTH_A // LANES
    blk = lambda off: pl.BlockSpec((None, S, LANES), lambda p, b: (b, 0, base + off + p))
    return pl.pallas_call(
        _attn_b_kernel,
        grid=(nhp, B),
        in_specs=[blk(0), blk(nhp), blk(2 * nhp), blk(3 * nhp),
                  pl.BlockSpec((None,) + bias_b.shape[1:], lambda p, b: (p, 0, 0, 0))],
        out_specs=pl.BlockSpec((None, S, LANES), lambda p, b: (b, 0, p)),
        out_shape=jax.ShapeDtypeStruct((B, S, WIDTH_B), jnp.bfloat16),
        scratch_shapes=[pltpu.VMEM((PAIR, S, LANES), jnp.bfloat16)],
        compiler_params=pltpu.CompilerParams(
            dimension_semantics=("arbitrary", "arbitrary"), vmem_limit_bytes=VMEM_LIMIT),
        name="attn_neighbourhood",
    )(h, h, h, h, bias_b)


def _out_kernel(ya_ref, yb_ref, wa_ref, wb_ref, x_ref, gain_ref, bias_ref, o_ref):
    out = jnp.dot(ya_ref[...], wa_ref[...], preferred_element_type=jnp.float32)
    out = out + jnp.dot(yb_ref[...], wb_ref[...], preferred_element_type=jnp.float32)
    z = DEEPNORM_ALPHA * x_ref[...] + out
    mu = jnp.mean(z, axis=-1, keepdims=True)
    zc = z - mu
    var = jnp.mean(zc * zc, axis=-1, keepdims=True)
    o_ref[...] = zc * lax.rsqrt(var + LN_EPS) * gain_ref[...] + bias_ref[...]


def _output_projection(ya, yb, wa, wb, x2, gain, bias, tm=512):
    M, D = x2.shape
    return pl.pallas_call(
        _out_kernel,
        grid=(M // tm,),
        in_specs=[pl.BlockSpec((tm, WIDTH_A), lambda i: (i, 0)),
                  pl.BlockSpec((tm, WIDTH_B), lambda i: (i, 0)),
                  pl.BlockSpec((WIDTH_A, D), lambda i: (0, 0)),
                  pl.BlockSpec((WIDTH_B, D), lambda i: (0, 0)),
                  pl.BlockSpec((tm, D), lambda i: (i, 0)),
                  pl.BlockSpec((1, D), lambda i: (0, 0)),
                  pl.BlockSpec((1, D), lambda i: (0, 0))],
        out_specs=pl.BlockSpec((tm, D), lambda i: (i, 0)),
        out_shape=jax.ShapeDtypeStruct((M, D), jnp.float32),
        compiler_params=pltpu.CompilerParams(
            dimension_semantics=("arbitrary",), vmem_limit_bytes=VMEM_LIMIT),
        name="out_proj_ln",
    )(ya, yb, wa, wb, x2, gain, bias)


def _t5_bucket_index(rel):
    half = NUM_BUCKETS // 2
    max_exact = half // 2
    n = np.abs(rel)
    large = max_exact + (np.log(np.maximum(n, 1) / max_exact)
                         / np.log(T5_MAX_DISTANCE / max_exact)
                         * (half - max_exact)).astype(np.int64)
    large = np.minimum(large, half - 1)
    return ((rel > 0).astype(np.int64) * half + np.where(n < max_exact, n, large)).astype(np.int32)


def _dilated_bucket_tables():
    tables = []
    q = np.arange(QBLK)[:, None]
    for d in DILATIONS:
        L = SEQ // d
        offsets, nk = ((0,), L) if L < KBLK else ((0, -SIDE, -2 * SIDE), KBLK)
        variants = []
        for off in offsets:
            rel = off + np.arange(nk)[None, :] - q
            variants.append(np.where(np.abs(rel) <= SIDE, _t5_bucket_index(rel * d), -1))
        tables.append(np.stack(variants).astype(np.int32))
    return tables


def _dilated_bias_kernel(t5_ref, i1_ref, i4_ref, i16_ref, b1_ref, b4_ref, b16_ref):
    p = pl.program_id(0)
    for idx_ref, out_ref in ((i1_ref, b1_ref), (i4_ref, b4_ref), (i16_ref, b16_ref)):
        for v in range(idx_ref.shape[0]):
            idx = idx_ref[v]
            for hh in range(PAIR):
                head = p * PAIR + hh

                def pick(bkt, acc):
                    return jnp.where(idx == bkt, t5_ref[bkt, head], acc)
                acc = lax.fori_loop(0, NUM_BUCKETS, pick,
                                    jnp.full(idx.shape, NEG_INF, jnp.float32))
                out_ref[v, hh * QBLK:(hh + 1) * QBLK, :] = acc


def _dilated_bias(t5_table):
    tables = [jnp.asarray(t) for t in _dilated_bucket_tables()]
    nhp = N_HEADS_A // PAIR
    whole = lambda t: pl.BlockSpec(t.shape, lambda p: (0, 0, 0))
    out_shapes = [jax.ShapeDtypeStruct((nhp, t.shape[0], PAIR * QBLK, t.shape[2]), jnp.float32)
                  for t in tables]
    return pl.pallas_call(
        _dilated_bias_kernel,
        grid=(nhp,),
        in_specs=[pl.BlockSpec(memory_space=pltpu.SMEM)] + [whole(t) for t in tables],
        out_specs=[pl.BlockSpec((None,) + o.shape[1:], lambda p: (p, 0, 0, 0)) for o in out_shapes],
        out_shape=out_shapes,
        compiler_params=pltpu.CompilerParams(dimension_semantics=("arbitrary",)),
        name="dilated_bias",
    )(t5_table, *tables)


RPB_ROWS = 2 * NA_ROWS - 1
RPB_COLS = 2 * NA_COLS - 1
NBR_VARIANT_ROWS = (tuple(range(NA_ROWS // 2)) + (NA_ROWS // 2,)
                    + tuple(range(GRID_ROWS - NA_ROWS // 2 + 1, GRID_ROWS)))


def _nbr_bias_kernel(rpb_ref, out_ref, m_scr):
    p = pl.program_id(0)
    shape = (GRID_W, LANES)
    qc = lax.broadcasted_iota(jnp.int32, shape, 0)
    lane = lax.broadcasted_iota(jnp.int32, shape, 1)
    kc = lane & (GRID_W - 1)
    qs = jnp.clip(qc - NA_COLS // 2, 0, GRID_W - NA_COLS)
    ci = jnp.where((kc >= qs) & (kc < qs + NA_COLS), kc - qc + NA_COLS - 1, -1)
    first = lane < GRID_W
    for hh in range(PAIR):
        head = p * PAIR + hh

        def fill_row(ri, _):
            base = (head * RPB_ROWS + ri) * RPB_COLS

            def pick(t, acc):
                return jnp.where(ci == t, rpb_ref[base + t], acc)
            m_scr[ri] = lax.fori_loop(0, RPB_COLS, pick, jnp.full(shape, NEG_INF, jnp.float32))
            return 0
        lax.fori_loop(0, RPB_ROWS, fill_row, 0)
        for v, r in enumerate(NBR_VARIANT_ROWS):
            rs = min(max(r - NA_ROWS // 2, 0), GRID_ROWS - NA_ROWS)
            for j2 in range(NA_ROWS * GRID_W // LANES):
                ri = rs + 2 * j2 - r + NA_ROWS - 1
                out_ref[v, hh * GRID_W:(hh + 1) * GRID_W, j2 * LANES:(j2 + 1) * LANES] = \
                    jnp.where(first, m_scr[ri], m_scr[ri + 1])


def _neighbourhood_bias(rpb):
    nhp = N_HEADS_B // PAIR
    nv = len(NBR_VARIANT_ROWS)
    out = jax.ShapeDtypeStruct((nhp, nv, PAIR * GRID_W, NA_ROWS * GRID_W), jnp.float32)
    return pl.pallas_call(
        _nbr_bias_kernel,
        grid=(nhp,),
        in_specs=[pl.BlockSpec(memory_space=pltpu.SMEM)],
        out_specs=pl.BlockSpec((None,) + out.shape[1:], lambda p: (p, 0, 0, 0)),
        out_shape=out,
        scratch_shapes=[pltpu.VMEM((RPB_ROWS, GRID_W, LANES), jnp.float32)],
        compiler_params=pltpu.CompilerParams(dimension_semantics=("arbitrary",)),
        name="nbr_bias",
    )(rpb.reshape(-1))


def kernel(x, w_in, w_out, t5_bias, na_rpb, ln_gain, ln_bias):
    B, S, D = x.shape
    depth = w_in.shape[0]
    b1, b4, b16 = _dilated_bias(t5_bias.astype(jnp.float32))
    for layer in range(depth):
        h = _input_projection(x.reshape(B * S, D), w_in[layer].astype(jnp.bfloat16))
        h = h.reshape(B, S, IN_WIDTH)
        ya = _attention_a(h, b1, b4, b16)
        yb = _attention_b(h, _neighbourhood_bias(na_rpb[layer].astype(jnp.float32)))
        wo = w_out[layer].astype(jnp.bfloat16)
        out = _output_projection(ya.reshape(B * S, WIDTH_A), yb.reshape(B * S, WIDTH_B),
                                 wo[:WIDTH_A], wo[WIDTH_A:], x.reshape(B * S, D),
                                 ln_gain[layer].reshape(1, D), ln_bias[layer].reshape(1, D))
        x = out.reshape(B, S, D)
    return x
```

```python
import functools

import numpy as np
import jax
import jax.numpy as jnp
from jax import lax
from jax.experimental import pallas as pl
from jax.experimental.pallas import tpu as pltpu

D_MODEL = 2048
SEQ = 2048
HEAD_DIM = 64
N_HEADS_A = 16
N_HEADS_B = 16
WIDTH_A = N_HEADS_A * HEAD_DIM
WIDTH_B = N_HEADS_B * HEAD_DIM
IN_WIDTH = 4 * WIDTH_A + 4 * WIDTH_B
DILATIONS = (1, 4, 16)
SIDE = 64
NUM_BUCKETS = 32
T5_MAX_DISTANCE = 1024
GRID_W = 64
GRID_ROWS = SEQ // GRID_W
NA_ROWS = 8
NA_COLS = 16
DEEPNORM_ALPHA = 2.0 ** 0.25
LN_EPS = 1e-5
NEG_INF = -1e30
LOG2E = 1.4426950408889634
Q_SCALE = HEAD_DIM ** -0.5 * LOG2E

LANES = 128
PAIR = LANES // HEAD_DIM
QBLK = 128
KBLK = QBLK + 2 * SIDE
BLOCK_UNROLL = 16
VMEM_LIMIT = 56 * 1024 * 1024


def _proj_kernel(x_ref, w_ref, o_ref, xb_scr):
    @pl.when(pl.program_id(1) == 0)
    def _():
        xb_scr[...] = x_ref[...].astype(jnp.bfloat16)
    o_ref[...] = jnp.dot(xb_scr[...], w_ref[...],
                         preferred_element_type=jnp.float32).astype(o_ref.dtype)


def _input_projection(x2, wb, tm=2048, tn=512):
    M, D = x2.shape
    N = wb.shape[1]
    return pl.pallas_call(
        _proj_kernel,
        grid=(M // tm, N // tn),
        in_specs=[pl.BlockSpec((tm, D), lambda m, n: (m, 0)),
                  pl.BlockSpec((D, tn), lambda m, n: (0, n))],
        out_specs=pl.BlockSpec((tm, tn), lambda m, n: (m, n)),
        out_shape=jax.ShapeDtypeStruct((M, N), jnp.bfloat16),
        scratch_shapes=[pltpu.VMEM((tm, D), jnp.bfloat16)],
        compiler_params=pltpu.CompilerParams(
            dimension_semantics=("arbitrary", "arbitrary"), vmem_limit_bytes=VMEM_LIMIT),
        name="in_proj",
    )(x2, wb)


def _head0_lanes(rows):
    return lax.broadcasted_iota(jnp.int32, (rows, LANES), 1) < HEAD_DIM


def _softmax_block(lhs, kb, vb, bias, lane_lt):
    s = lax.dot_general(lhs, kb, (((1,), (1,)), ((), ())),
                        preferred_element_type=jnp.float32) + bias
    return _softmax_pv(s, vb, lane_lt)


def _softmax_pv(s, vb, lane_lt):
    nq = s.shape[0] // 2

    def half(sh):
        m = jnp.max(sh, axis=-1, keepdims=True)
        return jnp.exp2(sh - m).astype(jnp.bfloat16), m

    p0, m0 = half(s[:nq])
    p1, m1 = half(s[nq:])
    v_ones = jnp.concatenate([vb, jnp.ones_like(vb)], axis=1)
    pv = jnp.dot(jnp.concatenate([p0, p1], axis=0), v_ones, preferred_element_type=jnp.float32)
    return (jnp.where(lane_lt, pv[:nq, :LANES], pv[nq:, :LANES]), jnp.where(lane_lt, m0, m1),
            jnp.where(lane_lt, pv[:nq, LANES:], pv[nq:, LANES:]))


def _attn_a_kernel(q_ref, k_ref, v_ref, g_ref, b1_ref, b4_ref, b16_ref, y_ref,
                   f32_scr, f4_scr, qm_scr, kp_scr, vp_scr, num_scr, m_scr, den_scr):
    S = q_ref.shape[0]
    CH = 256

    STEP = DILATIONS[1] // DILATIONS[0]
    assert all(b == a * STEP for a, b in zip(DILATIONS, DILATIONS[1:]))

    def regroup(emit, keep_natural):
        src, dst = f32_scr, f4_scr
        if keep_natural:
            for row0 in range(0, S, CH):
                emit(0, row0, src[row0:row0 + CH, :])
        for c in range(1, len(DILATIONS)):
            groups = DILATIONS[c - 1]
            L = S // groups
            ch = min(CH, L // STEP)
            for g in range(groups):
                for s in range(STEP):
                    for j0 in range(0, L // STEP, ch):
                        blk = src[pl.ds(g * L + s + j0 * STEP, ch, stride=STEP), :]
                        row0 = (g + groups * s) * (L // STEP) + j0
                        emit(c, row0, blk)
                        if c + 1 < len(DILATIONS):
                            dst[row0:row0 + ch, :] = blk
            src, dst = dst, src

    def emit_q(c, row0, blk):
        lt = _head0_lanes(blk.shape[0])
        rows = slice(row0, row0 + blk.shape[0])
        qm_scr[c, 0, rows, :] = jnp.where(lt, blk, 0.0).astype(jnp.bfloat16)
        qm_scr[c, 1, rows, :] = jnp.where(lt, 0.0, blk).astype(jnp.bfloat16)

    def emit_to(dst):
        def emit(c, row0, blk):
            dst[c - 1, row0:row0 + blk.shape[0], :] = blk.astype(jnp.bfloat16)
        return emit

    f32_scr[...] = q_ref[...].astype(jnp.float32) * Q_SCALE
    regroup(emit_q, True)
    f32_scr[...] = k_ref[...].astype(jnp.float32)
    regroup(emit_to(kp_scr), False)
    f32_scr[...] = v_ref[...].astype(jnp.float32)
    regroup(emit_to(vp_scr), False)

    lane_q = _head0_lanes(QBLK)

    def scores(c, d, bias_ref, i):
        L = S // d
        nblk = L // QBLK
        nk = min(KBLK, L)
        r, bi = divmod(i, nblk)
        q0 = i * QBLK
        lhs = jnp.concatenate([qm_scr[c, 0, q0:q0 + QBLK, :],
                               qm_scr[c, 1, q0:q0 + QBLK, :]], axis=0)
        if nk < KBLK:
            k0 = r * L
            bias = bias_ref[0]
        else:
            k0 = r * L + min(max(bi * QBLK - SIDE, 0), L - KBLK)
            bias = bias_ref[0 if bi == 0 else (2 if bi == nblk - 1 else 1)]
        kb = k_ref[k0:k0 + nk, :] if d == 1 else kp_scr[c - 1, k0:k0 + nk, :]
        s = lax.dot_general(lhs, kb, (((1,), (1,)), ((), ())),
                            preferred_element_type=jnp.float32) + bias
        return s, k0, nk

    def finish(c, d, i, s, k0, nk):
        vb = v_ref[k0:k0 + nk, :] if d == 1 else vp_scr[c - 1, k0:k0 + nk, :]
        return _softmax_pv(s, vb, lane_q)

    def stash(c, d, i, num, m, den):
        L = S // d
        r, bi = divmod(i, L // QBLK)
        rows = pl.ds(r + bi * QBLK * d, QBLK, stride=d)
        num_scr[c - 1, rows, :] = num
        m_scr[c - 1, rows, :] = m
        den_scr[c - 1, rows, :] = den

    def combine(c, d, i, num, m, den):
        rows = slice(i * QBLK, (i + 1) * QBLK)
        m4, m16 = m_scr[0, rows, :], m_scr[1, rows, :]
        top = jnp.maximum(jnp.maximum(m, m4), m16)
        w1, w4, w16 = jnp.exp2(m - top), jnp.exp2(m4 - top), jnp.exp2(m16 - top)
        top_num = w1 * num + w4 * num_scr[0, rows, :] + w16 * num_scr[1, rows, :]
        top_den = w1 * den + w4 * den_scr[0, rows, :] + w16 * den_scr[1, rows, :]
        g = g_ref[rows, :].astype(jnp.float32)
        y_ref[rows, :] = (top_num * g / (top_den * (1.0 + jnp.exp(-g)))).astype(y_ref.dtype)

    def run_branch(c, d, bias_ref, ahead, sink):
        n = S // QBLK
        pending = [scores(c, d, bias_ref, i) for i in range(min(ahead, n))]
        for i in range(n):
            if i + ahead < n:
                pending.append(scores(c, d, bias_ref, i + ahead))
            sink(c, d, i, *finish(c, d, i, *pending.pop(0)))

    run_branch(2, DILATIONS[2], b16_ref, 6, stash)
    run_branch(1, DILATIONS[1], b4_ref, 2, stash)
    run_branch(0, DILATIONS[0], b1_ref, 2, combine)


def _attention_a(h, b1, b4, b16):
    B, S, _ = h.shape
    nhp = N_HEADS_A // PAIR
    blk = lambda off: pl.BlockSpec((None, S, LANES), lambda p, b: (b, 0, off + p))
    tab = lambda t: pl.BlockSpec((None,) + t.shape[1:], lambda p, b: (p, 0, 0, 0))
    return pl.pallas_call(
        _attn_a_kernel,
        grid=(nhp, B),
        in_specs=[blk(0), blk(nhp), blk(2 * nhp), blk(3 * nhp), tab(b1), tab(b4), tab(b16)],
        out_specs=pl.BlockSpec((None, S, LANES), lambda p, b: (b, 0, p)),
        out_shape=jax.ShapeDtypeStruct((B, S, WIDTH_A), jnp.bfloat16),
        scratch_shapes=[
            pltpu.VMEM((S, LANES), jnp.float32),
            pltpu.VMEM((S, LANES), jnp.float32),
            pltpu.VMEM((3, PAIR, S, LANES), jnp.bfloat16),
            pltpu.VMEM((2, S, LANES), jnp.bfloat16),
            pltpu.VMEM((2, S, LANES), jnp.bfloat16),
            pltpu.VMEM((2, S, LANES), jnp.float32),
            pltpu.VMEM((2, S, LANES), jnp.float32),
            pltpu.VMEM((2, S, LANES), jnp.float32),
        ],
        compiler_params=pltpu.CompilerParams(
            dimension_semantics=("arbitrary", "arbitrary"), vmem_limit_bytes=VMEM_LIMIT),
        name="attn_dilated",
    )(h, h, h, h, b1, b4, b16)


def _attn_b_kernel(q_ref, k_ref, v_ref, g_ref, bias_ref, y_ref, qm_scr):
    S = q_ref.shape[0]
    CH = 256
    lane_lt = _head0_lanes(CH)

    def prep(i, _):
        rows = pl.ds(pl.multiple_of(i * CH, CH), CH)
        blk = q_ref[rows, :].astype(jnp.float32) * Q_SCALE
        qm_scr[0, rows, :] = jnp.where(lane_lt, blk, 0.0).astype(jnp.bfloat16)
        qm_scr[1, rows, :] = jnp.where(lane_lt, 0.0, blk).astype(jnp.bfloat16)
        return 0
    lax.fori_loop(0, S // CH, prep, 0)

    lane_q = _head0_lanes(GRID_W)
    nk = NA_ROWS * GRID_W
    last = GRID_ROWS - NA_ROWS

    for r in range(GRID_ROWS):
        rows = slice(r * GRID_W, (r + 1) * GRID_W)
        rs = min(max(r - NA_ROWS // 2, 0), last)
        variant = r if r < NA_ROWS // 2 else (NA_ROWS // 2 if r <= last + NA_ROWS // 2 else r - last)
        lhs = jnp.concatenate([qm_scr[0, rows, :], qm_scr[1, rows, :]], axis=0)
        k0 = rs * GRID_W
        o, _, db = _softmax_block(lhs, k_ref[k0:k0 + nk, :], v_ref[k0:k0 + nk, :],
                                  bias_ref[variant], lane_q)
        g = g_ref[rows, :].astype(jnp.float32)
        y_ref[rows, :] = (o * g / (db * (1.0 + jnp.exp(-g)))).astype(y_ref.dtype)


def _attention_b(h, bias_b):
    B, S, _ = h.shape
    nhp = N_HEADS_B // PAIR
    base = 4 * WIDTH_A // LANES
    blk = lambda off: pl.BlockSpec((None, S, LANES), lambda p, b: (b, 0, base + off + p))
    return pl.pallas_call(
        _attn_b_kernel,
        grid=(nhp, B),
        in_specs=[blk(0), blk(nhp), blk(2 * nhp), blk(3 * nhp),
                  pl.BlockSpec((None,) + bias_b.shape[1:], lambda p, b: (p, 0, 0, 0))],
        out_specs=pl.BlockSpec((None, S, LANES), lambda p, b: (b, 0, p)),
        out_shape=jax.ShapeDtypeStruct((B, S, WIDTH_B), jnp.bfloat16),
        scratch_shapes=[pltpu.VMEM((PAIR, S, LANES), jnp.bfloat16)],
        compiler_params=pltpu.CompilerParams(
            dimension_semantics=("arbitrary", "arbitrary"), vmem_limit_bytes=VMEM_LIMIT),
        name="attn_neighbourhood",
    )(h, h, h, h, bias_b)


def _out_kernel(ya_ref, yb_ref, wa_ref, wb_ref, x_ref, gain_ref, bias_ref, o_ref):
    out = jnp.dot(ya_ref[...], wa_ref[...], preferred_element_type=jnp.float32)
    out = out + jnp.dot(yb_ref[...], wb_ref[...], preferred_element_type=jnp.float32)
    z = DEEPNORM_ALPHA * x_ref[...] + out
    mu = jnp.mean(z, axis=-1, keepdims=True)
    zc = z - mu
    var = jnp.mean(zc * zc, axis=-1, keepdims=True)
    o_ref[...] = zc * lax.rsqrt(var + LN_EPS) * gain_ref[...] + bias_ref[...]


def _output_projection(ya, yb, wa, wb, x2, gain, bias, tm=512):
    M, D = x2.shape
    return pl.pallas_call(
        _out_kernel,
        grid=(M // tm,),
        in_specs=[pl.BlockSpec((tm, WIDTH_A), lambda i: (i, 0)),
                  pl.BlockSpec((tm, WIDTH_B), lambda i: (i, 0)),
                  pl.BlockSpec((WIDTH_A, D), lambda i: (0, 0)),
                  pl.BlockSpec((WIDTH_B, D), lambda i: (0, 0)),
                  pl.BlockSpec((tm, D), lambda i: (i, 0)),
                  pl.BlockSpec((1, D), lambda i: (0, 0)),
                  pl.BlockSpec((1, D), lambda i: (0, 0))],
        out_specs=pl.BlockSpec((tm, D), lambda i: (i, 0)),
        out_shape=jax.ShapeDtypeStruct((M, D), jnp.float32),
        compiler_params=pltpu.CompilerParams(
            dimension_semantics=("arbitrary",), vmem_limit_bytes=VMEM_LIMIT),
        name="out_proj_ln",
    )(ya, yb, wa, wb, x2, gain, bias)


def _t5_bucket_index(rel):
    half = NUM_BUCKETS // 2
    max_exact = half // 2
    n = np.abs(rel)
    large = max_exact + (np.log(np.maximum(n, 1) / max_exact)
                         / np.log(T5_MAX_DISTANCE / max_exact)
                         * (half - max_exact)).astype(np.int64)
    large = np.minimum(large, half - 1)
    return ((rel > 0).astype(np.int64) * half + np.where(n < max_exact, n, large)).astype(np.int32)


def _dilated_bucket_tables():
    tables = []
    q = np.arange(QBLK)[:, None]
    for d in DILATIONS:
        L = SEQ // d
        offsets, nk = ((0,), L) if L < KBLK else ((0, -SIDE, -2 * SIDE), KBLK)
        variants = []
        for off in offsets:
            rel = off + np.arange(nk)[None, :] - q
            variants.append(np.where(np.abs(rel) <= SIDE, _t5_bucket_index(rel * d), -1))
        tables.append(np.stack(variants).astype(np.int32))
    return tables


def _dilated_bias_kernel(t5_ref, i1_ref, i4_ref, i16_ref, b1_ref, b4_ref, b16_ref):
    p = pl.program_id(0)
    rows = 32
    for idx_ref, out_ref in ((i1_ref, b1_ref), (i4_ref, b4_ref), (i16_ref, b16_ref)):
        nv = idx_ref.shape[0]

        def chunk(i, _, idx_ref=idx_ref, out_ref=out_ref, nv=nv):
            v = i % nv
            r0 = pl.multiple_of((i // nv) * rows, rows)
            idx = idx_ref[v, pl.ds(r0, rows), :]
            for hh in range(PAIR):
                acc = jnp.full(idx.shape, NEG_INF, jnp.float32)
                for bkt in range(NUM_BUCKETS):
                    acc = jnp.where(idx == bkt, t5_ref[bkt, p * PAIR + hh] * LOG2E, acc)
                out_ref[v, pl.ds(hh * QBLK + r0, rows), :] = acc
            return 0
        lax.fori_loop(0, nv * (QBLK // rows), chunk, 0)


def _dilated_bias(t5_table):
    tables = [jnp.asarray(t) for t in _dilated_bucket_tables()]
    nhp = N_HEADS_A // PAIR
    whole = lambda t: pl.BlockSpec(t.shape, lambda p: (0, 0, 0))
    out_shapes = [jax.ShapeDtypeStruct((nhp, t.shape[0], PAIR * QBLK, t.shape[2]), jnp.float32)
                  for t in tables]
    return pl.pallas_call(
        _dilated_bias_kernel,
        grid=(nhp,),
        in_specs=[pl.BlockSpec(memory_space=pltpu.SMEM)] + [whole(t) for t in tables],
        out_specs=[pl.BlockSpec((None,) + o.shape[1:], lambda p: (p, 0, 0, 0)) for o in out_shapes],
        out_shape=out_shapes,
        compiler_params=pltpu.CompilerParams(dimension_semantics=("arbitrary",)),
        name="dilated_bias",
    )(t5_table, *tables)


RPB_ROWS = 2 * NA_ROWS - 1
RPB_COLS = 2 * NA_COLS - 1
NBR_VARIANT_ROWS = (tuple(range(NA_ROWS // 2)) + (NA_ROWS // 2,)
                    + tuple(range(GRID_ROWS - NA_ROWS // 2 + 1, GRID_ROWS)))


def _nbr_bias_kernel(rpb_ref, out_ref, m_scr):
    p = pl.program_id(0)
    shape = (GRID_W, LANES)
    qc = lax.broadcasted_iota(jnp.int32, shape, 0)
    lane = lax.broadcasted_iota(jnp.int32, shape, 1)
    kc = lane & (GRID_W - 1)
    qs = jnp.clip(qc - NA_COLS // 2, 0, GRID_W - NA_COLS)
    ci = jnp.where((kc >= qs) & (kc < qs + NA_COLS), kc - qc + NA_COLS - 1, -1)
    first = lane < GRID_W
    for hh in range(PAIR):
        head = p * PAIR + hh

        def fill_row(ri, _):
            base = (head * RPB_ROWS + ri) * RPB_COLS

            acc = jnp.full(shape, NEG_INF, jnp.float32)
            for t in range(RPB_COLS):
                acc = jnp.where(ci == t, rpb_ref[base + t] * LOG2E, acc)
            m_scr[ri] = acc
            return 0
        lax.fori_loop(0, RPB_ROWS, fill_row, 0)
        for v, r in enumerate(NBR_VARIANT_ROWS):
            rs = min(max(r - NA_ROWS // 2, 0), GRID_ROWS - NA_ROWS)
            for j2 in range(NA_ROWS * GRID_W // LANES):
                ri = rs + 2 * j2 - r + NA_ROWS - 1
                out_ref[v, hh * GRID_W:(hh + 1) * GRID_W, j2 * LANES:(j2 + 1) * LANES] = \
                    jnp.where(first, m_scr[ri], m_scr[ri + 1])


def _neighbourhood_bias(rpb):
    nhp = N_HEADS_B // PAIR
    nv = len(NBR_VARIANT_ROWS)
    out = jax.ShapeDtypeStruct((nhp, nv, PAIR * GRID_W, NA_ROWS * GRID_W), jnp.float32)
    return pl.pallas_call(
        _nbr_bias_kernel,
        grid=(nhp,),
        in_specs=[pl.BlockSpec(memory_space=pltpu.SMEM)],
        out_specs=pl.BlockSpec((None,) + out.shape[1:], lambda p: (p, 0, 0, 0)),
        out_shape=out,
        scratch_shapes=[pltpu.VMEM((RPB_ROWS, GRID_W, LANES), jnp.float32)],
        compiler_params=pltpu.CompilerParams(dimension_semantics=("arbitrary",)),
        name="nbr_bias",
    )(rpb.reshape(-1))


def kernel(x, w_in, w_out, t5_bias, na_rpb, ln_gain, ln_bias):
    B, S, D = x.shape
    depth = w_in.shape[0]
    b1, b4, b16 = _dilated_bias(t5_bias.astype(jnp.float32))
    for layer in range(depth):
        h = _input_projection(x.reshape(B * S, D), w_in[layer].astype(jnp.bfloat16))
        h = h.reshape(B, S, IN_WIDTH)
        ya = _attention_a(h, b1, b4, b16)
        yb = _attention_b(h, _neighbourhood_bias(na_rpb[layer].astype(jnp.float32)))
        wo = w_out[layer].astype(jnp.bfloat16)
        out = _output_projection(ya.reshape(B * S, WIDTH_A), yb.reshape(B * S, WIDTH_B),
                                 wo[:WIDTH_A], wo[WIDTH_A:], x.reshape(B * S, D),
                                 ln_gain[layer].reshape(1, D), ln_bias[layer].reshape(1, D))
        x = out.reshape(B, S, D)
    return x
```

```python
import functools

import numpy as np
import jax
import jax.numpy as jnp
from jax import lax
from jax.experimental import pallas as pl
from jax.experimental.pallas import tpu as pltpu

D_MODEL = 2048
SEQ = 2048
HEAD_DIM = 64
N_HEADS_A = 16
N_HEADS_B = 16
WIDTH_A = N_HEADS_A * HEAD_DIM
WIDTH_B = N_HEADS_B * HEAD_DIM
IN_WIDTH = 4 * WIDTH_A + 4 * WIDTH_B
DILATIONS = (1, 4, 16)
SIDE = 64
NUM_BUCKETS = 32
T5_MAX_DISTANCE = 1024
GRID_W = 64
GRID_ROWS = SEQ // GRID_W
NA_ROWS = 8
NA_COLS = 16
DEEPNORM_ALPHA = 2.0 ** 0.25
LN_EPS = 1e-5
NEG_INF = -1e30
LOG2E = 1.4426950408889634
Q_SCALE = HEAD_DIM ** -0.5 * LOG2E

LANES = 128
PAIR = LANES // HEAD_DIM
QBLK = 128
KBLK = QBLK + 2 * SIDE
BLOCK_UNROLL = 16
VMEM_LIMIT = 56 * 1024 * 1024


def _proj_kernel(x_ref, w_ref, o_ref, xb_scr):
    @pl.when(pl.program_id(1) == 0)
    def _():
        xb_scr[...] = x_ref[...].astype(jnp.bfloat16)
    o_ref[...] = jnp.dot(xb_scr[...], w_ref[...].astype(jnp.bfloat16),
                         preferred_element_type=jnp.float32).astype(o_ref.dtype)


def _input_projection(x2, wb, tm=2048, tn=512):
    M, D = x2.shape
    N = wb.shape[1]
    return pl.pallas_call(
        _proj_kernel,
        grid=(M // tm, N // tn),
        in_specs=[pl.BlockSpec((tm, D), lambda m, n: (m, 0)),
                  pl.BlockSpec((D, tn), lambda m, n: (0, n))],
        out_specs=pl.BlockSpec((tm, tn), lambda m, n: (m, n)),
        out_shape=jax.ShapeDtypeStruct((M, N), jnp.bfloat16),
        scratch_shapes=[pltpu.VMEM((tm, D), jnp.bfloat16)],
        compiler_params=pltpu.CompilerParams(
            dimension_semantics=("arbitrary", "arbitrary"), vmem_limit_bytes=VMEM_LIMIT),
        name="in_proj",
    )(x2, wb)


def _head0_lanes(rows):
    return lax.broadcasted_iota(jnp.int32, (rows, LANES), 1) < HEAD_DIM


def _softmax_block(lhs, kb, vb, bias, lane_lt):
    s = lax.dot_general(lhs, kb, (((1,), (1,)), ((), ())),
                        preferred_element_type=jnp.float32) + bias
    return _softmax_pv(s, vb, lane_lt)


def _softmax_pv(s, vb, lane_lt):
    nq = s.shape[0] // 2

    def half(sh):
        m = jnp.max(sh, axis=-1, keepdims=True)
        return jnp.exp2(sh - m).astype(jnp.bfloat16), m

    p0, m0 = half(s[:nq])
    p1, m1 = half(s[nq:])
    v_ones = jnp.concatenate([vb, jnp.ones_like(vb)], axis=1)
    pv = jnp.dot(jnp.concatenate([p0, p1], axis=0), v_ones, preferred_element_type=jnp.float32)
    return (jnp.where(lane_lt, pv[:nq, :LANES], pv[nq:, :LANES]), jnp.where(lane_lt, m0, m1),
            jnp.where(lane_lt, pv[:nq, LANES:], pv[nq:, LANES:]))


def _attn_a_kernel(q_ref, k_ref, v_ref, g_ref, b1_ref, b4_ref, b16_ref, y_ref,
                   f32_scr, f4_scr, qm_scr, kp_scr, vp_scr, num_scr, m_scr, den_scr):
    S = q_ref.shape[0]
    CH = 256

    STEP = DILATIONS[1] // DILATIONS[0]
    assert all(b == a * STEP for a, b in zip(DILATIONS, DILATIONS[1:]))

    def regroup(emit, keep_natural):
        src, dst = f32_scr, f4_scr
        if keep_natural:
            for row0 in range(0, S, CH):
                emit(0, row0, src[row0:row0 + CH, :])
        for c in range(1, len(DILATIONS)):
            groups = DILATIONS[c - 1]
            L = S // groups
            ch = min(CH, L // STEP)
            for g in range(groups):
                for s in range(STEP):
                    for j0 in range(0, L // STEP, ch):
                        blk = src[pl.ds(g * L + s + j0 * STEP, ch, stride=STEP), :]
                        row0 = (g + groups * s) * (L // STEP) + j0
                        emit(c, row0, blk)
                        if c + 1 < len(DILATIONS):
                            dst[row0:row0 + ch, :] = blk
            src, dst = dst, src

    def emit_q(c, row0, blk):
        lt = _head0_lanes(blk.shape[0])
        rows = slice(row0, row0 + blk.shape[0])
        qm_scr[c, 0, rows, :] = jnp.where(lt, blk, 0.0).astype(jnp.bfloat16)
        qm_scr[c, 1, rows, :] = jnp.where(lt, 0.0, blk).astype(jnp.bfloat16)

    def emit_to(dst):
        def emit(c, row0, blk):
            dst[c - 1, row0:row0 + blk.shape[0], :] = blk.astype(jnp.bfloat16)
        return emit

    f32_scr[...] = q_ref[...].astype(jnp.float32) * Q_SCALE
    regroup(emit_q, True)
    f32_scr[...] = k_ref[...].astype(jnp.float32)
    regroup(emit_to(kp_scr), False)
    f32_scr[...] = v_ref[...].astype(jnp.float32)
    regroup(emit_to(vp_scr), False)

    lane_q = _head0_lanes(QBLK)

    def scores(c, d, bias_ref, i):
        L = S // d
        nblk = L // QBLK
        nk = min(KBLK, L)
        r, bi = divmod(i, nblk)
        q0 = i * QBLK
        lhs = jnp.concatenate([qm_scr[c, 0, q0:q0 + QBLK, :],
                               qm_scr[c, 1, q0:q0 + QBLK, :]], axis=0)
        if nk < KBLK:
            k0 = r * L
            bias = bias_ref[0]
        else:
            k0 = r * L + min(max(bi * QBLK - SIDE, 0), L - KBLK)
            bias = bias_ref[0 if bi == 0 else (2 if bi == nblk - 1 else 1)]
        kb = k_ref[k0:k0 + nk, :] if d == 1 else kp_scr[c - 1, k0:k0 + nk, :]
        s = lax.dot_general(lhs, kb, (((1,), (1,)), ((), ())),
                            preferred_element_type=jnp.float32) + bias
        return s, k0, nk

    def finish(c, d, i, s, k0, nk):
        vb = v_ref[k0:k0 + nk, :] if d == 1 else vp_scr[c - 1, k0:k0 + nk, :]
        return _softmax_pv(s, vb, lane_q)

    def stash(c, d, i, num, m, den):
        L = S // d
        r, bi = divmod(i, L // QBLK)
        rows = pl.ds(r + bi * QBLK * d, QBLK, stride=d)
        num_scr[c - 1, rows, :] = num
        m_scr[c - 1, rows, :] = m
        den_scr[c - 1, rows, :] = den

    def combine(c, d, i, num, m, den):
        rows = slice(i * QBLK, (i + 1) * QBLK)
        m4, m16 = m_scr[0, rows, :], m_scr[1, rows, :]
        top = jnp.maximum(jnp.maximum(m, m4), m16)
        w1, w4, w16 = jnp.exp2(m - top), jnp.exp2(m4 - top), jnp.exp2(m16 - top)
        top_num = w1 * num + w4 * num_scr[0, rows, :] + w16 * num_scr[1, rows, :]
        top_den = w1 * den + w4 * den_scr[0, rows, :] + w16 * den_scr[1, rows, :]
        g = g_ref[rows, :].astype(jnp.float32)
        y_ref[rows, :] = (top_num * g / (top_den * (1.0 + jnp.exp(-g)))).astype(y_ref.dtype)

    def run_branch(c, d, bias_ref, ahead, sink):
        n = S // QBLK
        pending = [scores(c, d, bias_ref, i) for i in range(min(ahead, n))]
        for i in range(n):
            if i + ahead < n:
                pending.append(scores(c, d, bias_ref, i + ahead))
            sink(c, d, i, *finish(c, d, i, *pending.pop(0)))

    run_branch(2, DILATIONS[2], b16_ref, 6, stash)
    run_branch(1, DILATIONS[1], b4_ref, 2, stash)
    run_branch(0, DILATIONS[0], b1_ref, 2, combine)


def _attention_a(h, b1, b4, b16):
    B, S, _ = h.shape
    nhp = N_HEADS_A // PAIR
    blk = lambda off: pl.BlockSpec((None, S, LANES), lambda p, b: (b, 0, off + p))
    tab = lambda t: pl.BlockSpec((None,) + t.shape[1:], lambda p, b: (p, 0, 0, 0))
    return pl.pallas_call(
        _attn_a_kernel,
        grid=(nhp, B),
        in_specs=[blk(0), blk(nhp), blk(2 * nhp), blk(3 * nhp), tab(b1), tab(b4), tab(b16)],
        out_specs=pl.BlockSpec((None, S, LANES), lambda p, b: (b, 0, p)),
        out_shape=jax.ShapeDtypeStruct((B, S, WIDTH_A), jnp.bfloat16),
        scratch_shapes=[
            pltpu.VMEM((S, LANES), jnp.float32),
            pltpu.VMEM((S, LANES), jnp.float32),
            pltpu.VMEM((3, PAIR, S, LANES), jnp.bfloat16),
            pltpu.VMEM((2, S, LANES), jnp.bfloat16),
            pltpu.VMEM((2, S, LANES), jnp.bfloat16),
            pltpu.VMEM((2, S, LANES), jnp.float32),
            pltpu.VMEM((2, S, LANES), jnp.float32),
            pltpu.VMEM((2, S, LANES), jnp.float32),
        ],
        compiler_params=pltpu.CompilerParams(
            dimension_semantics=("arbitrary", "arbitrary"), vmem_limit_bytes=VMEM_LIMIT),
        name="attn_dilated",
    )(h, h, h, h, b1, b4, b16)


def _attn_b_kernel(q_ref, k_ref, v_ref, g_ref, bias_ref, y_ref, qm_scr):
    S = q_ref.shape[0]
    CH = 256
    lane_lt = _head0_lanes(CH)

    def prep(i, _):
        rows = pl.ds(pl.multiple_of(i * CH, CH), CH)
        blk = q_ref[rows, :].astype(jnp.float32) * Q_SCALE
        qm_scr[0, rows, :] = jnp.where(lane_lt, blk, 0.0).astype(jnp.bfloat16)
        qm_scr[1, rows, :] = jnp.where(lane_lt, 0.0, blk).astype(jnp.bfloat16)
        return 0
    lax.fori_loop(0, S // CH, prep, 0)

    lane_q = _head0_lanes(GRID_W)
    nk = NA_ROWS * GRID_W
    last = GRID_ROWS - NA_ROWS

    for r in range(GRID_ROWS):
        rows = slice(r * GRID_W, (r + 1) * GRID_W)
        rs = min(max(r - NA_ROWS // 2, 0), last)
        variant = r if r < NA_ROWS // 2 else (NA_ROWS // 2 if r <= last + NA_ROWS // 2 else r - last)
        lhs = jnp.concatenate([qm_scr[0, rows, :], qm_scr[1, rows, :]], axis=0)
        k0 = rs * GRID_W
        o, _, db = _softmax_block(lhs, k_ref[k0:k0 + nk, :], v_ref[k0:k0 + nk, :],
                                  bias_ref[variant], lane_q)
        g = g_ref[rows, :].astype(jnp.float32)
        y_ref[rows, :] = (o * g / (db * (1.0 + jnp.exp(-g)))).astype(y_ref.dtype)


def _attention_b(h, bias_b):
    B, S, _ = h.shape
    nhp = N_HEADS_B // PAIR
    base = 4 * WIDTH_A // LANES
    blk = lambda off: pl.BlockSpec((None, S, LANES), lambda p, b: (b, 0, base + off + p))
    return pl.pallas_call(
        _attn_b_kernel,
        grid=(nhp, B),
        in_specs=[blk(0), blk(nhp), blk(2 * nhp), blk(3 * nhp),
                  pl.BlockSpec((None,) + bias_b.shape[1:], lambda p, b: (p, 0, 0, 0))],
        out_specs=pl.BlockSpec((None, S, LANES), lambda p, b: (b, 0, p)),
        out_shape=jax.ShapeDtypeStruct((B, S, WIDTH_B), jnp.bfloat16),
        scratch_shapes=[pltpu.VMEM((PAIR, S, LANES), jnp.bfloat16)],
        compiler_params=pltpu.CompilerParams(
            dimension_semantics=("arbitrary", "arbitrary"), vmem_limit_bytes=VMEM_LIMIT),
        name="attn_neighbourhood",
    )(h, h, h, h, bias_b)


def _out_kernel(ya_ref, yb_ref, wa_ref, wb_ref, x_ref, gain_ref, bias_ref, o_ref):
    half = o_ref.shape[0] // 2
    for rows in (slice(0, half), slice(half, 2 * half)):
        out = jnp.dot(ya_ref[rows, :], wa_ref[...], preferred_element_type=jnp.float32)
        out = out + jnp.dot(yb_ref[rows, :], wb_ref[...], preferred_element_type=jnp.float32)
        z = DEEPNORM_ALPHA * x_ref[rows, :] + out
        mu = jnp.mean(z, axis=-1, keepdims=True)
        zc = z - mu
        var = jnp.mean(zc * zc, axis=-1, keepdims=True)
        o_ref[rows, :] = zc * lax.rsqrt(var + LN_EPS) * gain_ref[...] + bias_ref[...]


def _output_projection(ya, yb, wa, wb, x2, gain, bias, tm=512):
    M, D = x2.shape
    return pl.pallas_call(
        _out_kernel,
        grid=(M // tm,),
        in_specs=[pl.BlockSpec((tm, WIDTH_A), lambda i: (i, 0)),
                  pl.BlockSpec((tm, WIDTH_B), lambda i: (i, 0)),
                  pl.BlockSpec((WIDTH_A, D), lambda i: (0, 0)),
                  pl.BlockSpec((WIDTH_B, D), lambda i: (0, 0)),
                  pl.BlockSpec((tm, D), lambda i: (i, 0)),
                  pl.BlockSpec((1, D), lambda i: (0, 0)),
                  pl.BlockSpec((1, D), lambda i: (0, 0))],
        out_specs=pl.BlockSpec((tm, D), lambda i: (i, 0)),
        out_shape=jax.ShapeDtypeStruct((M, D), jnp.float32),
        compiler_params=pltpu.CompilerParams(
            dimension_semantics=("arbitrary",), vmem_limit_bytes=VMEM_LIMIT),
        name="out_proj_ln",
    )(ya, yb, wa, wb, x2, gain, bias)


def _t5_bucket_index(rel):
    half = NUM_BUCKETS // 2
    max_exact = half // 2
    n = np.abs(rel)
    large = max_exact + (np.log(np.maximum(n, 1) / max_exact)
                         / np.log(T5_MAX_DISTANCE / max_exact)
                         * (half - max_exact)).astype(np.int64)
    large = np.minimum(large, half - 1)
    return ((rel > 0).astype(np.int64) * half + np.where(n < max_exact, n, large)).astype(np.int32)


def _dilated_bucket_tables():
    tables = []
    q = np.arange(QBLK)[:, None]
    for d in DILATIONS:
        L = SEQ // d
        offsets, nk = ((0,), L) if L < KBLK else ((0, -SIDE, -2 * SIDE), KBLK)
        variants = []
        for off in offsets:
            rel = off + np.arange(nk)[None, :] - q
            variants.append(np.where(np.abs(rel) <= SIDE, _t5_bucket_index(rel * d), -1))
        tables.append(np.stack(variants).astype(np.int32))
    return tables


def _dilated_bias_kernel(t5_ref, i1_ref, i4_ref, i16_ref, b1_ref, b4_ref, b16_ref):
    p = pl.program_id(0)
    rows = 32
    for idx_ref, out_ref in ((i1_ref, b1_ref), (i4_ref, b4_ref), (i16_ref, b16_ref)):
        nv = idx_ref.shape[0]

        def chunk(i, _, idx_ref=idx_ref, out_ref=out_ref, nv=nv):
            v = i % nv
            r0 = pl.multiple_of((i // nv) * rows, rows)
            idx = idx_ref[v, pl.ds(r0, rows), :]
            for hh in range(PAIR):
                acc = jnp.full(idx.shape, NEG_INF, jnp.float32)
                for bkt in range(NUM_BUCKETS):
                    acc = jnp.where(idx == bkt, t5_ref[bkt, p * PAIR + hh] * LOG2E, acc)
                out_ref[v, pl.ds(hh * QBLK + r0, rows), :] = acc
            return 0
        lax.fori_loop(0, nv * (QBLK // rows), chunk, 0)


def _dilated_bias(t5_table):
    tables = [jnp.asarray(t) for t in _dilated_bucket_tables()]
    nhp = N_HEADS_A // PAIR
    whole = lambda t: pl.BlockSpec(t.shape, lambda p: (0, 0, 0))
    out_shapes = [jax.ShapeDtypeStruct((nhp, t.shape[0], PAIR * QBLK, t.shape[2]), jnp.float32)
                  for t in tables]
    return pl.pallas_call(
        _dilated_bias_kernel,
        grid=(nhp,),
        in_specs=[pl.BlockSpec(memory_space=pltpu.SMEM)] + [whole(t) for t in tables],
        out_specs=[pl.BlockSpec((None,) + o.shape[1:], lambda p: (p, 0, 0, 0)) for o in out_shapes],
        out_shape=out_shapes,
        compiler_params=pltpu.CompilerParams(dimension_semantics=("arbitrary",)),
        name="dilated_bias",
    )(t5_table, *tables)


RPB_ROWS = 2 * NA_ROWS - 1
RPB_COLS = 2 * NA_COLS - 1
NBR_VARIANT_ROWS = (tuple(range(NA_ROWS // 2)) + (NA_ROWS // 2,)
                    + tuple(range(GRID_ROWS - NA_ROWS // 2 + 1, GRID_ROWS)))


def _nbr_bias_kernel(rpb_ref, out_ref, m_scr):
    p = pl.program_id(0)
    shape = (GRID_W, LANES)
    qc = lax.broadcasted_iota(jnp.int32, shape, 0)
    lane = lax.broadcasted_iota(jnp.int32, shape, 1)
    kc = lane & (GRID_W - 1)
    qs = jnp.clip(qc - NA_COLS // 2, 0, GRID_W - NA_COLS)
    ci = jnp.where((kc >= qs) & (kc < qs + NA_COLS), kc - qc + NA_COLS - 1, -1)
    first = lane < GRID_W
    for hh in range(PAIR):
        head = p * PAIR + hh

        def fill_row(ri, _):
            base = (head * RPB_ROWS + ri) * RPB_COLS

            acc = jnp.full(shape, NEG_INF, jnp.float32)
            for t in range(RPB_COLS):
                acc = jnp.where(ci == t, rpb_ref[base + t] * LOG2E, acc)
            m_scr[ri] = acc
            return 0
        lax.fori_loop(0, RPB_ROWS, fill_row, 0)
        for v, r in enumerate(NBR_VARIANT_ROWS):
            rs = min(max(r - NA_ROWS // 2, 0), GRID_ROWS - NA_ROWS)
            for j2 in range(NA_ROWS * GRID_W // LANES):
                ri = rs + 2 * j2 - r + NA_ROWS - 1
                out_ref[v, hh * GRID_W:(hh + 1) * GRID_W, j2 * LANES:(j2 + 1) * LANES] = \
                    jnp.where(first, m_scr[ri], m_scr[ri + 1])


def _neighbourhood_bias(rpb):
    nhp = N_HEADS_B // PAIR
    nv = len(NBR_VARIANT_ROWS)
    out = jax.ShapeDtypeStruct((nhp, nv, PAIR * GRID_W, NA_ROWS * GRID_W), jnp.float32)
    return pl.pallas_call(
        _nbr_bias_kernel,
        grid=(nhp,),
        in_specs=[pl.BlockSpec(memory_space=pltpu.SMEM)],
        out_specs=pl.BlockSpec((None,) + out.shape[1:], lambda p: (p, 0, 0, 0)),
        out_shape=out,
        scratch_shapes=[pltpu.VMEM((RPB_ROWS, GRID_W, LANES), jnp.float32)],
        compiler_params=pltpu.CompilerParams(dimension_semantics=("arbitrary",)),
        name="nbr_bias",
    )(rpb.reshape(-1))


def kernel(x, w_in, w_out, t5_bias, na_rpb, ln_gain, ln_bias):
    B, S, D = x.shape
    depth = w_in.shape[0]
    b1, b4, b16 = _dilated_bias(t5_bias.astype(jnp.float32))
    for layer in range(depth):
        h = _input_projection(x.reshape(B * S, D), w_in[layer])
        h = h.reshape(B, S, IN_WIDTH)
        ya = _attention_a(h, b1, b4, b16)
        yb = _attention_b(h, _neighbourhood_bias(na_rpb[layer].astype(jnp.float32)))
        wo = w_out[layer].astype(jnp.bfloat16)
        out = _output_projection(ya.reshape(B * S, WIDTH_A), yb.reshape(B * S, WIDTH_B),
                                 wo[:WIDTH_A], wo[WIDTH_A:], x.reshape(B * S, D),
                                 ln_gain[layer].reshape(1, D), ln_bias[layer].reshape(1, D))
        x = out.reshape(B, S, D)
    return x
```

```python
import functools

import numpy as np
import jax
import jax.numpy as jnp
from jax import lax
from jax.experimental import pallas as pl
from jax.experimental.pallas import tpu as pltpu

D_MODEL = 2048
SEQ = 2048
HEAD_DIM = 64
N_HEADS_A = 16
N_HEADS_B = 16
WIDTH_A = N_HEADS_A * HEAD_DIM
WIDTH_B = N_HEADS_B * HEAD_DIM
IN_WIDTH = 4 * WIDTH_A + 4 * WIDTH_B
DILATIONS = (1, 4, 16)
SIDE = 64
NUM_BUCKETS = 32
T5_MAX_DISTANCE = 1024
GRID_W = 64
GRID_ROWS = SEQ // GRID_W
NA_ROWS = 8
NA_COLS = 16
DEEPNORM_ALPHA = 2.0 ** 0.25
LN_EPS = 1e-5
NEG_INF = -1e30
SHIFT_MARGIN = 1.25
SHIFT_SAMPLES = 8
SHIFT_SAMPLE_ROWS = 32
MIN_DENOMINATOR = 2.0 ** -100
MAX_DENOMINATOR = 2.0 ** 100
LOG2E = 1.4426950408889634
Q_SCALE = HEAD_DIM ** -0.5 * LOG2E

LANES = 128
PAIR = LANES // HEAD_DIM
QBLK = 128
KBLK = QBLK + 2 * SIDE
FAST_AHEAD = (2, 2, 4)
VMEM_LIMIT = 56 * 1024 * 1024


def _proj_kernel(x_ref, w_ref, o_ref, xb_scr):
    @pl.when(pl.program_id(1) == 0)
    def _():
        xb_scr[...] = x_ref[...].astype(jnp.bfloat16)
    o_ref[...] = jnp.dot(xb_scr[...], w_ref[...].astype(jnp.bfloat16),
                         preferred_element_type=jnp.float32).astype(o_ref.dtype)


def _input_projection(x2, wb, tm=2048, tn=512):
    M, D = x2.shape
    N = wb.shape[1]
    return pl.pallas_call(
        _proj_kernel,
        grid=(M // tm, N // tn),
        in_specs=[pl.BlockSpec((tm, D), lambda m, n: (m, 0)),
                  pl.BlockSpec((D, tn), lambda m, n: (0, n))],
        out_specs=pl.BlockSpec((tm, tn), lambda m, n: (m, n)),
        out_shape=jax.ShapeDtypeStruct((M, N), jnp.bfloat16),
        scratch_shapes=[pltpu.VMEM((tm, D), jnp.bfloat16)],
        compiler_params=pltpu.CompilerParams(
            dimension_semantics=("arbitrary", "arbitrary"), vmem_limit_bytes=VMEM_LIMIT),
        name="in_proj",
    )(x2, wb)


def _head0_lanes(rows):
    return lax.broadcasted_iota(jnp.int32, (rows, LANES), 1) < HEAD_DIM


def _softmax_block(lhs, kb, vb, bias, lane_lt):
    s = lax.dot_general(lhs, kb, (((1,), (1,)), ((), ())),
                        preferred_element_type=jnp.float32) + bias
    return _softmax_pv(s, vb, lane_lt)


def _softmax_pv(s, vb, lane_lt):
    nq = s.shape[0] // 2

    def half(sh):
        m = jnp.max(sh, axis=-1, keepdims=True)
        return jnp.exp2(sh - m).astype(jnp.bfloat16), m

    p0, m0 = half(s[:nq])
    p1, m1 = half(s[nq:])
    v_ones = jnp.concatenate([vb, jnp.ones_like(vb)], axis=1)
    pv = jnp.dot(jnp.concatenate([p0, p1], axis=0), v_ones, preferred_element_type=jnp.float32)
    return (jnp.where(lane_lt, pv[:nq, :LANES], pv[nq:, :LANES]), jnp.where(lane_lt, m0, m1),
            jnp.where(lane_lt, pv[:nq, LANES:], pv[nq:, LANES:]))


def _pair_norms2(xb):
    x = xb.astype(jnp.float32)
    row = lax.broadcasted_iota(jnp.int32, (LANES, 2 * LANES), 0)
    col = lax.broadcasted_iota(jnp.int32, (LANES, 2 * LANES), 1)
    pick = ((row < HEAD_DIM) == (col < LANES)).astype(jnp.bfloat16)
    return jnp.dot((x * x).astype(jnp.bfloat16), pick, preferred_element_type=jnp.float32)


def _head_max(t):
    while t.ndim > 2:
        t = jnp.max(t, axis=0)
    n = t.shape[0] // PAIR
    full = lambda x: jnp.max(jnp.max(x, axis=0, keepdims=True), axis=1, keepdims=True)
    return full(t[:n]), full(t[n:])


def _softmax_shift(q_ref, k_ref, tops):
    n_rows = q_ref.shape[0]

    def max_norm2(ref):
        rows = jnp.concatenate([ref[r0:r0 + SHIFT_SAMPLE_ROWS, :]
                                for r0 in range(0, n_rows, n_rows // SHIFT_SAMPLES)], axis=0)
        return jnp.max(_pair_norms2(rows), axis=0, keepdims=True)

    bound = jnp.sqrt(max_norm2(q_ref) * max_norm2(k_ref)) * (Q_SCALE * SHIFT_MARGIN)
    return [jnp.max(bound[:, hh * LANES:(hh + 1) * LANES], axis=1, keepdims=True)
            + functools.reduce(jnp.maximum, [t[hh] for t in tops]) for hh in range(PAIR)]


def _out_of_range(dens):
    lo = jnp.min(functools.reduce(jnp.minimum, dens))
    hi = jnp.max(functools.reduce(jnp.maximum, dens))
    return jnp.logical_not(jnp.logical_and(lo >= MIN_DENOMINATOR, hi <= MAX_DENOMINATOR))


def _attn_a_kernel(q_ref, k_ref, v_ref, g_ref, b1_ref, b4_ref, b16_ref, y_ref,
                   f32_scr, f4_scr, qm_scr, kp_scr, vp_scr, s1_scr, s4_scr, s16_scr,
                   num_scr, m_scr, den_scr):
    S = q_ref.shape[0]
    CH = 256
    bias_refs = (b1_ref, b4_ref, b16_ref)
    shifted_refs = (s1_scr, s4_scr, s16_scr)

    STEP = DILATIONS[1] // DILATIONS[0]
    assert all(b == a * STEP for a, b in zip(DILATIONS, DILATIONS[1:]))

    def regroup(emit, keep_natural):
        src, dst = f32_scr, f4_scr
        if keep_natural:
            for row0 in range(0, S, CH):
                emit(0, row0, src[row0:row0 + CH, :])
        for c in range(1, len(DILATIONS)):
            groups = DILATIONS[c - 1]
            L = S // groups
            ch = min(CH, L // STEP)
            for g in range(groups):
                for s in range(STEP):
                    for j0 in range(0, L // STEP, ch):
                        blk = src[pl.ds(g * L + s + j0 * STEP, ch, stride=STEP), :]
                        row0 = (g + groups * s) * (L // STEP) + j0
                        emit(c, row0, blk)
                        if c + 1 < len(DILATIONS):
                            dst[row0:row0 + ch, :] = blk
            src, dst = dst, src

    def emit_to(dst):
        def emit(c, row0, blk):
            dst[c - 1, row0:row0 + blk.shape[0], :] = blk.astype(jnp.bfloat16)
        return emit

    def emit_q(c, row0, blk):
        lt = _head0_lanes(blk.shape[0])
        rows = slice(row0, row0 + blk.shape[0])
        qm_scr[c, 0, rows, :] = jnp.where(lt, blk, 0.0).astype(jnp.bfloat16)
        qm_scr[c, 1, rows, :] = jnp.where(lt, 0.0, blk).astype(jnp.bfloat16)

    f32_scr[...] = q_ref[...].astype(jnp.float32) * Q_SCALE
    regroup(emit_q, True)
    f32_scr[...] = k_ref[...].astype(jnp.float32)
    regroup(emit_to(kp_scr), False)
    f32_scr[...] = v_ref[...].astype(jnp.float32)
    regroup(emit_to(vp_scr), False)

    shift = _softmax_shift(q_ref, k_ref, [_head_max(r[...]) for r in bias_refs])
    for bias_ref, shifted_ref in zip(bias_refs, shifted_refs):
        for v in range(bias_ref.shape[0]):
            for hh in range(PAIR):
                rows = slice(hh * QBLK, (hh + 1) * QBLK)
                shifted_ref[v, rows, :] = bias_ref[v, rows, :] - shift[hh]

    lane_q = _head0_lanes(QBLK)

    def block_operands(c, d, i):
        L = S // d
        nblk = L // QBLK
        nk = min(KBLK, L)
        r, bi = divmod(i, nblk)
        q0 = i * QBLK
        lhs = jnp.concatenate([qm_scr[c, 0, q0:q0 + QBLK, :],
                               qm_scr[c, 1, q0:q0 + QBLK, :]], axis=0)
        if nk < KBLK:
            k0, variant = r * L, 0
        else:
            k0 = r * L + min(max(bi * QBLK - SIDE, 0), L - KBLK)
            variant = 0 if bi == 0 else (2 if bi == nblk - 1 else 1)
        keys = slice(k0, k0 + nk)
        kb = k_ref[keys, :] if d == 1 else kp_scr[c - 1, keys, :]
        vb = v_ref[keys, :] if d == 1 else vp_scr[c - 1, keys, :]
        return lhs, kb, vb, variant

    def natural_rows(d, i):
        r, bi = divmod(i, S // d // QBLK)
        return pl.ds(r + bi * QBLK * d, QBLK, stride=d)

    def gate_store(rows, num, den):
        g = g_ref[rows, :].astype(jnp.float32)
        y_ref[rows, :] = (num * g / (den * (1.0 + jnp.exp(-g)))).astype(y_ref.dtype)

    def fast_probs(c, d, i):
        lhs, kb, vb, variant = block_operands(c, d, i)
        s = lax.dot_general(lhs, kb, (((1,), (1,)), ((), ())), preferred_element_type=jnp.float32)
        return jnp.exp2(s + shifted_refs[c][variant]).astype(jnp.bfloat16), vb

    def fast_pv(p, vb):
        pv = jnp.dot(p, jnp.concatenate([vb, jnp.ones_like(vb)], axis=1),
                     preferred_element_type=jnp.float32)
        return (jnp.where(lane_q, pv[:QBLK, :LANES], pv[QBLK:, :LANES]),
                jnp.where(lane_q, pv[:QBLK, LANES:], pv[QBLK:, LANES:]))

    nblocks = S // QBLK

    def fast_branch(c, ahead, sink):
        d = DILATIONS[c]
        pending = [fast_probs(c, d, i) for i in range(min(ahead, nblocks))]
        for i in range(nblocks):
            if i + ahead < nblocks:
                pending.append(fast_probs(c, d, i + ahead))
            sink(d, i, *fast_pv(*pending.pop(0)))

    def fast_stash(c):
        def sink(d, i, num, den):
            rows = natural_rows(d, i)
            num_scr[c - 1, rows, :] = num
            den_scr[c - 1, rows, :] = den
        return sink

    den_mins = []

    def fast_combine(d, i, num, den):
        rows = slice(i * QBLK, (i + 1) * QBLK)
        num = num + num_scr[0, rows, :] + num_scr[1, rows, :]
        den = den + den_scr[0, rows, :] + den_scr[1, rows, :]
        gate_store(rows, num, den)
        den_mins.append(den)

    fast_branch(2, FAST_AHEAD[2], fast_stash(2))
    fast_branch(1, FAST_AHEAD[1], fast_stash(1))
    fast_branch(0, FAST_AHEAD[0], fast_combine)

    @pl.when(_out_of_range(den_mins))
    def _():
        def scores(c, d, i):
            lhs, kb, vb, variant = block_operands(c, d, i)
            s = lax.dot_general(lhs, kb, (((1,), (1,)), ((), ())),
                                preferred_element_type=jnp.float32) + bias_refs[c][variant]
            return s, vb

        def run_branch(c, ahead, sink):
            d = DILATIONS[c]
            pending = [scores(c, d, i) for i in range(min(ahead, nblocks))]
            for i in range(nblocks):
                if i + ahead < nblocks:
                    pending.append(scores(c, d, i + ahead))
                s, vb = pending.pop(0)
                sink(d, i, *_softmax_pv(s, vb, lane_q))

        def stash(c):
            def sink(d, i, num, m, den):
                rows = natural_rows(d, i)
                num_scr[c - 1, rows, :] = num
                m_scr[c - 1, rows, :] = m
                den_scr[c - 1, rows, :] = den
            return sink

        def combine(d, i, num, m, den):
            rows = slice(i * QBLK, (i + 1) * QBLK)
            m4, m16 = m_scr[0, rows, :], m_scr[1, rows, :]
            top = jnp.maximum(jnp.maximum(m, m4), m16)
            w1, w4, w16 = jnp.exp2(m - top), jnp.exp2(m4 - top), jnp.exp2(m16 - top)
            gate_store(rows,
                       w1 * num + w4 * num_scr[0, rows, :] + w16 * num_scr[1, rows, :],
                       w1 * den + w4 * den_scr[0, rows, :] + w16 * den_scr[1, rows, :])

        run_branch(2, 6, stash(2))
        run_branch(1, 2, stash(1))
        run_branch(0, 2, combine)


def _attention_a(h, b1, b4, b16):
    B, S, _ = h.shape
    nhp = N_HEADS_A // PAIR
    blk = lambda off: pl.BlockSpec((None, S, LANES), lambda p, b: (b, 0, off + p))
    tab = lambda t: pl.BlockSpec((None,) + t.shape[1:], lambda p, b: (p, 0, 0, 0))
    return pl.pallas_call(
        _attn_a_kernel,
        grid=(nhp, B),
        in_specs=[blk(0), blk(nhp), blk(2 * nhp), blk(3 * nhp), tab(b1), tab(b4), tab(b16)],
        out_specs=pl.BlockSpec((None, S, LANES), lambda p, b: (b, 0, p)),
        out_shape=jax.ShapeDtypeStruct((B, S, WIDTH_A), jnp.bfloat16),
        scratch_shapes=[
            pltpu.VMEM((S, LANES), jnp.float32),
            pltpu.VMEM((S, LANES), jnp.float32),
            pltpu.VMEM((3, PAIR, S, LANES), jnp.bfloat16),
            pltpu.VMEM((2, S, LANES), jnp.bfloat16),
            pltpu.VMEM((2, S, LANES), jnp.bfloat16),
            pltpu.VMEM(b1.shape[1:], jnp.float32),
            pltpu.VMEM(b4.shape[1:], jnp.float32),
            pltpu.VMEM(b16.shape[1:], jnp.float32),
            pltpu.VMEM((2, S, LANES), jnp.float32),
            pltpu.VMEM((2, S, LANES), jnp.float32),
            pltpu.VMEM((2, S, LANES), jnp.float32),
        ],
        compiler_params=pltpu.CompilerParams(
            dimension_semantics=("arbitrary", "arbitrary"), vmem_limit_bytes=VMEM_LIMIT),
        name="attn_dilated",
    )(h, h, h, h, b1, b4, b16)


def _attn_b_kernel(q_ref, k_ref, v_ref, g_ref, bias_ref, y_ref, qm_scr, shifted_scr):
    S = q_ref.shape[0]
    CH = 256
    lane_lt = _head0_lanes(CH)

    def prep(i, _):
        rows = pl.ds(pl.multiple_of(i * CH, CH), CH)
        blk = q_ref[rows, :].astype(jnp.float32) * Q_SCALE
        qm_scr[0, rows, :] = jnp.where(lane_lt, blk, 0.0).astype(jnp.bfloat16)
        qm_scr[1, rows, :] = jnp.where(lane_lt, 0.0, blk).astype(jnp.bfloat16)
        return 0
    lax.fori_loop(0, S // CH, prep, 0)

    lane_q = _head0_lanes(GRID_W)
    nk = NA_ROWS * GRID_W
    last = GRID_ROWS - NA_ROWS

    shift = _softmax_shift(q_ref, k_ref, [_head_max(bias_ref[...])])
    for v in range(bias_ref.shape[0]):
        for hh in range(PAIR):
            half = slice(hh * GRID_W, (hh + 1) * GRID_W)
            shifted_scr[v, half, :] = bias_ref[v, half, :] - shift[hh]

    def operands(r):
        rows = slice(r * GRID_W, (r + 1) * GRID_W)
        k0 = min(max(r - NA_ROWS // 2, 0), last) * GRID_W
        variant = r if r < NA_ROWS // 2 else (NA_ROWS // 2 if r <= last + NA_ROWS // 2 else r - last)
        lhs = jnp.concatenate([qm_scr[0, rows, :], qm_scr[1, rows, :]], axis=0)
        return lhs, k_ref[k0:k0 + nk, :], v_ref[k0:k0 + nk, :], variant, rows

    def gate_store(rows, num, den):
        g = g_ref[rows, :].astype(jnp.float32)
        y_ref[rows, :] = (num * g / (den * (1.0 + jnp.exp(-g)))).astype(y_ref.dtype)

    den_mins = []
    for r in range(GRID_ROWS):
        lhs, kb, vb, variant, rows = operands(r)
        s = lax.dot_general(lhs, kb, (((1,), (1,)), ((), ())), preferred_element_type=jnp.float32)
        p = jnp.exp2(s + shifted_scr[variant]).astype(jnp.bfloat16)
        pv = jnp.dot(p, jnp.concatenate([vb, jnp.ones_like(vb)], axis=1),
                     preferred_element_type=jnp.float32)
        den = jnp.where(lane_q, pv[:GRID_W, LANES:], pv[GRID_W:, LANES:])
        gate_store(rows, jnp.where(lane_q, pv[:GRID_W, :LANES], pv[GRID_W:, :LANES]), den)
        den_mins.append(den)

    @pl.when(_out_of_range(den_mins))
    def _():
        for r in range(GRID_ROWS):
            lhs, kb, vb, variant, rows = operands(r)
            num, _, den = _softmax_block(lhs, kb, vb, bias_ref[variant], lane_q)
            gate_store(rows, num, den)


def _attention_b(h, bias_b):
    B, S, _ = h.shape
    nhp = N_HEADS_B // PAIR
    base = 4 * WIDTH_A // LANES
    blk = lambda off: pl.BlockSpec((None, S, LANES), lambda p, b: (b, 0, base + off + p))
    return pl.pallas_call(
        _attn_b_kernel,
        grid=(nhp, B),
        in_specs=[blk(0), blk(nhp), blk(2 * nhp), blk(3 * nhp),
                  pl.BlockSpec((None,) + bias_b.shape[1:], lambda p, b: (p, 0, 0, 0))],
        out_specs=pl.BlockSpec((None, S, LANES), lambda p, b: (b, 0, p)),
        out_shape=jax.ShapeDtypeStruct((B, S, WIDTH_B), jnp.bfloat16),
        scratch_shapes=[pltpu.VMEM((PAIR, S, LANES), jnp.bfloat16),
                        pltpu.VMEM(bias_b.shape[1:], jnp.float32)],
        compiler_params=pltpu.CompilerParams(
            dimension_semantics=("arbitrary", "arbitrary"), vmem_limit_bytes=VMEM_LIMIT),
        name="attn_neighbourhood",
    )(h, h, h, h, bias_b)


def _out_kernel(ya_ref, yb_ref, wa_ref, wb_ref, x_ref, gain_ref, bias_ref, o_ref):
    half = o_ref.shape[0] // 2
    for rows in (slice(0, half), slice(half, 2 * half)):
        out = jnp.dot(ya_ref[rows, :], wa_ref[...], preferred_element_type=jnp.float32)
        out = out + jnp.dot(yb_ref[rows, :], wb_ref[...], preferred_element_type=jnp.float32)
        z = DEEPNORM_ALPHA * x_ref[rows, :] + out
        mu = jnp.mean(z, axis=-1, keepdims=True)
        zc = z - mu
        var = jnp.mean(zc * zc, axis=-1, keepdims=True)
        o_ref[rows, :] = zc * lax.rsqrt(var + LN_EPS) * gain_ref[...] + bias_ref[...]


def _output_projection(ya, yb, wa, wb, x2, gain, bias, tm=512):
    M, D = x2.shape
    return pl.pallas_call(
        _out_kernel,
        grid=(M // tm,),
        in_specs=[pl.BlockSpec((tm, WIDTH_A), lambda i: (i, 0)),
                  pl.BlockSpec((tm, WIDTH_B), lambda i: (i, 0)),
                  pl.BlockSpec((WIDTH_A, D), lambda i: (0, 0)),
                  pl.BlockSpec((WIDTH_B, D), lambda i: (0, 0)),
                  pl.BlockSpec((tm, D), lambda i: (i, 0)),
                  pl.BlockSpec((1, D), lambda i: (0, 0)),
                  pl.BlockSpec((1, D), lambda i: (0, 0))],
        out_specs=pl.BlockSpec((tm, D), lambda i: (i, 0)),
        out_shape=jax.ShapeDtypeStruct((M, D), jnp.float32),
        compiler_params=pltpu.CompilerParams(
            dimension_semantics=("arbitrary",), vmem_limit_bytes=VMEM_LIMIT),
        name="out_proj_ln",
    )(ya, yb, wa, wb, x2, gain, bias)


def _t5_bucket_index(rel):
    half = NUM_BUCKETS // 2
    max_exact = half // 2
    n = np.abs(rel)
    large = max_exact + (np.log(np.maximum(n, 1) / max_exact)
                         / np.log(T5_MAX_DISTANCE / max_exact)
                         * (half - max_exact)).astype(np.int64)
    large = np.minimum(large, half - 1)
    return ((rel > 0).astype(np.int64) * half + np.where(n < max_exact, n, large)).astype(np.int32)


def _dilated_bucket_tables():
    tables = []
    q = np.arange(QBLK)[:, None]
    for d in DILATIONS:
        L = SEQ // d
        offsets, nk = ((0,), L) if L < KBLK else ((0, -SIDE, -2 * SIDE), KBLK)
        variants = []
        for off in offsets:
            rel = off + np.arange(nk)[None, :] - q
            variants.append(np.where(np.abs(rel) <= SIDE, _t5_bucket_index(rel * d), -1))
        tables.append(np.stack(variants).astype(np.int32))
    return tables


def _dilated_bias_kernel(t5_ref, i1_ref, i4_ref, i16_ref, b1_ref, b4_ref, b16_ref):
    p = pl.program_id(0)
    rows = 32
    for idx_ref, out_ref in ((i1_ref, b1_ref), (i4_ref, b4_ref), (i16_ref, b16_ref)):
        nv = idx_ref.shape[0]

        def chunk(i, _, idx_ref=idx_ref, out_ref=out_ref, nv=nv):
            v = i % nv
            r0 = pl.multiple_of((i // nv) * rows, rows)
            idx = idx_ref[v, pl.ds(r0, rows), :]
            for hh in range(PAIR):
                acc = jnp.full(idx.shape, NEG_INF, jnp.float32)
                for bkt in range(NUM_BUCKETS):
                    acc = jnp.where(idx == bkt, t5_ref[bkt, p * PAIR + hh] * LOG2E, acc)
                out_ref[v, pl.ds(hh * QBLK + r0, rows), :] = acc
            return 0
        lax.fori_loop(0, nv * (QBLK // rows), chunk, 0)


def _dilated_bias(t5_table):
    tables = [jnp.asarray(t) for t in _dilated_bucket_tables()]
    nhp = N_HEADS_A // PAIR
    whole = lambda t: pl.BlockSpec(t.shape, lambda p: (0, 0, 0))
    out_shapes = [jax.ShapeDtypeStruct((nhp, t.shape[0], PAIR * QBLK, t.shape[2]), jnp.float32)
                  for t in tables]
    return pl.pallas_call(
        _dilated_bias_kernel,
        grid=(nhp,),
        in_specs=[pl.BlockSpec(memory_space=pltpu.SMEM)] + [whole(t) for t in tables],
        out_specs=[pl.BlockSpec((None,) + o.shape[1:], lambda p: (p, 0, 0, 0)) for o in out_shapes],
        out_shape=out_shapes,
        compiler_params=pltpu.CompilerParams(dimension_semantics=("arbitrary",)),
        name="dilated_bias",
    )(t5_table, *tables)


RPB_ROWS = 2 * NA_ROWS - 1
RPB_COLS = 2 * NA_COLS - 1
NBR_VARIANT_ROWS = (tuple(range(NA_ROWS // 2)) + (NA_ROWS // 2,)
                    + tuple(range(GRID_ROWS - NA_ROWS // 2 + 1, GRID_ROWS)))


def _nbr_bias_kernel(rpb_ref, out_ref, m_scr):
    p = pl.program_id(0)
    shape = (GRID_W, LANES)
    qc = lax.broadcasted_iota(jnp.int32, shape, 0)
    lane = lax.broadcasted_iota(jnp.int32, shape, 1)
    kc = lane & (GRID_W - 1)
    qs = jnp.clip(qc - NA_COLS // 2, 0, GRID_W - NA_COLS)
    ci = jnp.where((kc >= qs) & (kc < qs + NA_COLS), kc - qc + NA_COLS - 1, -1)
    first = lane < GRID_W
    for hh in range(PAIR):
        head = p * PAIR + hh

        def fill_row(ri, _):
            base = (head * RPB_ROWS + ri) * RPB_COLS

            acc = jnp.full(shape, NEG_INF, jnp.float32)
            for t in range(RPB_COLS):
                acc = jnp.where(ci == t, rpb_ref[base + t] * LOG2E, acc)
            m_scr[ri] = acc
            return 0
        lax.fori_loop(0, RPB_ROWS, fill_row, 0)
        for v, r in enumerate(NBR_VARIANT_ROWS):
            rs = min(max(r - NA_ROWS // 2, 0), GRID_ROWS - NA_ROWS)
            for j2 in range(NA_ROWS * GRID_W // LANES):
                ri = rs + 2 * j2 - r + NA_ROWS - 1
                out_ref[v, hh * GRID_W:(hh + 1) * GRID_W, j2 * LANES:(j2 + 1) * LANES] = \
                    jnp.where(first, m_scr[ri], m_scr[ri + 1])


def _neighbourhood_bias(rpb):
    nhp = N_HEADS_B // PAIR
    nv = len(NBR_VARIANT_ROWS)
    out = jax.ShapeDtypeStruct((nhp, nv, PAIR * GRID_W, NA_ROWS * GRID_W), jnp.float32)
    return pl.pallas_call(
        _nbr_bias_kernel,
        grid=(nhp,),
        in_specs=[pl.BlockSpec(memory_space=pltpu.SMEM)],
        out_specs=pl.BlockSpec((None,) + out.shape[1:], lambda p: (p, 0, 0, 0)),
        out_shape=out,
        scratch_shapes=[pltpu.VMEM((RPB_ROWS, GRID_W, LANES), jnp.float32)],
        compiler_params=pltpu.CompilerParams(dimension_semantics=("arbitrary",)),
        name="nbr_bias",
    )(rpb.reshape(-1))


def kernel(x, w_in, w_out, t5_bias, na_rpb, ln_gain, ln_bias):
    B, S, D = x.shape
    depth = w_in.shape[0]
    b1, b4, b16 = _dilated_bias(t5_bias.astype(jnp.float32))
    for layer in range(depth):
        h = _input_projection(x.reshape(B * S, D), w_in[layer])
        h = h.reshape(B, S, IN_WIDTH)
        ya = _attention_a(h, b1, b4, b16)
        yb = _attention_b(h, _neighbourhood_bias(na_rpb[layer].astype(jnp.float32)))
        wo = w_out[layer].astype(jnp.bfloat16)
        out = _output_projection(ya.reshape(B * S, WIDTH_A), yb.reshape(B * S, WIDTH_B),
                                 wo[:WIDTH_A], wo[WIDTH_A:], x.reshape(B * S, D),
                                 ln_gain[layer].reshape(1, D), ln_bias[layer].reshape(1, D))
        x = out.reshape(B, S, D)
    return x
```

```python
import functools

import numpy as np
import jax
import jax.numpy as jnp
from jax import lax
from jax.experimental import pallas as pl
from jax.experimental.pallas import tpu as pltpu

D_MODEL = 2048
SEQ = 2048
HEAD_DIM = 64
N_HEADS_A = 16
N_HEADS_B = 16
WIDTH_A = N_HEADS_A * HEAD_DIM
WIDTH_B = N_HEADS_B * HEAD_DIM
IN_WIDTH = 4 * WIDTH_A + 4 * WIDTH_B
DILATIONS = (1, 4, 16)
SIDE = 64
NUM_BUCKETS = 32
T5_MAX_DISTANCE = 1024
GRID_W = 64
GRID_ROWS = SEQ // GRID_W
NA_ROWS = 8
NA_COLS = 16
DEEPNORM_ALPHA = 2.0 ** 0.25
LN_EPS = 1e-5
NEG_INF = -1e30
SHIFT_MARGIN = 1.25
SHIFT_SAMPLES = 8
SHIFT_SAMPLE_ROWS = 32
MIN_DENOMINATOR = 2.0 ** -100
MAX_DENOMINATOR = 2.0 ** 100
LOG2E = 1.4426950408889634
Q_SCALE = HEAD_DIM ** -0.5 * LOG2E

LANES = 128
PAIR = LANES // HEAD_DIM
QBLK = 128
KBLK = QBLK + 2 * SIDE
FAST_AHEAD = (2, 2, 4)
VMEM_LIMIT = 56 * 1024 * 1024


def _proj_kernel(x_ref, w_ref, o_ref, xb_scr):
    @pl.when(pl.program_id(1) == 0)
    def _():
        xb_scr[...] = x_ref[...].astype(jnp.bfloat16)
    o_ref[...] = jnp.dot(xb_scr[...], w_ref[...].astype(jnp.bfloat16),
                         preferred_element_type=jnp.float32).astype(o_ref.dtype)


def _input_projection(x2, wb, tm=2048, tn=512):
    M, D = x2.shape
    N = wb.shape[1]
    return pl.pallas_call(
        _proj_kernel,
        grid=(M // tm, N // tn),
        in_specs=[pl.BlockSpec((tm, D), lambda m, n: (m, 0)),
                  pl.BlockSpec((D, tn), lambda m, n: (0, n))],
        out_specs=pl.BlockSpec((tm, tn), lambda m, n: (m, n)),
        out_shape=jax.ShapeDtypeStruct((M, N), jnp.bfloat16),
        scratch_shapes=[pltpu.VMEM((tm, D), jnp.bfloat16)],
        compiler_params=pltpu.CompilerParams(
            dimension_semantics=("arbitrary", "arbitrary"), vmem_limit_bytes=VMEM_LIMIT),
        name="in_proj",
    )(x2, wb)


def _head0_lanes(rows):
    return lax.broadcasted_iota(jnp.int32, (rows, LANES), 1) < HEAD_DIM


def _softmax_block(lhs, kb, vb, bias, lane_lt):
    s = lax.dot_general(lhs, kb, (((1,), (1,)), ((), ())),
                        preferred_element_type=jnp.float32) + bias
    return _softmax_pv(s, vb, lane_lt)


def _softmax_pv(s, vb, lane_lt):
    nq = s.shape[0] // 2

    def half(sh):
        m = jnp.max(sh, axis=-1, keepdims=True)
        return jnp.exp2(sh - m).astype(jnp.bfloat16), m

    p0, m0 = half(s[:nq])
    p1, m1 = half(s[nq:])
    v_ones = jnp.concatenate([vb, jnp.ones_like(vb)], axis=1)
    pv = jnp.dot(jnp.concatenate([p0, p1], axis=0), v_ones, preferred_element_type=jnp.float32)
    return (jnp.where(lane_lt, pv[:nq, :LANES], pv[nq:, :LANES]), jnp.where(lane_lt, m0, m1),
            jnp.where(lane_lt, pv[:nq, LANES:], pv[nq:, LANES:]))


def _pair_norms2(xb):
    x = xb.astype(jnp.float32)
    row = lax.broadcasted_iota(jnp.int32, (LANES, 2 * LANES), 0)
    col = lax.broadcasted_iota(jnp.int32, (LANES, 2 * LANES), 1)
    pick = ((row < HEAD_DIM) == (col < LANES)).astype(jnp.bfloat16)
    return jnp.dot((x * x).astype(jnp.bfloat16), pick, preferred_element_type=jnp.float32)


def _head_max(t):
    while t.ndim > 2:
        t = jnp.max(t, axis=0)
    n = t.shape[0] // PAIR
    full = lambda x: jnp.max(jnp.max(x, axis=0, keepdims=True), axis=1, keepdims=True)
    return full(t[:n]), full(t[n:])


def _softmax_shift(q_ref, k_ref, tops):
    n_rows = q_ref.shape[0]

    def max_norm2(ref):
        rows = jnp.concatenate([ref[r0:r0 + SHIFT_SAMPLE_ROWS, :]
                                for r0 in range(0, n_rows, n_rows // SHIFT_SAMPLES)], axis=0)
        return jnp.max(_pair_norms2(rows), axis=0, keepdims=True)

    bound = jnp.sqrt(max_norm2(q_ref) * max_norm2(k_ref)) * (Q_SCALE * SHIFT_MARGIN)
    return [jnp.max(bound[:, hh * LANES:(hh + 1) * LANES], axis=1, keepdims=True)
            + functools.reduce(jnp.maximum, [t[hh] for t in tops]) for hh in range(PAIR)]


def _out_of_range(dens):
    lo = jnp.min(functools.reduce(jnp.minimum, dens))
    hi = jnp.max(functools.reduce(jnp.maximum, dens))
    return jnp.logical_not(jnp.logical_and(lo >= MIN_DENOMINATOR, hi <= MAX_DENOMINATOR))


def _attn_a_kernel(q_ref, k_ref, v_ref, g_ref, b1_ref, b4_ref, b16_ref, y_ref,
                   f32_scr, f4_scr, qm_scr, kp_scr, vp_scr, s1_scr, s4_scr, s16_scr,
                   num_scr, m_scr, den_scr):
    S = q_ref.shape[0]
    CH = 256
    bias_refs = (b1_ref, b4_ref, b16_ref)
    shifted_refs = (s1_scr, s4_scr, s16_scr)

    STEP = DILATIONS[1] // DILATIONS[0]
    assert all(b == a * STEP for a, b in zip(DILATIONS, DILATIONS[1:]))

    def regroup(emit, keep_natural):
        src, dst = f32_scr, f4_scr
        if keep_natural:
            for row0 in range(0, S, CH):
                emit(0, row0, src[row0:row0 + CH, :])
        for c in range(1, len(DILATIONS)):
            groups = DILATIONS[c - 1]
            L = S // groups
            ch = min(CH, L // STEP)
            for g in range(groups):
                for s in range(STEP):
                    for j0 in range(0, L // STEP, ch):
                        blk = src[pl.ds(g * L + s + j0 * STEP, ch, stride=STEP), :]
                        row0 = (g + groups * s) * (L // STEP) + j0
                        emit(c, row0, blk)
                        if c + 1 < len(DILATIONS):
                            dst[row0:row0 + ch, :] = blk
            src, dst = dst, src

    def emit_to(dst):
        def emit(c, row0, blk):
            dst[c - 1, row0:row0 + blk.shape[0], :] = blk.astype(jnp.bfloat16)
        return emit

    def emit_q(c, row0, blk):
        lt = _head0_lanes(blk.shape[0])
        rows = slice(row0, row0 + blk.shape[0])
        qm_scr[c, 0, rows, :] = jnp.where(lt, blk, 0.0).astype(jnp.bfloat16)
        qm_scr[c, 1, rows, :] = jnp.where(lt, 0.0, blk).astype(jnp.bfloat16)

    f32_scr[...] = q_ref[...].astype(jnp.float32) * Q_SCALE
    regroup(emit_q, True)
    f32_scr[...] = k_ref[...].astype(jnp.float32)
    regroup(emit_to(kp_scr), False)
    f32_scr[...] = v_ref[...].astype(jnp.float32)
    regroup(emit_to(vp_scr), False)

    shift = _softmax_shift(q_ref, k_ref, [_head_max(r[...]) for r in bias_refs])
    for bias_ref, shifted_ref in zip(bias_refs, shifted_refs):
        for v in range(bias_ref.shape[0]):
            for hh in range(PAIR):
                rows = slice(hh * QBLK, (hh + 1) * QBLK)
                shifted_ref[v, rows, :] = bias_ref[v, rows, :] - shift[hh]

    lane_q = _head0_lanes(QBLK)

    def block_operands(c, d, i):
        L = S // d
        nblk = L // QBLK
        nk = min(KBLK, L)
        r, bi = divmod(i, nblk)
        q0 = i * QBLK
        lhs = jnp.concatenate([qm_scr[c, 0, q0:q0 + QBLK, :],
                               qm_scr[c, 1, q0:q0 + QBLK, :]], axis=0)
        if nk < KBLK:
            k0, variant = r * L, 0
        else:
            k0 = r * L + min(max(bi * QBLK - SIDE, 0), L - KBLK)
            variant = 0 if bi == 0 else (2 if bi == nblk - 1 else 1)
        keys = slice(k0, k0 + nk)
        kb = k_ref[keys, :] if d == 1 else kp_scr[c - 1, keys, :]
        vb = v_ref[keys, :] if d == 1 else vp_scr[c - 1, keys, :]
        return lhs, kb, vb, variant

    def natural_rows(d, i):
        r, bi = divmod(i, S // d // QBLK)
        return pl.ds(r + bi * QBLK * d, QBLK, stride=d)

    def gate_store(rows, num, den):
        g = g_ref[rows, :].astype(jnp.float32)
        y_ref[rows, :] = (num * g / (den * (1.0 + jnp.exp(-g)))).astype(y_ref.dtype)

    def fast_probs(c, d, i):
        lhs, kb, vb, variant = block_operands(c, d, i)
        s = lax.dot_general(lhs, kb, (((1,), (1,)), ((), ())), preferred_element_type=jnp.float32)
        return jnp.exp2(s + shifted_refs[c][variant]).astype(jnp.bfloat16), vb

    def fast_pv(p, vb):
        pv = jnp.dot(p, jnp.concatenate([vb, jnp.ones_like(vb)], axis=1),
                     preferred_element_type=jnp.float32)
        return (jnp.where(lane_q, pv[:QBLK, :LANES], pv[QBLK:, :LANES]),
                jnp.where(lane_q, pv[:QBLK, LANES:], pv[QBLK:, LANES:]))

    nblocks = S // QBLK

    def fast_branch(c, ahead, sink):
        d = DILATIONS[c]
        pending = [fast_probs(c, d, i) for i in range(min(ahead, nblocks))]
        for i in range(nblocks):
            if i + ahead < nblocks:
                pending.append(fast_probs(c, d, i + ahead))
            sink(d, i, *fast_pv(*pending.pop(0)))

    def coarser_rows(c, i):
        L = S // DILATIONS[c]
        r, bi = divmod(i, L // QBLK)
        g, s = r % DILATIONS[c - 1], r // DILATIONS[c - 1]
        return pl.ds(g * L * STEP + s + bi * QBLK * STEP, QBLK, stride=STEP)

    den_mins = []

    def fast_sink(c):
        def sink(d, i, num, den):
            own = slice(i * QBLK, (i + 1) * QBLK)
            if c + 1 < len(DILATIONS):
                num = num + num_scr[c, own, :]
                den = den + den_scr[c, own, :]
            if c > 0:
                rows = coarser_rows(c, i)
                num_scr[c - 1, rows, :] = num
                den_scr[c - 1, rows, :] = den
            else:
                gate_store(own, num, den)
                den_mins.append(den)
        return sink

    for c in reversed(range(len(DILATIONS))):
        fast_branch(c, FAST_AHEAD[c], fast_sink(c))

    @pl.when(_out_of_range(den_mins))
    def _():
        def scores(c, d, i):
            lhs, kb, vb, variant = block_operands(c, d, i)
            s = lax.dot_general(lhs, kb, (((1,), (1,)), ((), ())),
                                preferred_element_type=jnp.float32) + bias_refs[c][variant]
            return s, vb

        def run_branch(c, ahead, sink):
            d = DILATIONS[c]
            pending = [scores(c, d, i) for i in range(min(ahead, nblocks))]
            for i in range(nblocks):
                if i + ahead < nblocks:
                    pending.append(scores(c, d, i + ahead))
                s, vb = pending.pop(0)
                sink(d, i, *_softmax_pv(s, vb, lane_q))

        def stash(c):
            def sink(d, i, num, m, den):
                rows = natural_rows(d, i)
                num_scr[c - 1, rows, :] = num
                m_scr[c - 1, rows, :] = m
                den_scr[c - 1, rows, :] = den
            return sink

        def combine(d, i, num, m, den):
            rows = slice(i * QBLK, (i + 1) * QBLK)
            m4, m16 = m_scr[0, rows, :], m_scr[1, rows, :]
            top = jnp.maximum(jnp.maximum(m, m4), m16)
            w1, w4, w16 = jnp.exp2(m - top), jnp.exp2(m4 - top), jnp.exp2(m16 - top)
            gate_store(rows,
                       w1 * num + w4 * num_scr[0, rows, :] + w16 * num_scr[1, rows, :],
                       w1 * den + w4 * den_scr[0, rows, :] + w16 * den_scr[1, rows, :])

        run_branch(2, 6, stash(2))
        run_branch(1, 2, stash(1))
        run_branch(0, 2, combine)


def _attention_a(h, b1, b4, b16):
    B, S, _ = h.shape
    nhp = N_HEADS_A // PAIR
    blk = lambda off: pl.BlockSpec((None, S, LANES), lambda p, b: (b, 0, off + p))
    tab = lambda t: pl.BlockSpec((None,) + t.shape[1:], lambda p, b: (p, 0, 0, 0))
    return pl.pallas_call(
        _attn_a_kernel,
        grid=(nhp, B),
        in_specs=[blk(0), blk(nhp), blk(2 * nhp), blk(3 * nhp), tab(b1), tab(b4), tab(b16)],
        out_specs=pl.BlockSpec((None, S, LANES), lambda p, b: (b, 0, p)),
        out_shape=jax.ShapeDtypeStruct((B, S, WIDTH_A), jnp.bfloat16),
        scratch_shapes=[
            pltpu.VMEM((S, LANES), jnp.float32),
            pltpu.VMEM((S, LANES), jnp.float32),
            pltpu.VMEM((3, PAIR, S, LANES), jnp.bfloat16),
            pltpu.VMEM((2, S, LANES), jnp.bfloat16),
            pltpu.VMEM((2, S, LANES), jnp.bfloat16),
            pltpu.VMEM(b1.shape[1:], jnp.float32),
            pltpu.VMEM(b4.shape[1:], jnp.float32),
            pltpu.VMEM(b16.shape[1:], jnp.float32),
            pltpu.VMEM((2, S, LANES), jnp.float32),
            pltpu.VMEM((2, S, LANES), jnp.float32),
            pltpu.VMEM((2, S, LANES), jnp.float32),
        ],
        compiler_params=pltpu.CompilerParams(
            dimension_semantics=("arbitrary", "arbitrary"), vmem_limit_bytes=VMEM_LIMIT),
        name="attn_dilated",
    )(h, h, h, h, b1, b4, b16)


def _attn_b_kernel(q_ref, k_ref, v_ref, g_ref, bias_ref, y_ref, qm_scr):
    S = q_ref.shape[0]
    CH = 256
    lane_lt = _head0_lanes(CH)

    def prep(i, _):
        rows = pl.ds(pl.multiple_of(i * CH, CH), CH)
        blk = q_ref[rows, :].astype(jnp.float32) * Q_SCALE
        qm_scr[0, rows, :] = jnp.where(lane_lt, blk, 0.0).astype(jnp.bfloat16)
        qm_scr[1, rows, :] = jnp.where(lane_lt, 0.0, blk).astype(jnp.bfloat16)
        return 0
    lax.fori_loop(0, S // CH, prep, 0)

    lane_q = _head0_lanes(GRID_W)
    nk = NA_ROWS * GRID_W
    last = GRID_ROWS - NA_ROWS

    for r in range(GRID_ROWS):
        rows = slice(r * GRID_W, (r + 1) * GRID_W)
        k0 = min(max(r - NA_ROWS // 2, 0), last) * GRID_W
        variant = r if r < NA_ROWS // 2 else (NA_ROWS // 2 if r <= last + NA_ROWS // 2 else r - last)
        lhs = jnp.concatenate([qm_scr[0, rows, :], qm_scr[1, rows, :]], axis=0)
        num, _, den = _softmax_block(lhs, k_ref[k0:k0 + nk, :], v_ref[k0:k0 + nk, :],
                                     bias_ref[variant], lane_q)
        g = g_ref[rows, :].astype(jnp.float32)
        y_ref[rows, :] = (num * g / (den * (1.0 + jnp.exp(-g)))).astype(y_ref.dtype)


def _attention_b(h, bias_b):
    B, S, _ = h.shape
    nhp = N_HEADS_B // PAIR
    base = 4 * WIDTH_A // LANES
    blk = lambda off: pl.BlockSpec((None, S, LANES), lambda p, b: (b, 0, base + off + p))
    return pl.pallas_call(
        _attn_b_kernel,
        grid=(nhp, B),
        in_specs=[blk(0), blk(nhp), blk(2 * nhp), blk(3 * nhp),
                  pl.BlockSpec((None,) + bias_b.shape[1:], lambda p, b: (p, 0, 0, 0))],
        out_specs=pl.BlockSpec((None, S, LANES), lambda p, b: (b, 0, p)),
        out_shape=jax.ShapeDtypeStruct((B, S, WIDTH_B), jnp.bfloat16),
        scratch_shapes=[pltpu.VMEM((PAIR, S, LANES), jnp.bfloat16)],
        compiler_params=pltpu.CompilerParams(
            dimension_semantics=("arbitrary", "arbitrary"), vmem_limit_bytes=VMEM_LIMIT),
        name="attn_neighbourhood",
    )(h, h, h, h, bias_b)


def _out_kernel(ya_ref, yb_ref, wa_ref, wb_ref, x_ref, gain_ref, bias_ref, o_ref):
    half = o_ref.shape[0] // 2
    for rows in (slice(0, half), slice(half, 2 * half)):
        out = jnp.dot(ya_ref[rows, :], wa_ref[...], preferred_element_type=jnp.float32)
        out = out + jnp.dot(yb_ref[rows, :], wb_ref[...], preferred_element_type=jnp.float32)
        z = DEEPNORM_ALPHA * x_ref[rows, :] + out
        mu = jnp.mean(z, axis=-1, keepdims=True)
        zc = z - mu
        var = jnp.mean(zc * zc, axis=-1, keepdims=True)
        o_ref[rows, :] = zc * lax.rsqrt(var + LN_EPS) * gain_ref[...] + bias_ref[...]


def _output_projection(ya, yb, wa, wb, x2, gain, bias, tm=512):
    M, D = x2.shape
    return pl.pallas_call(
        _out_kernel,
        grid=(M // tm,),
        in_specs=[pl.BlockSpec((tm, WIDTH_A), lambda i: (i, 0)),
                  pl.BlockSpec((tm, WIDTH_B), lambda i: (i, 0)),
                  pl.BlockSpec((WIDTH_A, D), lambda i: (0, 0)),
                  pl.BlockSpec((WIDTH_B, D), lambda i: (0, 0)),
                  pl.BlockSpec((tm, D), lambda i: (i, 0)),
                  pl.BlockSpec((1, D), lambda i: (0, 0)),
                  pl.BlockSpec((1, D), lambda i: (0, 0))],
        out_specs=pl.BlockSpec((tm, D), lambda i: (i, 0)),
        out_shape=jax.ShapeDtypeStruct((M, D), jnp.float32),
        compiler_params=pltpu.CompilerParams(
            dimension_semantics=("arbitrary",), vmem_limit_bytes=VMEM_LIMIT),
        name="out_proj_ln",
    )(ya, yb, wa, wb, x2, gain, bias)


def _t5_bucket_index(rel):
    half = NUM_BUCKETS // 2
    max_exact = half // 2
    n = np.abs(rel)
    large = max_exact + (np.log(np.maximum(n, 1) / max_exact)
                         / np.log(T5_MAX_DISTANCE / max_exact)
                         * (half - max_exact)).astype(np.int64)
    large = np.minimum(large, half - 1)
    return ((rel > 0).astype(np.int64) * half + np.where(n < max_exact, n, large)).astype(np.int32)


def _dilated_bucket_tables():
    tables = []
    q = np.arange(QBLK)[:, None]
    for d in DILATIONS:
        L = SEQ // d
        offsets, nk = ((0,), L) if L < KBLK else ((0, -SIDE, -2 * SIDE), KBLK)
        variants = []
        for off in offsets:
            rel = off + np.arange(nk)[None, :] - q
            variants.append(np.where(np.abs(rel) <= SIDE, _t5_bucket_index(rel * d), -1))
        tables.append(np.stack(variants).astype(np.int32))
    return tables


def _dilated_bias_kernel(t5_ref, i1_ref, i4_ref, i16_ref, b1_ref, b4_ref, b16_ref):
    p = pl.program_id(0)
    rows = 32
    for idx_ref, out_ref in ((i1_ref, b1_ref), (i4_ref, b4_ref), (i16_ref, b16_ref)):
        nv = idx_ref.shape[0]

        def chunk(i, _, idx_ref=idx_ref, out_ref=out_ref, nv=nv):
            v = i % nv
            r0 = pl.multiple_of((i // nv) * rows, rows)
            idx = idx_ref[v, pl.ds(r0, rows), :]
            for hh in range(PAIR):
                acc = jnp.full(idx.shape, NEG_INF, jnp.float32)
                for bkt in range(NUM_BUCKETS):
                    acc = jnp.where(idx == bkt, t5_ref[bkt, p * PAIR + hh] * LOG2E, acc)
                out_ref[v, pl.ds(hh * QBLK + r0, rows), :] = acc
            return 0
        lax.fori_loop(0, nv * (QBLK // rows), chunk, 0)


def _dilated_bias(t5_table):
    tables = [jnp.asarray(t) for t in _dilated_bucket_tables()]
    nhp = N_HEADS_A // PAIR
    whole = lambda t: pl.BlockSpec(t.shape, lambda p: (0, 0, 0))
    out_shapes = [jax.ShapeDtypeStruct((nhp, t.shape[0], PAIR * QBLK, t.shape[2]), jnp.float32)
                  for t in tables]
    return pl.pallas_call(
        _dilated_bias_kernel,
        grid=(nhp,),
        in_specs=[pl.BlockSpec(memory_space=pltpu.SMEM)] + [whole(t) for t in tables],
        out_specs=[pl.BlockSpec((None,) + o.shape[1:], lambda p: (p, 0, 0, 0)) for o in out_shapes],
        out_shape=out_shapes,
        compiler_params=pltpu.CompilerParams(dimension_semantics=("arbitrary",)),
        name="dilated_bias",
    )(t5_table, *tables)


RPB_ROWS = 2 * NA_ROWS - 1
RPB_COLS = 2 * NA_COLS - 1
NBR_VARIANT_ROWS = (tuple(range(NA_ROWS // 2)) + (NA_ROWS // 2,)
                    + tuple(range(GRID_ROWS - NA_ROWS // 2 + 1, GRID_ROWS)))


def _nbr_bias_kernel(rpb_ref, out_ref, m_scr):
    p = pl.program_id(0)
    shape = (GRID_W, LANES)
    qc = lax.broadcasted_iota(jnp.int32, shape, 0)
    lane = lax.broadcasted_iota(jnp.int32, shape, 1)
    kc = lane & (GRID_W - 1)
    qs = jnp.clip(qc - NA_COLS // 2, 0, GRID_W - NA_COLS)
    ci = jnp.where((kc >= qs) & (kc < qs + NA_COLS), kc - qc + NA_COLS - 1, -1)
    first = lane < GRID_W
    for hh in range(PAIR):
        head = p * PAIR + hh

        def fill_row(ri, _):
            base = (head * RPB_ROWS + ri) * RPB_COLS

            acc = jnp.full(shape, NEG_INF, jnp.float32)
            for t in range(RPB_COLS):
                acc = jnp.where(ci == t, rpb_ref[base + t] * LOG2E, acc)
            m_scr[ri] = acc
            return 0
        lax.fori_loop(0, RPB_ROWS, fill_row, 0)
        for v, r in enumerate(NBR_VARIANT_ROWS):
            rs = min(max(r - NA_ROWS // 2, 0), GRID_ROWS - NA_ROWS)
            for j2 in range(NA_ROWS * GRID_W // LANES):
                ri = rs + 2 * j2 - r + NA_ROWS - 1
                out_ref[v, hh * GRID_W:(hh + 1) * GRID_W, j2 * LANES:(j2 + 1) * LANES] = \
                    jnp.where(first, m_scr[ri], m_scr[ri + 1])


def _neighbourhood_bias(rpb):
    nhp = N_HEADS_B // PAIR
    nv = len(NBR_VARIANT_ROWS)
    out = jax.ShapeDtypeStruct((nhp, nv, PAIR * GRID_W, NA_ROWS * GRID_W), jnp.float32)
    return pl.pallas_call(
        _nbr_bias_kernel,
        grid=(nhp,),
        in_specs=[pl.BlockSpec(memory_space=pltpu.SMEM)],
        out_specs=pl.BlockSpec((None,) + out.shape[1:], lambda p: (p, 0, 0, 0)),
        out_shape=out,
        scratch_shapes=[pltpu.VMEM((RPB_ROWS, GRID_W, LANES), jnp.float32)],
        compiler_params=pltpu.CompilerParams(dimension_semantics=("arbitrary",)),
        name="nbr_bias",
    )(rpb.reshape(-1))


def kernel(x, w_in, w_out, t5_bias, na_rpb, ln_gain, ln_bias):
    B, S, D = x.shape
    depth = w_in.shape[0]
    b1, b4, b16 = _dilated_bias(t5_bias.astype(jnp.float32))
    for layer in range(depth):
        h = _input_projection(x.reshape(B * S, D), w_in[layer])
        h = h.reshape(B, S, IN_WIDTH)
        ya = _attention_a(h, b1, b4, b16)
        yb = _attention_b(h, _neighbourhood_bias(na_rpb[layer].astype(jnp.float32)))
        wo = w_out[layer].astype(jnp.bfloat16)
        out = _output_projection(ya.reshape(B * S, WIDTH_A), yb.reshape(B * S, WIDTH_B),
                                 wo[:WIDTH_A], wo[WIDTH_A:], x.reshape(B * S, D),
                                 ln_gain[layer].reshape(1, D), ln_bias[layer].reshape(1, D))
        x = out.reshape(B, S, D)
    return x
```

```python
import functools

import numpy as np
import jax
import jax.numpy as jnp
from jax import lax
from jax.experimental import pallas as pl
from jax.experimental.pallas import tpu as pltpu

D_MODEL = 2048
SEQ = 2048
HEAD_DIM = 64
N_HEADS_A = 16
N_HEADS_B = 16
WIDTH_A = N_HEADS_A * HEAD_DIM
WIDTH_B = N_HEADS_B * HEAD_DIM
IN_WIDTH = 4 * WIDTH_A + 4 * WIDTH_B
DILATIONS = (1, 4, 16)
SIDE = 64
NUM_BUCKETS = 32
T5_MAX_DISTANCE = 1024
GRID_W = 64
GRID_ROWS = SEQ // GRID_W
NA_ROWS = 8
NA_COLS = 16
DEEPNORM_ALPHA = 2.0 ** 0.25
LN_EPS = 1e-5
NEG_INF = -1e30
SHIFT_MARGIN = 1.25
SHIFT_SAMPLES = 8
SHIFT_SAMPLE_ROWS = 32
MIN_DENOMINATOR = 2.0 ** -100
MAX_DENOMINATOR = 2.0 ** 100
LOG2E = 1.4426950408889634
Q_SCALE = HEAD_DIM ** -0.5 * LOG2E

LANES = 128
PAIR = LANES // HEAD_DIM
QBLK = 128
KBLK = QBLK + 2 * SIDE
FAST_AHEAD = (2, 2, 4)
VMEM_LIMIT = 56 * 1024 * 1024


def _proj_kernel(x_ref, w_ref, o_ref, xb_scr):
    @pl.when(pl.program_id(1) == 0)
    def _():
        xb_scr[...] = x_ref[...].astype(jnp.bfloat16)
    o_ref[...] = jnp.dot(xb_scr[...], w_ref[...].astype(jnp.bfloat16),
                         preferred_element_type=jnp.float32).astype(o_ref.dtype)


def _input_projection(x2, wb, tm=2048, tn=512):
    M, D = x2.shape
    N = wb.shape[1]
    return pl.pallas_call(
        _proj_kernel,
        grid=(M // tm, N // tn),
        in_specs=[pl.BlockSpec((tm, D), lambda m, n: (m, 0)),
                  pl.BlockSpec((D, tn), lambda m, n: (0, n))],
        out_specs=pl.BlockSpec((tm, tn), lambda m, n: (m, n)),
        out_shape=jax.ShapeDtypeStruct((M, N), jnp.bfloat16),
        scratch_shapes=[pltpu.VMEM((tm, D), jnp.bfloat16)],
        compiler_params=pltpu.CompilerParams(
            dimension_semantics=("arbitrary", "arbitrary"), vmem_limit_bytes=VMEM_LIMIT),
        name="in_proj",
    )(x2, wb)


def _head0_lanes(rows):
    return lax.broadcasted_iota(jnp.int32, (rows, LANES), 1) < HEAD_DIM


def _softmax_block(lhs, kb, vb, bias, lane_lt):
    s = lax.dot_general(lhs, kb, (((1,), (1,)), ((), ())),
                        preferred_element_type=jnp.float32) + bias
    return _softmax_pv(s, vb, lane_lt)


def _softmax_pv(s, vb, lane_lt):
    nq = s.shape[0] // 2

    def half(sh):
        m = jnp.max(sh, axis=-1, keepdims=True)
        return jnp.exp2(sh - m).astype(jnp.bfloat16), m

    p0, m0 = half(s[:nq])
    p1, m1 = half(s[nq:])
    v_ones = jnp.concatenate([vb, jnp.ones_like(vb)], axis=1)
    pv = jnp.dot(jnp.concatenate([p0, p1], axis=0), v_ones, preferred_element_type=jnp.float32)
    return (jnp.where(lane_lt, pv[:nq, :LANES], pv[nq:, :LANES]), jnp.where(lane_lt, m0, m1),
            jnp.where(lane_lt, pv[:nq, LANES:], pv[nq:, LANES:]))


def _pair_norms2(xb):
    x = xb.astype(jnp.float32)
    row = lax.broadcasted_iota(jnp.int32, (LANES, 2 * LANES), 0)
    col = lax.broadcasted_iota(jnp.int32, (LANES, 2 * LANES), 1)
    pick = ((row < HEAD_DIM) == (col < LANES)).astype(jnp.bfloat16)
    return jnp.dot((x * x).astype(jnp.bfloat16), pick, preferred_element_type=jnp.float32)


def _head_max(t):
    while t.ndim > 2:
        t = jnp.max(t, axis=0)
    n = t.shape[0] // PAIR
    full = lambda x: jnp.max(jnp.max(x, axis=0, keepdims=True), axis=1, keepdims=True)
    return full(t[:n]), full(t[n:])


def _softmax_shift(q_ref, k_ref, tops):
    n_rows = q_ref.shape[0]

    def max_norm2(ref):
        rows = jnp.concatenate([ref[r0:r0 + SHIFT_SAMPLE_ROWS, :]
                                for r0 in range(0, n_rows, n_rows // SHIFT_SAMPLES)], axis=0)
        return jnp.max(_pair_norms2(rows), axis=0, keepdims=True)

    bound = jnp.sqrt(max_norm2(q_ref) * max_norm2(k_ref)) * (Q_SCALE * SHIFT_MARGIN)
    return [jnp.max(bound[:, hh * LANES:(hh + 1) * LANES], axis=1, keepdims=True)
            + functools.reduce(jnp.maximum, [t[hh] for t in tops]) for hh in range(PAIR)]


def _out_of_range(dens):
    lo = jnp.min(functools.reduce(jnp.minimum, dens))
    hi = jnp.max(functools.reduce(jnp.maximum, dens))
    return jnp.logical_not(jnp.logical_and(lo >= MIN_DENOMINATOR, hi <= MAX_DENOMINATOR))


def _attn_a_kernel(q_ref, k_ref, v_ref, g_ref, b1_ref, b4_ref, b16_ref, y_ref,
                   f32_scr, f4_scr, qm_scr, kp_scr, vp_scr, s1_scr, s4_scr, s16_scr,
                   num_scr, m_scr, den_scr):
    S = q_ref.shape[0]
    CH = 256
    bias_refs = (b1_ref, b4_ref, b16_ref)
    shifted_refs = (s1_scr, s4_scr, s16_scr)

    STEP = DILATIONS[1] // DILATIONS[0]
    assert all(b == a * STEP for a, b in zip(DILATIONS, DILATIONS[1:]))

    def regroup(emit, keep_natural):
        src, dst = f32_scr, f4_scr
        if keep_natural:
            for row0 in range(0, S, CH):
                emit(0, row0, src[row0:row0 + CH, :])
        for c in range(1, len(DILATIONS)):
            groups = DILATIONS[c - 1]
            L = S // groups
            ch = min(CH, L // STEP)
            for g in range(groups):
                for s in range(STEP):
                    for j0 in range(0, L // STEP, ch):
                        blk = src[pl.ds(g * L + s + j0 * STEP, ch, stride=STEP), :]
                        row0 = (g + groups * s) * (L // STEP) + j0
                        emit(c, row0, blk)
                        if c + 1 < len(DILATIONS):
                            dst[row0:row0 + ch, :] = blk
            src, dst = dst, src

    def emit_to(dst):
        def emit(c, row0, blk):
            dst[c - 1, row0:row0 + blk.shape[0], :] = blk.astype(jnp.bfloat16)
        return emit

    def emit_q(c, row0, blk):
        lt = _head0_lanes(blk.shape[0])
        rows = slice(row0, row0 + blk.shape[0])
        qm_scr[c, 0, rows, :] = jnp.where(lt, blk, 0.0).astype(jnp.bfloat16)
        qm_scr[c, 1, rows, :] = jnp.where(lt, 0.0, blk).astype(jnp.bfloat16)

    f32_scr[...] = q_ref[...].astype(jnp.float32) * Q_SCALE
    regroup(emit_q, True)
    f32_scr[...] = k_ref[...].astype(jnp.float32)
    regroup(emit_to(kp_scr), False)
    f32_scr[...] = v_ref[...].astype(jnp.float32)
    regroup(emit_to(vp_scr), False)

    shift = _softmax_shift(q_ref, k_ref, [_head_max(r[...]) for r in bias_refs])
    for bias_ref, shifted_ref in zip(bias_refs, shifted_refs):
        for v in range(bias_ref.shape[0]):
            for hh in range(PAIR):
                rows = slice(hh * QBLK, (hh + 1) * QBLK)
                shifted_ref[v, rows, :] = bias_ref[v, rows, :] - shift[hh]

    lane_q = _head0_lanes(QBLK)

    def block_operands(c, d, i):
        L = S // d
        nblk = L // QBLK
        nk = min(KBLK, L)
        r, bi = divmod(i, nblk)
        q0 = i * QBLK
        lhs = jnp.concatenate([qm_scr[c, 0, q0:q0 + QBLK, :],
                               qm_scr[c, 1, q0:q0 + QBLK, :]], axis=0)
        if nk < KBLK:
            k0, variant = r * L, 0
        else:
            k0 = r * L + min(max(bi * QBLK - SIDE, 0), L - KBLK)
            variant = 0 if bi == 0 else (2 if bi == nblk - 1 else 1)
        keys = slice(k0, k0 + nk)
        kb = k_ref[keys, :] if d == 1 else kp_scr[c - 1, keys, :]
        vb = v_ref[keys, :] if d == 1 else vp_scr[c - 1, keys, :]
        return lhs, kb, vb, variant

    def natural_rows(d, i):
        r, bi = divmod(i, S // d // QBLK)
        return pl.ds(r + bi * QBLK * d, QBLK, stride=d)

    def gate_store(rows, num, den):
        g = g_ref[rows, :].astype(jnp.float32)
        y_ref[rows, :] = (num * g / (den * (1.0 + jnp.exp(-g)))).astype(y_ref.dtype)

    def fast_probs(c, d, i):
        lhs, kb, vb, variant = block_operands(c, d, i)
        s = lax.dot_general(lhs, kb, (((1,), (1,)), ((), ())), preferred_element_type=jnp.float32)
        return jnp.exp2(s + shifted_refs[c][variant]).astype(jnp.bfloat16), vb

    def fast_pv(p, vb):
        pv = jnp.dot(p, jnp.concatenate([vb, jnp.ones_like(vb)], axis=1),
                     preferred_element_type=jnp.float32)
        return (jnp.where(lane_q, pv[:QBLK, :LANES], pv[QBLK:, :LANES]),
                jnp.where(lane_q, pv[:QBLK, LANES:], pv[QBLK:, LANES:]))

    nblocks = S // QBLK

    def fast_branch(c, ahead, sink):
        d = DILATIONS[c]
        pending = [fast_probs(c, d, i) for i in range(min(ahead, nblocks))]
        for i in range(nblocks):
            if i + ahead < nblocks:
                pending.append(fast_probs(c, d, i + ahead))
            sink(d, i, *fast_pv(*pending.pop(0)))

    def coarser_rows(c, i):
        L = S // DILATIONS[c]
        r, bi = divmod(i, L // QBLK)
        g, s = r % DILATIONS[c - 1], r // DILATIONS[c - 1]
        return pl.ds(g * L * STEP + s + bi * QBLK * STEP, QBLK, stride=STEP)

    den_mins = []

    def fast_sink(c):
        def sink(d, i, num, den):
            own = slice(i * QBLK, (i + 1) * QBLK)
            if c + 1 < len(DILATIONS):
                num = num + num_scr[c, own, :]
                den = den + den_scr[c, own, :]
            if c > 0:
                rows = coarser_rows(c, i)
                num_scr[c - 1, rows, :] = num
                den_scr[c - 1, rows, :] = den
            else:
                gate_store(own, num, den)
                den_mins.append(den)
        return sink

    for c in reversed(range(len(DILATIONS))):
        fast_branch(c, FAST_AHEAD[c], fast_sink(c))

    @pl.when(_out_of_range(den_mins))
    def _():
        def scores(c, d, i):
            lhs, kb, vb, variant = block_operands(c, d, i)
            s = lax.dot_general(lhs, kb, (((1,), (1,)), ((), ())),
                                preferred_element_type=jnp.float32) + bias_refs[c][variant]
            return s, vb

        def run_branch(c, ahead, sink):
            d = DILATIONS[c]
            pending = [scores(c, d, i) for i in range(min(ahead, nblocks))]
            for i in range(nblocks):
                if i + ahead < nblocks:
                    pending.append(scores(c, d, i + ahead))
                s, vb = pending.pop(0)
                sink(d, i, *_softmax_pv(s, vb, lane_q))

        def stash(c):
            def sink(d, i, num, m, den):
                rows = natural_rows(d, i)
                num_scr[c - 1, rows, :] = num
                m_scr[c - 1, rows, :] = m
                den_scr[c - 1, rows, :] = den
            return sink

        def combine(d, i, num, m, den):
            rows = slice(i * QBLK, (i + 1) * QBLK)
            m4, m16 = m_scr[0, rows, :], m_scr[1, rows, :]
            top = jnp.maximum(jnp.maximum(m, m4), m16)
            w1, w4, w16 = jnp.exp2(m - top), jnp.exp2(m4 - top), jnp.exp2(m16 - top)
            gate_store(rows,
                       w1 * num + w4 * num_scr[0, rows, :] + w16 * num_scr[1, rows, :],
                       w1 * den + w4 * den_scr[0, rows, :] + w16 * den_scr[1, rows, :])

        run_branch(2, 6, stash(2))
        run_branch(1, 2, stash(1))
        run_branch(0, 2, combine)


def _attention_a(h, b1, b4, b16):
    B, S, _ = h.shape
    nhp = N_HEADS_A // PAIR
    blk = lambda off: pl.BlockSpec((None, S, LANES), lambda p, b: (b, 0, off + p))
    tab = lambda t: pl.BlockSpec((None,) + t.shape[1:], lambda p, b: (p, 0, 0, 0))
    return pl.pallas_call(
        _attn_a_kernel,
        grid=(nhp, B),
        in_specs=[blk(0), blk(nhp), blk(2 * nhp), blk(3 * nhp), tab(b1), tab(b4), tab(b16)],
        out_specs=pl.BlockSpec((None, S, LANES), lambda p, b: (b, 0, p)),
        out_shape=jax.ShapeDtypeStruct((B, S, WIDTH_A), jnp.bfloat16),
        scratch_shapes=[
            pltpu.VMEM((S, LANES), jnp.float32),
            pltpu.VMEM((S, LANES), jnp.float32),
            pltpu.VMEM((3, PAIR, S, LANES), jnp.bfloat16),
            pltpu.VMEM((2, S, LANES), jnp.bfloat16),
            pltpu.VMEM((2, S, LANES), jnp.bfloat16),
            pltpu.VMEM(b1.shape[1:], jnp.float32),
            pltpu.VMEM(b4.shape[1:], jnp.float32),
            pltpu.VMEM(b16.shape[1:], jnp.float32),
            pltpu.VMEM((2, S, LANES), jnp.float32),
            pltpu.VMEM((2, S, LANES), jnp.float32),
            pltpu.VMEM((2, S, LANES), jnp.float32),
        ],
        compiler_params=pltpu.CompilerParams(
            dimension_semantics=("arbitrary", "arbitrary"), vmem_limit_bytes=VMEM_LIMIT),
        name="attn_dilated",
    )(h, h, h, h, b1, b4, b16)


def _attn_b_kernel(q_ref, k_ref, v_ref, g_ref, bias_ref, y_ref, qm_scr):
    S = q_ref.shape[0]
    CH = 256
    lane_lt = _head0_lanes(CH)

    def prep(i, _):
        rows = pl.ds(pl.multiple_of(i * CH, CH), CH)
        blk = q_ref[rows, :].astype(jnp.float32) * Q_SCALE
        qm_scr[0, rows, :] = jnp.where(lane_lt, blk, 0.0).astype(jnp.bfloat16)
        qm_scr[1, rows, :] = jnp.where(lane_lt, 0.0, blk).astype(jnp.bfloat16)
        return 0
    lax.fori_loop(0, S // CH, prep, 0)

    lane_q = _head0_lanes(GRID_W)
    nk = NA_ROWS * GRID_W
    last = GRID_ROWS - NA_ROWS

    for r in range(GRID_ROWS):
        rows = slice(r * GRID_W, (r + 1) * GRID_W)
        k0 = min(max(r - NA_ROWS // 2, 0), last) * GRID_W
        variant = r if r < NA_ROWS // 2 else (NA_ROWS // 2 if r <= last + NA_ROWS // 2 else r - last)
        lhs = jnp.concatenate([qm_scr[0, rows, :], qm_scr[1, rows, :]], axis=0)
        num, _, den = _softmax_block(lhs, k_ref[k0:k0 + nk, :], v_ref[k0:k0 + nk, :],
                                     bias_ref[variant], lane_q)
        g = g_ref[rows, :].astype(jnp.float32)
        y_ref[rows, :] = (num * g / (den * (1.0 + jnp.exp(-g)))).astype(y_ref.dtype)


def _attention_b(h, bias_b):
    B, S, _ = h.shape
    nhp = N_HEADS_B // PAIR
    base = 4 * WIDTH_A // LANES
    blk = lambda off: pl.BlockSpec((None, S, LANES), lambda p, b: (b, 0, base + off + p))
    return pl.pallas_call(
        _attn_b_kernel,
        grid=(nhp, B),
        in_specs=[blk(0), blk(nhp), blk(2 * nhp), blk(3 * nhp),
                  pl.BlockSpec((None,) + bias_b.shape[1:], lambda p, b: (p, 0, 0, 0))],
        out_specs=pl.BlockSpec((None, S, LANES), lambda p, b: (b, 0, p)),
        out_shape=jax.ShapeDtypeStruct((B, S, WIDTH_B), jnp.bfloat16),
        scratch_shapes=[pltpu.VMEM((PAIR, S, LANES), jnp.bfloat16)],
        compiler_params=pltpu.CompilerParams(
            dimension_semantics=("arbitrary", "arbitrary"), vmem_limit_bytes=VMEM_LIMIT),
        name="attn_neighbourhood",
    )(h, h, h, h, bias_b)


def _out_kernel(ya_ref, yb_ref, wa_ref, wb_ref, x_ref, gain_ref, bias_ref, o_ref):
    half = o_ref.shape[0] // 2
    for rows in (slice(0, half), slice(half, 2 * half)):
        out = jnp.dot(ya_ref[rows, :], wa_ref[...], preferred_element_type=jnp.float32)
        out = out + jnp.dot(yb_ref[rows, :], wb_ref[...], preferred_element_type=jnp.float32)
        z = DEEPNORM_ALPHA * x_ref[rows, :] + out
        mu = jnp.mean(z, axis=-1, keepdims=True)
        zc = z - mu
        var = jnp.mean(zc * zc, axis=-1, keepdims=True)
        o_ref[rows, :] = zc * lax.rsqrt(var + LN_EPS) * gain_ref[...] + bias_ref[...]


def _output_projection(ya, yb, wa, wb, x2, gain, bias, tm=512):
    M, D = x2.shape
    return pl.pallas_call(
        _out_kernel,
        grid=(M // tm,),
        in_specs=[pl.BlockSpec((tm, WIDTH_A), lambda i: (i, 0)),
                  pl.BlockSpec((tm, WIDTH_B), lambda i: (i, 0)),
                  pl.BlockSpec((WIDTH_A, D), lambda i: (0, 0)),
                  pl.BlockSpec((WIDTH_B, D), lambda i: (0, 0)),
                  pl.BlockSpec((tm, D), lambda i: (i, 0)),
                  pl.BlockSpec((1, D), lambda i: (0, 0)),
                  pl.BlockSpec((1, D), lambda i: (0, 0))],
        out_specs=pl.BlockSpec((tm, D), lambda i: (i, 0)),
        out_shape=jax.ShapeDtypeStruct((M, D), jnp.float32),
        compiler_params=pltpu.CompilerParams(
            dimension_semantics=("arbitrary",), vmem_limit_bytes=VMEM_LIMIT),
        name="out_proj_ln",
    )(ya, yb, wa, wb, x2, gain, bias)


def _t5_bucket_index(rel):
    half = NUM_BUCKETS // 2
    max_exact = half // 2
    n = np.abs(rel)
    large = max_exact + (np.log(np.maximum(n, 1) / max_exact)
                         / np.log(T5_MAX_DISTANCE / max_exact)
                         * (half - max_exact)).astype(np.int64)
    large = np.minimum(large, half - 1)
    return ((rel > 0).astype(np.int64) * half + np.where(n < max_exact, n, large)).astype(np.int32)


BAND_LANES = 4 * LANES
BAND_ZERO = BAND_LANES // 2
SUBLANES = 8


def _dilated_band_buckets():
    lane = np.arange(BAND_LANES)
    out = np.empty((len(DILATIONS), 2, SUBLANES, BAND_LANES), np.int32)
    for c, d in enumerate(DILATIONS):
        for j, start in enumerate((0, SIDE)):
            rel = lane - BAND_ZERO - start
            out[c, j] = np.where(np.abs(rel) <= SIDE, _t5_bucket_index(rel * d), -1)[None, :]
    return out


def _dilated_bias_kernel(t5_ref, band_ref, b1_ref, b4_ref, b16_ref):
    p = pl.program_id(0)
    for c, out_ref in enumerate((b1_ref, b4_ref, b16_ref)):
        nv, _, nk = out_ref.shape
        for hh in range(PAIR):
            rolled = []
            for j in range(min(nv, 2)):
                idx = band_ref[c, j]
                profile = jnp.full(idx.shape, NEG_INF, jnp.float32)
                for bkt in range(NUM_BUCKETS):
                    profile = jnp.where(idx == bkt, t5_ref[bkt, p * PAIR + hh] * LOG2E, profile)
                rows = jnp.concatenate([profile] * (QBLK // SUBLANES), axis=0)
                rolled.append(pltpu.roll(rows, 0, 1, stride=1, stride_axis=0))
            own = slice(hh * QBLK, (hh + 1) * QBLK)
            out_ref[0, own, :] = rolled[0][:, BAND_ZERO:BAND_ZERO + nk]
            if nv > 1:
                out_ref[1, own, :] = rolled[1][:, BAND_ZERO:BAND_ZERO + nk]
                out_ref[2, own, :] = rolled[0][:, BAND_ZERO - 2 * SIDE:BAND_ZERO - 2 * SIDE + nk]


def _dilated_bias(t5_table):
    band = jnp.asarray(_dilated_band_buckets())
    nhp = N_HEADS_A // PAIR
    out_shapes = []
    for d in DILATIONS:
        L = SEQ // d
        nv, nk = (1, L) if L < KBLK else (3, KBLK)
        out_shapes.append(jax.ShapeDtypeStruct((nhp, nv, PAIR * QBLK, nk), jnp.float32))
    return pl.pallas_call(
        _dilated_bias_kernel,
        grid=(nhp,),
        in_specs=[pl.BlockSpec(memory_space=pltpu.SMEM),
                  pl.BlockSpec(band.shape, lambda p: (0, 0, 0, 0))],
        out_specs=[pl.BlockSpec((None,) + o.shape[1:], lambda p: (p, 0, 0, 0)) for o in out_shapes],
        out_shape=out_shapes,
        compiler_params=pltpu.CompilerParams(dimension_semantics=("arbitrary",)),
        name="dilated_bias",
    )(t5_table, band)


RPB_ROWS = 2 * NA_ROWS - 1
RPB_COLS = 2 * NA_COLS - 1
NBR_VARIANT_ROWS = (tuple(range(NA_ROWS // 2)) + (NA_ROWS // 2,)
                    + tuple(range(GRID_ROWS - NA_ROWS // 2 + 1, GRID_ROWS)))


def _nbr_bias_kernel(rpb_ref, out_ref, m_scr):
    p = pl.program_id(0)
    shape = (GRID_W, LANES)
    qc = lax.broadcasted_iota(jnp.int32, shape, 0)
    lane = lax.broadcasted_iota(jnp.int32, shape, 1)
    kc = lane & (GRID_W - 1)
    qs = jnp.clip(qc - NA_COLS // 2, 0, GRID_W - NA_COLS)
    in_window = (kc >= qs) & (kc < qs + NA_COLS)
    first = lane < GRID_W
    u = lax.broadcasted_iota(jnp.int32, (SUBLANES, LANES), 1)
    delta = jnp.where(u < NA_COLS, u, jnp.where(u >= LANES - NA_COLS, u - LANES, u - GRID_W))
    col = jnp.where(jnp.abs(delta) < NA_COLS, delta + NA_COLS - 1, -1)
    for hh in range(PAIR):
        head = p * PAIR + hh

        for ri in range(RPB_ROWS):
            base = (head * RPB_ROWS + ri) * RPB_COLS
            profile = jnp.full(col.shape, NEG_INF, jnp.float32)
            for t in range(RPB_COLS):
                profile = jnp.where(col == t, rpb_ref[base + t] * LOG2E, profile)
            rows = jnp.concatenate([profile] * (GRID_W // SUBLANES), axis=0)
            rolled = pltpu.roll(rows, 0, 1, stride=1, stride_axis=0)
            m_scr[ri] = jnp.where(in_window, rolled, NEG_INF)
        for v, r in enumerate(NBR_VARIANT_ROWS):
            rs = min(max(r - NA_ROWS // 2, 0), GRID_ROWS - NA_ROWS)
            for j2 in range(NA_ROWS * GRID_W // LANES):
                ri = rs + 2 * j2 - r + NA_ROWS - 1
                out_ref[v, hh * GRID_W:(hh + 1) * GRID_W, j2 * LANES:(j2 + 1) * LANES] = \
                    jnp.where(first, m_scr[ri], m_scr[ri + 1])


def _neighbourhood_bias(rpb):
    nhp = N_HEADS_B // PAIR
    nv = len(NBR_VARIANT_ROWS)
    out = jax.ShapeDtypeStruct((nhp, nv, PAIR * GRID_W, NA_ROWS * GRID_W), jnp.float32)
    return pl.pallas_call(
        _nbr_bias_kernel,
        grid=(nhp,),
        in_specs=[pl.BlockSpec(memory_space=pltpu.SMEM)],
        out_specs=pl.BlockSpec((None,) + out.shape[1:], lambda p: (p, 0, 0, 0)),
        out_shape=out,
        scratch_shapes=[pltpu.VMEM((RPB_ROWS, GRID_W, LANES), jnp.float32)],
        compiler_params=pltpu.CompilerParams(dimension_semantics=("arbitrary",)),
        name="nbr_bias",
    )(rpb.reshape(-1))


def kernel(x, w_in, w_out, t5_bias, na_rpb, ln_gain, ln_bias):
    B, S, D = x.shape
    depth = w_in.shape[0]
    b1, b4, b16 = _dilated_bias(t5_bias.astype(jnp.float32))
    for layer in range(depth):
        h = _input_projection(x.reshape(B * S, D), w_in[layer])
        h = h.reshape(B, S, IN_WIDTH)
        ya = _attention_a(h, b1, b4, b16)
        yb = _attention_b(h, _neighbourhood_bias(na_rpb[layer].astype(jnp.float32)))
        wo = w_out[layer].astype(jnp.bfloat16)
        out = _output_projection(ya.reshape(B * S, WIDTH_A), yb.reshape(B * S, WIDTH_B),
                                 wo[:WIDTH_A], wo[WIDTH_A:], x.reshape(B * S, D),
                                 ln_gain[layer].reshape(1, D), ln_bias[layer].reshape(1, D))
        x = out.reshape(B, S, D)
    return x
```

```python
import functools

import numpy as np
import jax
import jax.numpy as jnp
from jax import lax
from jax.experimental import pallas as pl
from jax.experimental.pallas import tpu as pltpu

D_MODEL = 2048
SEQ = 2048
HEAD_DIM = 64
N_HEADS_A = 16
N_HEADS_B = 16
WIDTH_A = N_HEADS_A * HEAD_DIM
WIDTH_B = N_HEADS_B * HEAD_DIM
IN_WIDTH = 4 * WIDTH_A + 4 * WIDTH_B
DILATIONS = (1, 4, 16)
SIDE = 64
NUM_BUCKETS = 32
T5_MAX_DISTANCE = 1024
GRID_W = 64
GRID_ROWS = SEQ // GRID_W
NA_ROWS = 8
NA_COLS = 16
DEEPNORM_ALPHA = 2.0 ** 0.25
LN_EPS = 1e-5
NEG_INF = -1e30
SHIFT_MARGIN = 1.25
SHIFT_SAMPLES = 8
SHIFT_SAMPLE_ROWS = 32
MIN_DENOMINATOR = 2.0 ** -100
MAX_DENOMINATOR = 2.0 ** 100
LOG2E = 1.4426950408889634
Q_SCALE = HEAD_DIM ** -0.5 * LOG2E

LANES = 128
SUBLANES = 8
PAIR = LANES // HEAD_DIM
QBLK = 128
KBLK = QBLK + 2 * SIDE
FAST_AHEAD = (2, 2, 4)
VMEM_LIMIT = 56 * 1024 * 1024


def _proj_kernel(x_ref, w_ref, o_ref, xb_scr):
    @pl.when(pl.program_id(1) == 0)
    def _():
        xb_scr[...] = x_ref[...].astype(jnp.bfloat16)
    o_ref[...] = jnp.dot(xb_scr[...], w_ref[...].astype(jnp.bfloat16),
                         preferred_element_type=jnp.float32).astype(o_ref.dtype)


def _input_projection(x2, wb, tm=2048, tn=512):
    M, D = x2.shape
    N = wb.shape[1]
    return pl.pallas_call(
        _proj_kernel,
        grid=(M // tm, N // tn),
        in_specs=[pl.BlockSpec((tm, D), lambda m, n: (m, 0)),
                  pl.BlockSpec((D, tn), lambda m, n: (0, n))],
        out_specs=pl.BlockSpec((tm, tn), lambda m, n: (m, n)),
        out_shape=jax.ShapeDtypeStruct((M, N), jnp.bfloat16),
        scratch_shapes=[pltpu.VMEM((tm, D), jnp.bfloat16)],
        compiler_params=pltpu.CompilerParams(
            dimension_semantics=("arbitrary", "arbitrary"), vmem_limit_bytes=VMEM_LIMIT),
        name="in_proj",
    )(x2, wb)


def _head0_lanes(rows):
    return lax.broadcasted_iota(jnp.int32, (rows, LANES), 1) < HEAD_DIM


def _softmax_block(lhs, kb, vb, bias, lane_lt):
    s = lax.dot_general(lhs, kb, (((1,), (1,)), ((), ())),
                        preferred_element_type=jnp.float32) + bias
    return _softmax_pv(s, vb, lane_lt)


def _softmax_pv(s, vb, lane_lt):
    nq = s.shape[0] // 2

    def half(sh):
        m = jnp.max(sh, axis=-1, keepdims=True)
        return jnp.exp2(sh - m).astype(jnp.bfloat16), m

    p0, m0 = half(s[:nq])
    p1, m1 = half(s[nq:])
    v_ones = jnp.concatenate([vb, jnp.ones_like(vb)], axis=1)
    pv = jnp.dot(jnp.concatenate([p0, p1], axis=0), v_ones, preferred_element_type=jnp.float32)
    return (jnp.where(lane_lt, pv[:nq, :LANES], pv[nq:, :LANES]), jnp.where(lane_lt, m0, m1),
            jnp.where(lane_lt, pv[:nq, LANES:], pv[nq:, LANES:]))


def _pair_norms2(xb):
    x = xb.astype(jnp.float32)
    row = lax.broadcasted_iota(jnp.int32, (LANES, 2 * LANES), 0)
    col = lax.broadcasted_iota(jnp.int32, (LANES, 2 * LANES), 1)
    pick = ((row < HEAD_DIM) == (col < LANES)).astype(jnp.bfloat16)
    return jnp.dot((x * x).astype(jnp.bfloat16), pick, preferred_element_type=jnp.float32)


def _head_max(t):
    while t.ndim > 2:
        t = jnp.max(t, axis=0)
    n = t.shape[0] // PAIR
    full = lambda x: jnp.max(jnp.max(x, axis=0, keepdims=True), axis=1, keepdims=True)
    return full(t[:n]), full(t[n:])


def _softmax_shift(q_ref, k_ref, tops):
    n_rows = q_ref.shape[0]

    def max_norm2(ref):
        rows = jnp.concatenate([ref[r0:r0 + SHIFT_SAMPLE_ROWS, :]
                                for r0 in range(0, n_rows, n_rows // SHIFT_SAMPLES)], axis=0)
        return jnp.max(_pair_norms2(rows), axis=0, keepdims=True)

    bound = jnp.sqrt(max_norm2(q_ref) * max_norm2(k_ref)) * (Q_SCALE * SHIFT_MARGIN)
    return [jnp.max(bound[:, hh * LANES:(hh + 1) * LANES], axis=1, keepdims=True)
            + functools.reduce(jnp.maximum, [t[hh] for t in tops]) for hh in range(PAIR)]


def _out_of_range(dens):
    lo = jnp.min(functools.reduce(jnp.minimum, dens))
    hi = jnp.max(functools.reduce(jnp.maximum, dens))
    return jnp.logical_not(jnp.logical_and(lo >= MIN_DENOMINATOR, hi <= MAX_DENOMINATOR))


def _attn_a_kernel(q_ref, k_ref, v_ref, g_ref, b1_ref, b4_ref, b16_ref, y_ref,
                   f32_scr, f4_scr, qm_scr, kp_scr, vp_scr, s1_scr, s4_scr, s16_scr,
                   num_scr, m_scr, den_scr):
    S = q_ref.shape[0]
    CH = 256
    bias_refs = (b1_ref, b4_ref, b16_ref)
    shifted_refs = (s1_scr, s4_scr, s16_scr)

    STEP = DILATIONS[1] // DILATIONS[0]
    assert all(b == a * STEP for a, b in zip(DILATIONS, DILATIONS[1:]))

    def regroup(emit, keep_natural):
        src, dst = f32_scr, f4_scr
        if keep_natural:
            for row0 in range(0, S, CH):
                emit(0, row0, src[row0:row0 + CH, :])
        for c in range(1, len(DILATIONS)):
            groups = DILATIONS[c - 1]
            L = S // groups
            ch = min(CH, L // STEP)
            for g in range(groups):
                for s in range(STEP):
                    for j0 in range(0, L // STEP, ch):
                        blk = src[pl.ds(g * L + s + j0 * STEP, ch, stride=STEP), :]
                        row0 = (g + groups * s) * (L // STEP) + j0
                        emit(c, row0, blk)
                        if c + 1 < len(DILATIONS):
                            dst[row0:row0 + ch, :] = blk
            src, dst = dst, src

    def emit_to(dst):
        def emit(c, row0, blk):
            dst[c - 1, row0:row0 + blk.shape[0], :] = blk.astype(jnp.bfloat16)
        return emit

    def emit_q(c, row0, blk):
        lt = _head0_lanes(blk.shape[0])
        rows = slice(row0, row0 + blk.shape[0])
        qm_scr[c, 0, rows, :] = jnp.where(lt, blk, 0.0).astype(jnp.bfloat16)
        qm_scr[c, 1, rows, :] = jnp.where(lt, 0.0, blk).astype(jnp.bfloat16)

    f32_scr[...] = q_ref[...].astype(jnp.float32) * Q_SCALE
    regroup(emit_q, True)
    f32_scr[...] = k_ref[...].astype(jnp.float32)
    regroup(emit_to(kp_scr), False)
    f32_scr[...] = v_ref[...].astype(jnp.float32)
    regroup(emit_to(vp_scr), False)

    shift = _softmax_shift(q_ref, k_ref, [_head_max(r[...]) for r in bias_refs])
    for bias_ref, shifted_ref in zip(bias_refs, shifted_refs):
        for v in range(bias_ref.shape[0]):
            for hh in range(PAIR):
                rows = slice(hh * QBLK, (hh + 1) * QBLK)
                shifted_ref[v, rows, :] = bias_ref[v, rows, :] - shift[hh]

    lane_q = _head0_lanes(QBLK)

    def block_operands(c, d, i):
        L = S // d
        nblk = L // QBLK
        nk = min(KBLK, L)
        r, bi = divmod(i, nblk)
        q0 = i * QBLK
        lhs = jnp.concatenate([qm_scr[c, 0, q0:q0 + QBLK, :],
                               qm_scr[c, 1, q0:q0 + QBLK, :]], axis=0)
        if nk < KBLK:
            k0, variant = r * L, 0
        else:
            k0 = r * L + min(max(bi * QBLK - SIDE, 0), L - KBLK)
            variant = 0 if bi == 0 else (2 if bi == nblk - 1 else 1)
        keys = slice(k0, k0 + nk)
        kb = k_ref[keys, :] if d == 1 else kp_scr[c - 1, keys, :]
        vb = v_ref[keys, :] if d == 1 else vp_scr[c - 1, keys, :]
        return lhs, kb, vb, variant

    def natural_rows(d, i):
        r, bi = divmod(i, S // d // QBLK)
        return pl.ds(r + bi * QBLK * d, QBLK, stride=d)

    def gate_store(rows, num, den):
        g = g_ref[rows, :].astype(jnp.float32)
        y_ref[rows, :] = (num * g / (den * (1.0 + jnp.exp(-g)))).astype(y_ref.dtype)

    def fast_probs(c, d, i):
        lhs, kb, vb, variant = block_operands(c, d, i)
        s = lax.dot_general(lhs, kb, (((1,), (1,)), ((), ())), preferred_element_type=jnp.float32)
        return jnp.exp2(s + shifted_refs[c][variant]).astype(jnp.bfloat16), vb

    def fast_pv(p, vb):
        pv = jnp.dot(p, jnp.concatenate([vb, jnp.ones_like(vb)], axis=1),
                     preferred_element_type=jnp.float32)
        return (jnp.where(lane_q, pv[:QBLK, :LANES], pv[QBLK:, :LANES]),
                jnp.where(lane_q, pv[:QBLK, LANES:], pv[QBLK:, LANES:]))

    nblocks = S // QBLK

    def fast_branch(c, ahead, sink):
        d = DILATIONS[c]
        pending = [fast_probs(c, d, i) for i in range(min(ahead, nblocks))]
        for i in range(nblocks):
            if i + ahead < nblocks:
                pending.append(fast_probs(c, d, i + ahead))
            sink(d, i, *fast_pv(*pending.pop(0)))

    def coarser_rows(c, i):
        L = S // DILATIONS[c]
        r, bi = divmod(i, L // QBLK)
        g, s = r % DILATIONS[c - 1], r // DILATIONS[c - 1]
        return pl.ds(g * L * STEP + s + bi * QBLK * STEP, QBLK, stride=STEP)

    den_mins = []

    def fast_sink(c):
        def sink(d, i, num, den):
            own = slice(i * QBLK, (i + 1) * QBLK)
            if c + 1 < len(DILATIONS):
                num = num + num_scr[c, own, :]
                den = den + den_scr[c, own, :]
            if c > 0:
                rows = coarser_rows(c, i)
                num_scr[c - 1, rows, :] = num
                den_scr[c - 1, rows, :] = den
            else:
                gate_store(own, num, den)
                den_mins.append(den)
        return sink

    for c in reversed(range(len(DILATIONS))):
        fast_branch(c, FAST_AHEAD[c], fast_sink(c))

    @pl.when(_out_of_range(den_mins))
    def _():
        def scores(c, d, i):
            lhs, kb, vb, variant = block_operands(c, d, i)
            s = lax.dot_general(lhs, kb, (((1,), (1,)), ((), ())),
                                preferred_element_type=jnp.float32) + bias_refs[c][variant]
            return s, vb

        def run_branch(c, ahead, sink):
            d = DILATIONS[c]
            pending = [scores(c, d, i) for i in range(min(ahead, nblocks))]
            for i in range(nblocks):
                if i + ahead < nblocks:
                    pending.append(scores(c, d, i + ahead))
                s, vb = pending.pop(0)
                sink(d, i, *_softmax_pv(s, vb, lane_q))

        def stash(c):
            def sink(d, i, num, m, den):
                rows = natural_rows(d, i)
                num_scr[c - 1, rows, :] = num
                m_scr[c - 1, rows, :] = m
                den_scr[c - 1, rows, :] = den
            return sink

        def combine(d, i, num, m, den):
            rows = slice(i * QBLK, (i + 1) * QBLK)
            m4, m16 = m_scr[0, rows, :], m_scr[1, rows, :]
            top = jnp.maximum(jnp.maximum(m, m4), m16)
            w1, w4, w16 = jnp.exp2(m - top), jnp.exp2(m4 - top), jnp.exp2(m16 - top)
            gate_store(rows,
                       w1 * num + w4 * num_scr[0, rows, :] + w16 * num_scr[1, rows, :],
                       w1 * den + w4 * den_scr[0, rows, :] + w16 * den_scr[1, rows, :])

        run_branch(2, 6, stash(2))
        run_branch(1, 2, stash(1))
        run_branch(0, 2, combine)


def _attention_a(h, b1, b4, b16):
    B, S, _ = h.shape
    nhp = N_HEADS_A // PAIR
    blk = lambda off: pl.BlockSpec((None, S, LANES), lambda p, b: (b, 0, off + p))
    tab = lambda t: pl.BlockSpec((None,) + t.shape[1:], lambda p, b: (p, 0, 0, 0))
    return pl.pallas_call(
        _attn_a_kernel,
        grid=(nhp, B),
        in_specs=[blk(0), blk(nhp), blk(2 * nhp), blk(3 * nhp), tab(b1), tab(b4), tab(b16)],
        out_specs=pl.BlockSpec((None, S, LANES), lambda p, b: (b, 0, p)),
        out_shape=jax.ShapeDtypeStruct((B, S, WIDTH_A), jnp.bfloat16),
        scratch_shapes=[
            pltpu.VMEM((S, LANES), jnp.float32),
            pltpu.VMEM((S, LANES), jnp.float32),
            pltpu.VMEM((3, PAIR, S, LANES), jnp.bfloat16),
            pltpu.VMEM((2, S, LANES), jnp.bfloat16),
            pltpu.VMEM((2, S, LANES), jnp.bfloat16),
            pltpu.VMEM(b1.shape[1:], jnp.float32),
            pltpu.VMEM(b4.shape[1:], jnp.float32),
            pltpu.VMEM(b16.shape[1:], jnp.float32),
            pltpu.VMEM((2, S, LANES), jnp.float32),
            pltpu.VMEM((2, S, LANES), jnp.float32),
            pltpu.VMEM((2, S, LANES), jnp.float32),
        ],
        compiler_params=pltpu.CompilerParams(
            dimension_semantics=("arbitrary", "arbitrary"), vmem_limit_bytes=VMEM_LIMIT),
        name="attn_dilated",
    )(h, h, h, h, b1, b4, b16)


def _attn_b_kernel(q_ref, k_ref, v_ref, g_ref, bias_ref, y_ref, qm_scr):
    S = q_ref.shape[0]
    CH = 256
    lane_lt = _head0_lanes(CH)

    for row0 in range(0, S, CH):
        rows = slice(row0, row0 + CH)
        blk = q_ref[rows, :].astype(jnp.float32) * Q_SCALE
        qm_scr[0, rows, :] = jnp.where(lane_lt, blk, 0.0).astype(jnp.bfloat16)
        qm_scr[1, rows, :] = jnp.where(lane_lt, 0.0, blk).astype(jnp.bfloat16)

    lane_q = _head0_lanes(GRID_W)
    nk = NA_ROWS * GRID_W
    last = GRID_ROWS - NA_ROWS

    for r in range(GRID_ROWS):
        rows = slice(r * GRID_W, (r + 1) * GRID_W)
        k0 = min(max(r - NA_ROWS // 2, 0), last) * GRID_W
        variant = r if r < NA_ROWS // 2 else (NA_ROWS // 2 if r <= last + NA_ROWS // 2 else r - last)
        lhs = jnp.concatenate([qm_scr[0, rows, :], qm_scr[1, rows, :]], axis=0)
        num, _, den = _softmax_block(lhs, k_ref[k0:k0 + nk, :], v_ref[k0:k0 + nk, :],
                                     bias_ref[variant], lane_q)
        g = g_ref[rows, :].astype(jnp.float32)
        y_ref[rows, :] = (num * g / (den * (1.0 + jnp.exp(-g)))).astype(y_ref.dtype)


def _attention_b(h, bias_b):
    B, S, _ = h.shape
    nhp = N_HEADS_B // PAIR
    base = 4 * WIDTH_A // LANES
    blk = lambda off: pl.BlockSpec((None, S, LANES), lambda p, b: (b, 0, base + off + p))
    return pl.pallas_call(
        _attn_b_kernel,
        grid=(nhp, B),
        in_specs=[blk(0), blk(nhp), blk(2 * nhp), blk(3 * nhp),
                  pl.BlockSpec((None,) + bias_b.shape[1:], lambda p, b: (p, 0, 0, 0))],
        out_specs=pl.BlockSpec((None, S, LANES), lambda p, b: (b, 0, p)),
        out_shape=jax.ShapeDtypeStruct((B, S, WIDTH_B), jnp.bfloat16),
        scratch_shapes=[pltpu.VMEM((PAIR, S, LANES), jnp.bfloat16)],
        compiler_params=pltpu.CompilerParams(
            dimension_semantics=("arbitrary", "arbitrary"), vmem_limit_bytes=VMEM_LIMIT),
        name="attn_neighbourhood",
    )(h, h, h, h, bias_b)


def _out_kernel(ya_ref, yb_ref, wa_ref, wb_ref, x_ref, gain_ref, bias_ref, o_ref):
    half = o_ref.shape[0] // 2
    for rows in (slice(0, half), slice(half, 2 * half)):
        out = jnp.dot(ya_ref[rows, :], wa_ref[...], preferred_element_type=jnp.float32)
        out = out + jnp.dot(yb_ref[rows, :], wb_ref[...], preferred_element_type=jnp.float32)
        z = DEEPNORM_ALPHA * x_ref[rows, :] + out
        mu = jnp.mean(z, axis=-1, keepdims=True)
        zc = z - mu
        var = jnp.mean(zc * zc, axis=-1, keepdims=True)
        o_ref[rows, :] = zc * lax.rsqrt(var + LN_EPS) * gain_ref[...] + bias_ref[...]


def _output_projection(ya, yb, w, x2, gain, bias, tm=512):
    M, D = x2.shape
    assert WIDTH_A == WIDTH_B
    return pl.pallas_call(
        _out_kernel,
        grid=(M // tm,),
        in_specs=[pl.BlockSpec((tm, WIDTH_A), lambda i: (i, 0)),
                  pl.BlockSpec((tm, WIDTH_B), lambda i: (i, 0)),
                  pl.BlockSpec((WIDTH_A, D), lambda i: (0, 0)),
                  pl.BlockSpec((WIDTH_B, D), lambda i: (1, 0)),
                  pl.BlockSpec((tm, D), lambda i: (i, 0)),
                  pl.BlockSpec((1, D), lambda i: (0, 0)),
                  pl.BlockSpec((1, D), lambda i: (0, 0))],
        out_specs=pl.BlockSpec((tm, D), lambda i: (i, 0)),
        out_shape=jax.ShapeDtypeStruct((M, D), jnp.float32),
        compiler_params=pltpu.CompilerParams(
            dimension_semantics=("arbitrary",), vmem_limit_bytes=VMEM_LIMIT),
        name="out_proj_ln",
    )(ya, yb, w, w, x2, gain, bias)


def _t5_bucket_index(rel):
    half = NUM_BUCKETS // 2
    max_exact = half // 2
    n = np.abs(rel)
    large = max_exact + (np.log(np.maximum(n, 1) / max_exact)
                         / np.log(T5_MAX_DISTANCE / max_exact)
                         * (half - max_exact)).astype(np.int64)
    large = np.minimum(large, half - 1)
    return ((rel > 0).astype(np.int64) * half + np.where(n < max_exact, n, large)).astype(np.int32)


BAND_LANES = 4 * LANES
BAND_ZERO = BAND_LANES // 2


def _dilated_band_buckets():
    lane = np.arange(BAND_LANES)
    out = np.empty((len(DILATIONS), 2, SUBLANES, BAND_LANES), np.int32)
    for c, d in enumerate(DILATIONS):
        for j, start in enumerate((0, SIDE)):
            rel = lane - BAND_ZERO - start
            out[c, j] = np.where(np.abs(rel) <= SIDE, _t5_bucket_index(rel * d), -1)[None, :]
    return out


def _dilated_bias_kernel(t5_ref, band_ref, b1_ref, b4_ref, b16_ref):
    p = pl.program_id(0)
    for c, out_ref in enumerate((b1_ref, b4_ref, b16_ref)):
        nv, _, nk = out_ref.shape
        for hh in range(PAIR):
            rolled = []
            for j in range(min(nv, 2)):
                idx = band_ref[c, j]
                profile = jnp.full(idx.shape, NEG_INF, jnp.float32)
                for bkt in range(NUM_BUCKETS):
                    profile = jnp.where(idx == bkt, t5_ref[bkt, p * PAIR + hh] * LOG2E, profile)
                rows = jnp.concatenate([profile] * (QBLK // SUBLANES), axis=0)
                rolled.append(pltpu.roll(rows, 0, 1, stride=1, stride_axis=0))
            own = slice(hh * QBLK, (hh + 1) * QBLK)
            out_ref[0, own, :] = rolled[0][:, BAND_ZERO:BAND_ZERO + nk]
            if nv > 1:
                out_ref[1, own, :] = rolled[1][:, BAND_ZERO:BAND_ZERO + nk]
                out_ref[2, own, :] = rolled[0][:, BAND_ZERO - 2 * SIDE:BAND_ZERO - 2 * SIDE + nk]


def _dilated_bias(t5_table):
    band = jnp.asarray(_dilated_band_buckets())
    nhp = N_HEADS_A // PAIR
    out_shapes = []
    for d in DILATIONS:
        L = SEQ // d
        nv, nk = (1, L) if L < KBLK else (3, KBLK)
        out_shapes.append(jax.ShapeDtypeStruct((nhp, nv, PAIR * QBLK, nk), jnp.float32))
    return pl.pallas_call(
        _dilated_bias_kernel,
        grid=(nhp,),
        in_specs=[pl.BlockSpec(memory_space=pltpu.SMEM),
                  pl.BlockSpec(band.shape, lambda p: (0, 0, 0, 0))],
        out_specs=[pl.BlockSpec((None,) + o.shape[1:], lambda p: (p, 0, 0, 0)) for o in out_shapes],
        out_shape=out_shapes,
        compiler_params=pltpu.CompilerParams(dimension_semantics=("arbitrary",)),
        name="dilated_bias",
    )(t5_table, band)


RPB_ROWS = 2 * NA_ROWS - 1
RPB_COLS = 2 * NA_COLS - 1
NBR_VARIANT_ROWS = (tuple(range(NA_ROWS // 2)) + (NA_ROWS // 2,)
                    + tuple(range(GRID_ROWS - NA_ROWS // 2 + 1, GRID_ROWS)))


def _nbr_bias_kernel(rpb_ref, out_ref, m_scr):
    p = pl.program_id(0)
    shape = (GRID_W, LANES)
    qc = lax.broadcasted_iota(jnp.int32, shape, 0)
    lane = lax.broadcasted_iota(jnp.int32, shape, 1)
    kc = lane & (GRID_W - 1)
    qs = jnp.clip(qc - NA_COLS // 2, 0, GRID_W - NA_COLS)
    in_window = (kc >= qs) & (kc < qs + NA_COLS)
    first = lane < GRID_W
    u = lax.broadcasted_iota(jnp.int32, (SUBLANES, LANES), 1)
    delta = jnp.where(u < NA_COLS, u, jnp.where(u >= LANES - NA_COLS, u - LANES, u - GRID_W))
    col = jnp.where(jnp.abs(delta) < NA_COLS, delta + NA_COLS - 1, -1)
    for hh in range(PAIR):
        head = p * PAIR + hh

        for ri in range(RPB_ROWS):
            base = (head * RPB_ROWS + ri) * RPB_COLS
            profile = jnp.full(col.shape, NEG_INF, jnp.float32)
            for t in range(RPB_COLS):
                profile = jnp.where(col == t, rpb_ref[base + t] * LOG2E, profile)
            rows = jnp.concatenate([profile] * (GRID_W // SUBLANES), axis=0)
            rolled = pltpu.roll(rows, 0, 1, stride=1, stride_axis=0)
            m_scr[ri] = jnp.where(in_window, rolled, NEG_INF)
        for v, r in enumerate(NBR_VARIANT_ROWS):
            rs = min(max(r - NA_ROWS // 2, 0), GRID_ROWS - NA_ROWS)
            for j2 in range(NA_ROWS * GRID_W // LANES):
                ri = rs + 2 * j2 - r + NA_ROWS - 1
                out_ref[v, hh * GRID_W:(hh + 1) * GRID_W, j2 * LANES:(j2 + 1) * LANES] = \
                    jnp.where(first, m_scr[ri], m_scr[ri + 1])


def _neighbourhood_bias(rpb):
    nhp = N_HEADS_B // PAIR
    nv = len(NBR_VARIANT_ROWS)
    out = jax.ShapeDtypeStruct((nhp, nv, PAIR * GRID_W, NA_ROWS * GRID_W), jnp.float32)
    return pl.pallas_call(
        _nbr_bias_kernel,
        grid=(nhp,),
        in_specs=[pl.BlockSpec(memory_space=pltpu.SMEM)],
        out_specs=pl.BlockSpec((None,) + out.shape[1:], lambda p: (p, 0, 0, 0)),
        out_shape=out,
        scratch_shapes=[pltpu.VMEM((RPB_ROWS, GRID_W, LANES), jnp.float32)],
        compiler_params=pltpu.CompilerParams(dimension_semantics=("arbitrary",)),
        name="nbr_bias",
    )(rpb.reshape(-1))


def kernel(x, w_in, w_out, t5_bias, na_rpb, ln_gain, ln_bias):
    B, S, D = x.shape
    depth = w_in.shape[0]
    b1, b4, b16 = _dilated_bias(t5_bias.astype(jnp.float32))
    for layer in range(depth):
        h = _input_projection(x.reshape(B * S, D), w_in[layer])
        h = h.reshape(B, S, IN_WIDTH)
        ya = _attention_a(h, b1, b4, b16)
        yb = _attention_b(h, _neighbourhood_bias(na_rpb[layer].astype(jnp.float32)))
        wo = w_out[layer].astype(jnp.bfloat16)
        out = _output_projection(ya.reshape(B * S, WIDTH_A), yb.reshape(B * S, WIDTH_B),
                                 wo, x.reshape(B * S, D),
                                 ln_gain[layer].reshape(1, D), ln_bias[layer].reshape(1, D))
        x = out.reshape(B, S, D)
    return x
```

```python
import functools

import numpy as np
import jax
import jax.numpy as jnp
from jax import lax
from jax.experimental import pallas as pl
from jax.experimental.pallas import tpu as pltpu

D_MODEL = 2048
SEQ = 2048
HEAD_DIM = 64
N_HEADS_A = 16
N_HEADS_B = 16
WIDTH_A = N_HEADS_A * HEAD_DIM
WIDTH_B = N_HEADS_B * HEAD_DIM
IN_WIDTH = 4 * WIDTH_A + 4 * WIDTH_B
DILATIONS = (1, 4, 16)
SIDE = 64
NUM_BUCKETS = 32
T5_MAX_DISTANCE = 1024
GRID_W = 64
GRID_ROWS = SEQ // GRID_W
NA_ROWS = 8
NA_COLS = 16
DEEPNORM_ALPHA = 2.0 ** 0.25
LN_EPS = 1e-5
NEG_INF = -1e30
SHIFT_MARGIN = 1.25
SHIFT_SAMPLES = 8
SHIFT_SAMPLE_ROWS = 32
MIN_DENOMINATOR = 2.0 ** -100
MAX_DENOMINATOR = 2.0 ** 100
LOG2E = 1.4426950408889634
Q_SCALE = HEAD_DIM ** -0.5 * LOG2E

LANES = 128
SUBLANES = 8
PAIR = LANES // HEAD_DIM
QBLK = 128
KBLK = QBLK + 2 * SIDE
FAST_AHEAD = (2, 2, 4)
VMEM_LIMIT = 56 * 1024 * 1024


def _proj_kernel(x_ref, w_ref, o_ref, xb_scr):
    @pl.when(pl.program_id(1) == 0)
    def _():
        xb_scr[...] = x_ref[...].astype(jnp.bfloat16)
    o_ref[...] = jnp.dot(xb_scr[...], w_ref[...].astype(jnp.bfloat16),
                         preferred_element_type=jnp.float32).astype(o_ref.dtype)


def _input_projection(x2, wb, tm=2048, tn=512):
    M, D = x2.shape
    N = wb.shape[1]
    return pl.pallas_call(
        _proj_kernel,
        grid=(M // tm, N // tn),
        in_specs=[pl.BlockSpec((tm, D), lambda m, n: (m, 0)),
                  pl.BlockSpec((D, tn), lambda m, n: (0, n))],
        out_specs=pl.BlockSpec((tm, tn), lambda m, n: (m, n)),
        out_shape=jax.ShapeDtypeStruct((M, N), jnp.bfloat16),
        scratch_shapes=[pltpu.VMEM((tm, D), jnp.bfloat16)],
        compiler_params=pltpu.CompilerParams(
            dimension_semantics=("arbitrary", "arbitrary"), vmem_limit_bytes=VMEM_LIMIT),
        name="in_proj",
    )(x2, wb)


def _head0_lanes(rows):
    return lax.broadcasted_iota(jnp.int32, (rows, LANES), 1) < HEAD_DIM


def _softmax_block(lhs, kb, vb, bias, lane_lt):
    s = lax.dot_general(lhs, kb, (((1,), (1,)), ((), ())),
                        preferred_element_type=jnp.float32) + bias
    return _softmax_pv(s, vb, lane_lt)


def _softmax_pv(s, vb, lane_lt):
    nq = s.shape[0] // 2

    def half(sh):
        m = jnp.max(sh, axis=-1, keepdims=True)
        return jnp.exp2(sh - m).astype(jnp.bfloat16), m

    p0, m0 = half(s[:nq])
    p1, m1 = half(s[nq:])
    v_ones = jnp.concatenate([vb, jnp.ones_like(vb)], axis=1)
    pv = jnp.dot(jnp.concatenate([p0, p1], axis=0), v_ones, preferred_element_type=jnp.float32)
    return (jnp.where(lane_lt, pv[:nq, :LANES], pv[nq:, :LANES]), jnp.where(lane_lt, m0, m1),
            jnp.where(lane_lt, pv[:nq, LANES:], pv[nq:, LANES:]))


def _pair_norms2(xb):
    x = xb.astype(jnp.float32)
    row = lax.broadcasted_iota(jnp.int32, (LANES, 2 * LANES), 0)
    col = lax.broadcasted_iota(jnp.int32, (LANES, 2 * LANES), 1)
    pick = ((row < HEAD_DIM) == (col < LANES)).astype(jnp.bfloat16)
    return jnp.dot((x * x).astype(jnp.bfloat16), pick, preferred_element_type=jnp.float32)


def _head_max(t):
    while t.ndim > 2:
        t = jnp.max(t, axis=0)
    n = t.shape[0] // PAIR
    full = lambda x: jnp.max(jnp.max(x, axis=0, keepdims=True), axis=1, keepdims=True)
    return full(t[:n]), full(t[n:])


def _softmax_shift(q_ref, k_ref, tops):
    n_rows = q_ref.shape[0]

    def max_norm2(ref):
        rows = jnp.concatenate([ref[r0:r0 + SHIFT_SAMPLE_ROWS, :]
                                for r0 in range(0, n_rows, n_rows // SHIFT_SAMPLES)], axis=0)
        return jnp.max(_pair_norms2(rows), axis=0, keepdims=True)

    bound = jnp.sqrt(max_norm2(q_ref) * max_norm2(k_ref)) * (Q_SCALE * SHIFT_MARGIN)
    return [jnp.max(bound[:, hh * LANES:(hh + 1) * LANES], axis=1, keepdims=True)
            + functools.reduce(jnp.maximum, [t[hh] for t in tops]) for hh in range(PAIR)]


def _out_of_range(dens):
    lo = jnp.min(functools.reduce(jnp.minimum, dens))
    hi = jnp.max(functools.reduce(jnp.maximum, dens))
    return jnp.logical_not(jnp.logical_and(lo >= MIN_DENOMINATOR, hi <= MAX_DENOMINATOR))


def _attn_a_kernel(q_ref, k_ref, v_ref, g_ref, b1_ref, b4_ref, b16_ref, y_ref,
                   f32_scr, f4_scr, qm_scr, kp_scr, vp_scr, s1_scr, s4_scr, s16_scr,
                   num_scr, m_scr, den_scr, before_check=None):
    S = q_ref.shape[0]
    CH = 256
    bias_refs = (b1_ref, b4_ref, b16_ref)
    shifted_refs = (s1_scr, s4_scr, s16_scr)

    STEP = DILATIONS[1] // DILATIONS[0]
    assert all(b == a * STEP for a, b in zip(DILATIONS, DILATIONS[1:]))

    def regroup(emit, keep_natural):
        src, dst = f32_scr, f4_scr
        if keep_natural:
            for row0 in range(0, S, CH):
                emit(0, row0, src[row0:row0 + CH, :])
        for c in range(1, len(DILATIONS)):
            groups = DILATIONS[c - 1]
            L = S // groups
            ch = min(CH, L // STEP)
            for g in range(groups):
                for s in range(STEP):
                    for j0 in range(0, L // STEP, ch):
                        blk = src[pl.ds(g * L + s + j0 * STEP, ch, stride=STEP), :]
                        row0 = (g + groups * s) * (L // STEP) + j0
                        emit(c, row0, blk)
                        if c + 1 < len(DILATIONS):
                            dst[row0:row0 + ch, :] = blk
            src, dst = dst, src

    def emit_to(dst):
        def emit(c, row0, blk):
            dst[c - 1, row0:row0 + blk.shape[0], :] = blk.astype(jnp.bfloat16)
        return emit

    def emit_q(c, row0, blk):
        lt = _head0_lanes(blk.shape[0])
        rows = slice(row0, row0 + blk.shape[0])
        qm_scr[c, 0, rows, :] = jnp.where(lt, blk, 0.0).astype(jnp.bfloat16)
        qm_scr[c, 1, rows, :] = jnp.where(lt, 0.0, blk).astype(jnp.bfloat16)

    f32_scr[...] = q_ref[...].astype(jnp.float32) * Q_SCALE
    regroup(emit_q, True)
    f32_scr[...] = k_ref[...].astype(jnp.float32)
    regroup(emit_to(kp_scr), False)
    f32_scr[...] = v_ref[...].astype(jnp.float32)
    regroup(emit_to(vp_scr), False)

    shift = _softmax_shift(q_ref, k_ref, [_head_max(r[...]) for r in bias_refs])
    for bias_ref, shifted_ref in zip(bias_refs, shifted_refs):
        for v in range(bias_ref.shape[0]):
            for hh in range(PAIR):
                rows = slice(hh * QBLK, (hh + 1) * QBLK)
                shifted_ref[v, rows, :] = bias_ref[v, rows, :] - shift[hh]

    lane_q = _head0_lanes(QBLK)

    def block_operands(c, d, i):
        L = S // d
        nblk = L // QBLK
        nk = min(KBLK, L)
        r, bi = divmod(i, nblk)
        q0 = i * QBLK
        lhs = jnp.concatenate([qm_scr[c, 0, q0:q0 + QBLK, :],
                               qm_scr[c, 1, q0:q0 + QBLK, :]], axis=0)
        if nk < KBLK:
            k0, variant = r * L, 0
        else:
            k0 = r * L + min(max(bi * QBLK - SIDE, 0), L - KBLK)
            variant = 0 if bi == 0 else (2 if bi == nblk - 1 else 1)
        keys = slice(k0, k0 + nk)
        kb = k_ref[keys, :] if d == 1 else kp_scr[c - 1, keys, :]
        vb = v_ref[keys, :] if d == 1 else vp_scr[c - 1, keys, :]
        return lhs, kb, vb, variant

    def natural_rows(d, i):
        r, bi = divmod(i, S // d // QBLK)
        return pl.ds(r + bi * QBLK * d, QBLK, stride=d)

    def gate_store(rows, num, den):
        g = g_ref[rows, :].astype(jnp.float32)
        y_ref[rows, :] = (num * g / (den * (1.0 + jnp.exp(-g)))).astype(y_ref.dtype)

    def fast_probs(c, d, i):
        lhs, kb, vb, variant = block_operands(c, d, i)
        s = lax.dot_general(lhs, kb, (((1,), (1,)), ((), ())), preferred_element_type=jnp.float32)
        return jnp.exp2(s + shifted_refs[c][variant]).astype(jnp.bfloat16), vb

    def fast_pv(p, vb):
        pv = jnp.dot(p, jnp.concatenate([vb, jnp.ones_like(vb)], axis=1),
                     preferred_element_type=jnp.float32)
        return (jnp.where(lane_q, pv[:QBLK, :LANES], pv[QBLK:, :LANES]),
                jnp.where(lane_q, pv[:QBLK, LANES:], pv[QBLK:, LANES:]))

    nblocks = S // QBLK

    def fast_branch(c, ahead, sink):
        d = DILATIONS[c]
        pending = [fast_probs(c, d, i) for i in range(min(ahead, nblocks))]
        for i in range(nblocks):
            if i + ahead < nblocks:
                pending.append(fast_probs(c, d, i + ahead))
            sink(d, i, *fast_pv(*pending.pop(0)))

    def coarser_rows(c, i):
        L = S // DILATIONS[c]
        r, bi = divmod(i, L // QBLK)
        g, s = r % DILATIONS[c - 1], r // DILATIONS[c - 1]
        return pl.ds(g * L * STEP + s + bi * QBLK * STEP, QBLK, stride=STEP)

    den_mins = []

    def fast_sink(c):
        def sink(d, i, num, den):
            own = slice(i * QBLK, (i + 1) * QBLK)
            if c + 1 < len(DILATIONS):
                num = num + num_scr[c, own, :]
                den = den + den_scr[c, own, :]
            if c > 0:
                rows = coarser_rows(c, i)
                num_scr[c - 1, rows, :] = num
                den_scr[c - 1, rows, :] = den
            else:
                gate_store(own, num, den)
                den_mins.append(den)
        return sink

    for c in reversed(range(len(DILATIONS))):
        fast_branch(c, FAST_AHEAD[c], fast_sink(c))

    if before_check is not None:
        before_check()

    @pl.when(_out_of_range(den_mins))
    def _():
        def scores(c, d, i):
            lhs, kb, vb, variant = block_operands(c, d, i)
            s = lax.dot_general(lhs, kb, (((1,), (1,)), ((), ())),
                                preferred_element_type=jnp.float32) + bias_refs[c][variant]
            return s, vb

        def run_branch(c, ahead, sink):
            d = DILATIONS[c]
            pending = [scores(c, d, i) for i in range(min(ahead, nblocks))]
            for i in range(nblocks):
                if i + ahead < nblocks:
                    pending.append(scores(c, d, i + ahead))
                s, vb = pending.pop(0)
                sink(d, i, *_softmax_pv(s, vb, lane_q))

        def stash(c):
            def sink(d, i, num, m, den):
                rows = natural_rows(d, i)
                num_scr[c - 1, rows, :] = num
                m_scr[c - 1, rows, :] = m
                den_scr[c - 1, rows, :] = den
            return sink

        def combine(d, i, num, m, den):
            rows = slice(i * QBLK, (i + 1) * QBLK)
            m4, m16 = m_scr[0, rows, :], m_scr[1, rows, :]
            top = jnp.maximum(jnp.maximum(m, m4), m16)
            w1, w4, w16 = jnp.exp2(m - top), jnp.exp2(m4 - top), jnp.exp2(m16 - top)
            gate_store(rows,
                       w1 * num + w4 * num_scr[0, rows, :] + w16 * num_scr[1, rows, :],
                       w1 * den + w4 * den_scr[0, rows, :] + w16 * den_scr[1, rows, :])

        run_branch(2, 6, stash(2))
        run_branch(1, 2, stash(1))
        run_branch(0, 2, combine)


def _attn_b_kernel(q_ref, k_ref, v_ref, g_ref, bias_ref, y_ref, qm_scr):
    S = q_ref.shape[0]
    CH = 256
    lane_lt = _head0_lanes(CH)

    for row0 in range(0, S, CH):
        rows = slice(row0, row0 + CH)
        blk = q_ref[rows, :].astype(jnp.float32) * Q_SCALE
        qm_scr[0, rows, :] = jnp.where(lane_lt, blk, 0.0).astype(jnp.bfloat16)
        qm_scr[1, rows, :] = jnp.where(lane_lt, 0.0, blk).astype(jnp.bfloat16)

    lane_q = _head0_lanes(GRID_W)
    nk = NA_ROWS * GRID_W
    last = GRID_ROWS - NA_ROWS

    for r in range(GRID_ROWS):
        rows = slice(r * GRID_W, (r + 1) * GRID_W)
        k0 = min(max(r - NA_ROWS // 2, 0), last) * GRID_W
        variant = r if r < NA_ROWS // 2 else (NA_ROWS // 2 if r <= last + NA_ROWS // 2 else r - last)
        lhs = jnp.concatenate([qm_scr[0, rows, :], qm_scr[1, rows, :]], axis=0)
        num, _, den = _softmax_block(lhs, k_ref[k0:k0 + nk, :], v_ref[k0:k0 + nk, :],
                                     bias_ref[variant], lane_q)
        g = g_ref[rows, :].astype(jnp.float32)
        y_ref[rows, :] = (num * g / (den * (1.0 + jnp.exp(-g)))).astype(y_ref.dtype)


def _attn_kernel(qa_ref, ka_ref, va_ref, ga_ref, b1_ref, b4_ref, b16_ref,
                 qb_ref, kb_ref, vb_ref, gb_ref, bb_ref, ya_ref, yb_ref, *scratch):
    _attn_a_kernel(qa_ref, ka_ref, va_ref, ga_ref, b1_ref, b4_ref, b16_ref, ya_ref, *scratch[:-1],
                   before_check=functools.partial(_attn_b_kernel, qb_ref, kb_ref, vb_ref, gb_ref, bb_ref,
                                                  yb_ref, scratch[-1]))


def _attention(h, b1, b4, b16, bias_b):
    B, S, _ = h.shape
    assert N_HEADS_A == N_HEADS_B
    nhp = N_HEADS_A // PAIR
    blk = lambda off: pl.BlockSpec((None, S, LANES), lambda p, b: (b, 0, off + p))
    tab = lambda t: pl.BlockSpec((None,) + t.shape[1:], lambda p, b: (p, 0, 0, 0))
    out = lambda: pl.BlockSpec((None, S, LANES), lambda p, b: (b, 0, p))
    base = 4 * WIDTH_A // LANES
    return pl.pallas_call(
        _attn_kernel,
        grid=(nhp, B),
        in_specs=[blk(0), blk(nhp), blk(2 * nhp), blk(3 * nhp), tab(b1), tab(b4), tab(b16),
                  blk(base), blk(base + nhp), blk(base + 2 * nhp), blk(base + 3 * nhp), tab(bias_b)],
        out_specs=[out(), out()],
        out_shape=[jax.ShapeDtypeStruct((B, S, WIDTH_A), jnp.bfloat16),
                   jax.ShapeDtypeStruct((B, S, WIDTH_B), jnp.bfloat16)],
        scratch_shapes=[
            pltpu.VMEM((S, LANES), jnp.float32),
            pltpu.VMEM((S, LANES), jnp.float32),
            pltpu.VMEM((3, PAIR, S, LANES), jnp.bfloat16),
            pltpu.VMEM((2, S, LANES), jnp.bfloat16),
            pltpu.VMEM((2, S, LANES), jnp.bfloat16),
            pltpu.VMEM(b1.shape[1:], jnp.float32),
            pltpu.VMEM(b4.shape[1:], jnp.float32),
            pltpu.VMEM(b16.shape[1:], jnp.float32),
            pltpu.VMEM((2, S, LANES), jnp.float32),
            pltpu.VMEM((2, S, LANES), jnp.float32),
            pltpu.VMEM((2, S, LANES), jnp.float32),
            pltpu.VMEM((PAIR, S, LANES), jnp.bfloat16),
        ],
        compiler_params=pltpu.CompilerParams(
            dimension_semantics=("arbitrary", "arbitrary"), vmem_limit_bytes=VMEM_LIMIT),
        name="attention",
    )(h, h, h, h, b1, b4, b16, h, h, h, h, bias_b)


def _out_kernel(ya_ref, yb_ref, wa_ref, wb_ref, x_ref, gain_ref, bias_ref, o_ref):
    half = o_ref.shape[0] // 2
    for rows in (slice(0, half), slice(half, 2 * half)):
        out = jnp.dot(ya_ref[rows, :], wa_ref[...], preferred_element_type=jnp.float32)
        out = out + jnp.dot(yb_ref[rows, :], wb_ref[...], preferred_element_type=jnp.float32)
        z = DEEPNORM_ALPHA * x_ref[rows, :] + out
        mu = jnp.mean(z, axis=-1, keepdims=True)
        zc = z - mu
        var = jnp.mean(zc * zc, axis=-1, keepdims=True)
        o_ref[rows, :] = zc * lax.rsqrt(var + LN_EPS) * gain_ref[...] + bias_ref[...]


def _output_projection(ya, yb, w, x2, gain, bias, tm=512):
    M, D = x2.shape
    assert WIDTH_A == WIDTH_B
    return pl.pallas_call(
        _out_kernel,
        grid=(M // tm,),
        in_specs=[pl.BlockSpec((tm, WIDTH_A), lambda i: (i, 0)),
                  pl.BlockSpec((tm, WIDTH_B), lambda i: (i, 0)),
                  pl.BlockSpec((WIDTH_A, D), lambda i: (0, 0)),
                  pl.BlockSpec((WIDTH_B, D), lambda i: (1, 0)),
                  pl.BlockSpec((tm, D), lambda i: (i, 0)),
                  pl.BlockSpec((1, D), lambda i: (0, 0)),
                  pl.BlockSpec((1, D), lambda i: (0, 0))],
        out_specs=pl.BlockSpec((tm, D), lambda i: (i, 0)),
        out_shape=jax.ShapeDtypeStruct((M, D), jnp.float32),
        compiler_params=pltpu.CompilerParams(
            dimension_semantics=("arbitrary",), vmem_limit_bytes=VMEM_LIMIT),
        name="out_proj_ln",
    )(ya, yb, w, w, x2, gain, bias)


def _t5_bucket_index(rel):
    half = NUM_BUCKETS // 2
    max_exact = half // 2
    n = np.abs(rel)
    large = max_exact + (np.log(np.maximum(n, 1) / max_exact)
                         / np.log(T5_MAX_DISTANCE / max_exact)
                         * (half - max_exact)).astype(np.int64)
    large = np.minimum(large, half - 1)
    return ((rel > 0).astype(np.int64) * half + np.where(n < max_exact, n, large)).astype(np.int32)


BAND_LANES = 4 * LANES
BAND_ZERO = BAND_LANES // 2


def _dilated_band_buckets():
    lane = np.arange(BAND_LANES)
    out = np.empty((len(DILATIONS), 2, SUBLANES, BAND_LANES), np.int32)
    for c, d in enumerate(DILATIONS):
        for j, start in enumerate((0, SIDE)):
            rel = lane - BAND_ZERO - start
            out[c, j] = np.where(np.abs(rel) <= SIDE, _t5_bucket_index(rel * d), -1)[None, :]
    return out


def _dilated_bias_kernel(t5_ref, band_ref, b1_ref, b4_ref, b16_ref):
    p = pl.program_id(0)
    for c, out_ref in enumerate((b1_ref, b4_ref, b16_ref)):
        nv, _, nk = out_ref.shape
        for hh in range(PAIR):
            rolled = []
            for j in range(min(nv, 2)):
                idx = band_ref[c, j]
                profile = jnp.full(idx.shape, NEG_INF, jnp.float32)
                for bkt in range(NUM_BUCKETS):
                    profile = jnp.where(idx == bkt, t5_ref[bkt, p * PAIR + hh] * LOG2E, profile)
                rows = jnp.concatenate([profile] * (QBLK // SUBLANES), axis=0)
                rolled.append(pltpu.roll(rows, 0, 1, stride=1, stride_axis=0))
            own = slice(hh * QBLK, (hh + 1) * QBLK)
            out_ref[0, own, :] = rolled[0][:, BAND_ZERO:BAND_ZERO + nk]
            if nv > 1:
                out_ref[1, own, :] = rolled[1][:, BAND_ZERO:BAND_ZERO + nk]
                out_ref[2, own, :] = rolled[0][:, BAND_ZERO - 2 * SIDE:BAND_ZERO - 2 * SIDE + nk]


def _dilated_bias(t5_table):
    band = jnp.asarray(_dilated_band_buckets())
    nhp = N_HEADS_A // PAIR
    out_shapes = []
    for d in DILATIONS:
        L = SEQ // d
        nv, nk = (1, L) if L < KBLK else (3, KBLK)
        out_shapes.append(jax.ShapeDtypeStruct((nhp, nv, PAIR * QBLK, nk), jnp.float32))
    return pl.pallas_call(
        _dilated_bias_kernel,
        grid=(nhp,),
        in_specs=[pl.BlockSpec(memory_space=pltpu.SMEM),
                  pl.BlockSpec(band.shape, lambda p: (0, 0, 0, 0))],
        out_specs=[pl.BlockSpec((None,) + o.shape[1:], lambda p: (p, 0, 0, 0)) for o in out_shapes],
        out_shape=out_shapes,
        compiler_params=pltpu.CompilerParams(dimension_semantics=("arbitrary",)),
        name="dilated_bias",
    )(t5_table, band)


RPB_ROWS = 2 * NA_ROWS - 1
RPB_COLS = 2 * NA_COLS - 1
NBR_VARIANT_ROWS = (tuple(range(NA_ROWS // 2)) + (NA_ROWS // 2,)
                    + tuple(range(GRID_ROWS - NA_ROWS // 2 + 1, GRID_ROWS)))


def _nbr_bias_kernel(rpb_ref, out_ref, m_scr):
    p = pl.program_id(0)
    shape = (GRID_W, LANES)
    qc = lax.broadcasted_iota(jnp.int32, shape, 0)
    lane = lax.broadcasted_iota(jnp.int32, shape, 1)
    kc = lane & (GRID_W - 1)
    qs = jnp.clip(qc - NA_COLS // 2, 0, GRID_W - NA_COLS)
    in_window = (kc >= qs) & (kc < qs + NA_COLS)
    first = lane < GRID_W
    u = lax.broadcasted_iota(jnp.int32, (SUBLANES, LANES), 1)
    delta = jnp.where(u < NA_COLS, u, jnp.where(u >= LANES - NA_COLS, u - LANES, u - GRID_W))
    col = jnp.where(jnp.abs(delta) < NA_COLS, delta + NA_COLS - 1, -1)
    for hh in range(PAIR):
        head = p * PAIR + hh

        for ri in range(RPB_ROWS):
            base = (head * RPB_ROWS + ri) * RPB_COLS
            profile = jnp.full(col.shape, NEG_INF, jnp.float32)
            for t in range(RPB_COLS):
                profile = jnp.where(col == t, rpb_ref[base + t] * LOG2E, profile)
            rows = jnp.concatenate([profile] * (GRID_W // SUBLANES), axis=0)
            rolled = pltpu.roll(rows, 0, 1, stride=1, stride_axis=0)
            m_scr[ri] = jnp.where(in_window, rolled, NEG_INF)
        for v, r in enumerate(NBR_VARIANT_ROWS):
            rs = min(max(r - NA_ROWS // 2, 0), GRID_ROWS - NA_ROWS)
            for j2 in range(NA_ROWS * GRID_W // LANES):
                ri = rs + 2 * j2 - r + NA_ROWS - 1
                out_ref[v, hh * GRID_W:(hh + 1) * GRID_W, j2 * LANES:(j2 + 1) * LANES] = \
                    jnp.where(first, m_scr[ri], m_scr[ri + 1])


def _neighbourhood_bias(rpb):
    nhp = N_HEADS_B // PAIR
    nv = len(NBR_VARIANT_ROWS)
    out = jax.ShapeDtypeStruct((nhp, nv, PAIR * GRID_W, NA_ROWS * GRID_W), jnp.float32)
    return pl.pallas_call(
        _nbr_bias_kernel,
        grid=(nhp,),
        in_specs=[pl.BlockSpec(memory_space=pltpu.SMEM)],
        out_specs=pl.BlockSpec((None,) + out.shape[1:], lambda p: (p, 0, 0, 0)),
        out_shape=out,
        scratch_shapes=[pltpu.VMEM((RPB_ROWS, GRID_W, LANES), jnp.float32)],
        compiler_params=pltpu.CompilerParams(dimension_semantics=("arbitrary",)),
        name="nbr_bias",
    )(rpb.reshape(-1))


def kernel(x, w_in, w_out, t5_bias, na_rpb, ln_gain, ln_bias):
    B, S, D = x.shape
    depth = w_in.shape[0]
    b1, b4, b16 = _dilated_bias(t5_bias.astype(jnp.float32))
    for layer in range(depth):
        h = _input_projection(x.reshape(B * S, D), w_in[layer])
        h = h.reshape(B, S, IN_WIDTH)
        ya, yb = _attention(h, b1, b4, b16, _neighbourhood_bias(na_rpb[layer].astype(jnp.float32)))
        wo = w_out[layer].astype(jnp.bfloat16)
        out = _output_projection(ya.reshape(B * S, WIDTH_A), yb.reshape(B * S, WIDTH_B),
                                 wo, x.reshape(B * S, D),
                                 ln_gain[layer].reshape(1, D), ln_bias[layer].reshape(1, D))
        x = out.reshape(B, S, D)
    return x
```

```python
import functools

import numpy as np
import jax
import jax.numpy as jnp
from jax import lax
from jax.experimental import pallas as pl
from jax.experimental.pallas import tpu as pltpu

D_MODEL = 2048
SEQ = 2048
HEAD_DIM = 64
N_HEADS_A = 16
N_HEADS_B = 16
WIDTH_A = N_HEADS_A * HEAD_DIM
WIDTH_B = N_HEADS_B * HEAD_DIM
IN_WIDTH = 4 * WIDTH_A + 4 * WIDTH_B
DILATIONS = (1, 4, 16)
SIDE = 64
NUM_BUCKETS = 32
T5_MAX_DISTANCE = 1024
GRID_W = 64
GRID_ROWS = SEQ // GRID_W
NA_ROWS = 8
NA_COLS = 16
DEEPNORM_ALPHA = 2.0 ** 0.25
LN_EPS = 1e-5
NEG_INF = -1e30
SHIFT_MARGIN = 1.25
SHIFT_SAMPLES = 8
SHIFT_SAMPLE_ROWS = 32
MIN_DENOMINATOR = 2.0 ** -100
MAX_DENOMINATOR = 2.0 ** 100
LOG2E = 1.4426950408889634
Q_SCALE = HEAD_DIM ** -0.5 * LOG2E

LANES = 128
SUBLANES = 8
PAIR = LANES // HEAD_DIM
QBLK = 128
KBLK = QBLK + 2 * SIDE
FAST_AHEAD = (2, 2, 4)
VMEM_LIMIT = 56 * 1024 * 1024


def _proj_kernel(x_ref, w_ref, o_ref, xb_scr):
    @pl.when(pl.program_id(1) == 0)
    def _():
        xb_scr[...] = x_ref[...].astype(jnp.bfloat16)
    o_ref[...] = jnp.dot(xb_scr[...], w_ref[...].astype(jnp.bfloat16),
                         preferred_element_type=jnp.float32).astype(o_ref.dtype)


def _input_projection(x2, wb, tm=2048, tn=512):
    M, D = x2.shape
    N = wb.shape[1]
    return pl.pallas_call(
        _proj_kernel,
        grid=(M // tm, N // tn),
        in_specs=[pl.BlockSpec((tm, D), lambda m, n: (m, 0)),
                  pl.BlockSpec((D, tn), lambda m, n: (0, n))],
        out_specs=pl.BlockSpec((tm, tn), lambda m, n: (m, n)),
        out_shape=jax.ShapeDtypeStruct((M, N), jnp.bfloat16),
        scratch_shapes=[pltpu.VMEM((tm, D), jnp.bfloat16)],
        compiler_params=pltpu.CompilerParams(
            dimension_semantics=("arbitrary", "arbitrary"), vmem_limit_bytes=VMEM_LIMIT),
        name="in_proj",
    )(x2, wb)


def _head0_lanes(rows):
    return lax.broadcasted_iota(jnp.int32, (rows, LANES), 1) < HEAD_DIM


def _softmax_block(lhs, kb, vb, bias, lane_lt):
    s = lax.dot_general(lhs, kb, (((1,), (1,)), ((), ())),
                        preferred_element_type=jnp.float32) + bias
    return _softmax_pv(s, vb, lane_lt)


def _softmax_pv(s, vb, lane_lt):
    nq = s.shape[0] // 2

    def half(sh):
        m = jnp.max(sh, axis=-1, keepdims=True)
        return jnp.exp2(sh - m).astype(jnp.bfloat16), m

    p0, m0 = half(s[:nq])
    p1, m1 = half(s[nq:])
    v_ones = jnp.concatenate([vb, jnp.ones_like(vb)], axis=1)
    pv = jnp.dot(jnp.concatenate([p0, p1], axis=0), v_ones, preferred_element_type=jnp.float32)
    return (jnp.where(lane_lt, pv[:nq, :LANES], pv[nq:, :LANES]), jnp.where(lane_lt, m0, m1),
            jnp.where(lane_lt, pv[:nq, LANES:], pv[nq:, LANES:]))


def _pair_norms2(xb):
    x = xb.astype(jnp.float32)
    row = lax.broadcasted_iota(jnp.int32, (LANES, 2 * LANES), 0)
    col = lax.broadcasted_iota(jnp.int32, (LANES, 2 * LANES), 1)
    pick = ((row < HEAD_DIM) == (col < LANES)).astype(jnp.bfloat16)
    return jnp.dot((x * x).astype(jnp.bfloat16), pick, preferred_element_type=jnp.float32)


def _head_max(t):
    while t.ndim > 2:
        t = jnp.max(t, axis=0)
    n = t.shape[0] // PAIR
    full = lambda x: jnp.max(jnp.max(x, axis=0, keepdims=True), axis=1, keepdims=True)
    return full(t[:n]), full(t[n:])


def _softmax_shift(q_ref, k_ref, tops):
    n_rows = q_ref.shape[0]

    def max_norm2(ref):
        rows = jnp.concatenate([ref[r0:r0 + SHIFT_SAMPLE_ROWS, :]
                                for r0 in range(0, n_rows, n_rows // SHIFT_SAMPLES)], axis=0)
        return jnp.max(_pair_norms2(rows), axis=0, keepdims=True)

    bound = jnp.sqrt(max_norm2(q_ref) * max_norm2(k_ref)) * (Q_SCALE * SHIFT_MARGIN)
    return [jnp.max(bound[:, hh * LANES:(hh + 1) * LANES], axis=1, keepdims=True)
            + functools.reduce(jnp.maximum, [t[hh] for t in tops]) for hh in range(PAIR)]


def _out_of_range(dens):
    lo = jnp.min(functools.reduce(jnp.minimum, dens))
    hi = jnp.max(functools.reduce(jnp.maximum, dens))
    return jnp.logical_not(jnp.logical_and(lo >= MIN_DENOMINATOR, hi <= MAX_DENOMINATOR))


def _attn_a_kernel(q_ref, k_ref, v_ref, g_ref, b1_ref, b4_ref, b16_ref, y_ref,
                   f32_scr, f4_scr, qm_scr, kp_scr, vp_scr, s1_scr, s4_scr, s16_scr,
                   num_scr, m_scr, den_scr, before_check=None):
    S = q_ref.shape[0]
    CH = 256
    bias_refs = (b1_ref, b4_ref, b16_ref)
    shifted_refs = (s1_scr, s4_scr, s16_scr)

    STEP = DILATIONS[1] // DILATIONS[0]
    assert all(b == a * STEP for a, b in zip(DILATIONS, DILATIONS[1:]))

    def regroup(emit, keep_natural):
        src, dst = f32_scr, f4_scr
        if keep_natural:
            for row0 in range(0, S, CH):
                emit(0, row0, src[row0:row0 + CH, :])
        for c in range(1, len(DILATIONS)):
            groups = DILATIONS[c - 1]
            L = S // groups
            ch = min(CH, L // STEP)
            for g in range(groups):
                for s in range(STEP):
                    for j0 in range(0, L // STEP, ch):
                        blk = src[pl.ds(g * L + s + j0 * STEP, ch, stride=STEP), :]
                        row0 = (g + groups * s) * (L // STEP) + j0
                        emit(c, row0, blk)
                        if c + 1 < len(DILATIONS):
                            dst[row0:row0 + ch, :] = blk
            src, dst = dst, src

    def emit_to(dst):
        def emit(c, row0, blk):
            dst[c - 1, row0:row0 + blk.shape[0], :] = blk.astype(jnp.bfloat16)
        return emit

    def emit_q(c, row0, blk):
        lt = _head0_lanes(blk.shape[0])
        rows = slice(row0, row0 + blk.shape[0])
        qm_scr[c, 0, rows, :] = jnp.where(lt, blk, 0.0).astype(jnp.bfloat16)
        qm_scr[c, 1, rows, :] = jnp.where(lt, 0.0, blk).astype(jnp.bfloat16)

    f32_scr[...] = q_ref[...].astype(jnp.float32) * Q_SCALE
    regroup(emit_q, True)
    f32_scr[...] = k_ref[...].astype(jnp.float32)
    regroup(emit_to(kp_scr), False)
    f32_scr[...] = v_ref[...].astype(jnp.float32)
    regroup(emit_to(vp_scr), False)

    shift = _softmax_shift(q_ref, k_ref, [_head_max(r[...]) for r in bias_refs])
    for bias_ref, shifted_ref in zip(bias_refs, shifted_refs):
        for v in range(bias_ref.shape[0]):
            for hh in range(PAIR):
                rows = slice(hh * QBLK, (hh + 1) * QBLK)
                shifted_ref[v, rows, :] = bias_ref[v, rows, :] - shift[hh]

    lane_q = _head0_lanes(QBLK)

    def block_operands(c, d, i):
        L = S // d
        nblk = L // QBLK
        nk = min(KBLK, L)
        r, bi = divmod(i, nblk)
        q0 = i * QBLK
        lhs = jnp.concatenate([qm_scr[c, 0, q0:q0 + QBLK, :],
                               qm_scr[c, 1, q0:q0 + QBLK, :]], axis=0)
        if nk < KBLK:
            k0, variant = r * L, 0
        else:
            k0 = r * L + min(max(bi * QBLK - SIDE, 0), L - KBLK)
            variant = 0 if bi == 0 else (2 if bi == nblk - 1 else 1)
        keys = slice(k0, k0 + nk)
        kb = k_ref[keys, :] if d == 1 else kp_scr[c - 1, keys, :]
        vb = v_ref[keys, :] if d == 1 else vp_scr[c - 1, keys, :]
        return lhs, kb, vb, variant

    def natural_rows(d, i):
        r, bi = divmod(i, S // d // QBLK)
        return pl.ds(r + bi * QBLK * d, QBLK, stride=d)

    def gate_store(rows, num, den):
        g = g_ref[rows, :].astype(jnp.float32)
        y_ref[rows, :] = (num * g / (den * (1.0 + jnp.exp(-g)))).astype(y_ref.dtype)

    def fast_probs(c, d, i):
        lhs, kb, vb, variant = block_operands(c, d, i)
        s = lax.dot_general(lhs, kb, (((1,), (1,)), ((), ())), preferred_element_type=jnp.float32)
        return jnp.exp2(s + shifted_refs[c][variant]).astype(jnp.bfloat16), vb

    def fast_pv(p, vb):
        pv = jnp.dot(p, jnp.concatenate([vb, jnp.ones_like(vb)], axis=1),
                     preferred_element_type=jnp.float32)
        return (jnp.where(lane_q, pv[:QBLK, :LANES], pv[QBLK:, :LANES]),
                jnp.where(lane_q, pv[:QBLK, LANES:], pv[QBLK:, LANES:]))

    nblocks = S // QBLK

    def fast_branch(c, ahead, sink):
        d = DILATIONS[c]
        pending = [fast_probs(c, d, i) for i in range(min(ahead, nblocks))]
        for i in range(nblocks):
            if i + ahead < nblocks:
                pending.append(fast_probs(c, d, i + ahead))
            sink(d, i, *fast_pv(*pending.pop(0)))

    def coarser_rows(c, i):
        L = S // DILATIONS[c]
        r, bi = divmod(i, L // QBLK)
        g, s = r % DILATIONS[c - 1], r // DILATIONS[c - 1]
        return pl.ds(g * L * STEP + s + bi * QBLK * STEP, QBLK, stride=STEP)

    den_mins = []

    def fast_sink(c):
        def sink(d, i, num, den):
            own = slice(i * QBLK, (i + 1) * QBLK)
            if c + 1 < len(DILATIONS):
                num = num + num_scr[c, own, :]
                den = den + den_scr[c, own, :]
            if c > 0:
                rows = coarser_rows(c, i)
                num_scr[c - 1, rows, :] = num
                den_scr[c - 1, rows, :] = den
            else:
                gate_store(own, num, den)
                den_mins.append(den)
        return sink

    for c in reversed(range(len(DILATIONS))):
        fast_branch(c, FAST_AHEAD[c], fast_sink(c))

    if before_check is not None:
        before_check()

    @pl.when(_out_of_range(den_mins))
    def _():
        def scores(c, d, i):
            lhs, kb, vb, variant = block_operands(c, d, i)
            s = lax.dot_general(lhs, kb, (((1,), (1,)), ((), ())),
                                preferred_element_type=jnp.float32) + bias_refs[c][variant]
            return s, vb

        def run_branch(c, ahead, sink):
            d = DILATIONS[c]
            pending = [scores(c, d, i) for i in range(min(ahead, nblocks))]
            for i in range(nblocks):
                if i + ahead < nblocks:
                    pending.append(scores(c, d, i + ahead))
                s, vb = pending.pop(0)
                sink(d, i, *_softmax_pv(s, vb, lane_q))

        def stash(c):
            def sink(d, i, num, m, den):
                rows = natural_rows(d, i)
                num_scr[c - 1, rows, :] = num
                m_scr[c - 1, rows, :] = m
                den_scr[c - 1, rows, :] = den
            return sink

        def combine(d, i, num, m, den):
            rows = slice(i * QBLK, (i + 1) * QBLK)
            m4, m16 = m_scr[0, rows, :], m_scr[1, rows, :]
            top = jnp.maximum(jnp.maximum(m, m4), m16)
            w1, w4, w16 = jnp.exp2(m - top), jnp.exp2(m4 - top), jnp.exp2(m16 - top)
            gate_store(rows,
                       w1 * num + w4 * num_scr[0, rows, :] + w16 * num_scr[1, rows, :],
                       w1 * den + w4 * den_scr[0, rows, :] + w16 * den_scr[1, rows, :])

        run_branch(2, 6, stash(2))
        run_branch(1, 2, stash(1))
        run_branch(0, 2, combine)


def _attn_b_kernel(q_ref, k_ref, v_ref, g_ref, bias_ref, y_ref, qm_scr):
    S = q_ref.shape[0]
    CH = 256
    lane_lt = _head0_lanes(CH)

    for row0 in range(0, S, CH):
        rows = slice(row0, row0 + CH)
        blk = q_ref[rows, :].astype(jnp.float32) * Q_SCALE
        qm_scr[0, rows, :] = jnp.where(lane_lt, blk, 0.0).astype(jnp.bfloat16)
        qm_scr[1, rows, :] = jnp.where(lane_lt, 0.0, blk).astype(jnp.bfloat16)

    lane_q = _head0_lanes(GRID_W)
    nk = NA_ROWS * GRID_W
    last = GRID_ROWS - NA_ROWS

    for r in range(GRID_ROWS):
        rows = slice(r * GRID_W, (r + 1) * GRID_W)
        k0 = min(max(r - NA_ROWS // 2, 0), last) * GRID_W
        variant = r if r < NA_ROWS // 2 else (NA_ROWS // 2 if r <= last + NA_ROWS // 2 else r - last)
        lhs = jnp.concatenate([qm_scr[0, rows, :], qm_scr[1, rows, :]], axis=0)
        num, _, den = _softmax_block(lhs, k_ref[k0:k0 + nk, :], v_ref[k0:k0 + nk, :],
                                     bias_ref[variant], lane_q)
        g = g_ref[rows, :].astype(jnp.float32)
        y_ref[rows, :] = (num * g / (den * (1.0 + jnp.exp(-g)))).astype(y_ref.dtype)


def _attn_kernel(qa_ref, ka_ref, va_ref, ga_ref, b1_ref, b4_ref, b16_ref,
                 qb_ref, kb_ref, vb_ref, gb_ref, bb_ref, ya_ref, yb_ref, *scratch):
    _attn_a_kernel(qa_ref, ka_ref, va_ref, ga_ref, b1_ref, b4_ref, b16_ref, ya_ref, *scratch[:-1],
                   before_check=functools.partial(_attn_b_kernel, qb_ref, kb_ref, vb_ref, gb_ref, bb_ref,
                                                  yb_ref, scratch[-1]))


def _attention(h, b1, b4, b16, bias_b):
    B, S, _ = h.shape
    assert N_HEADS_A == N_HEADS_B
    nhp = N_HEADS_A // PAIR
    blk = lambda off: pl.BlockSpec((None, S, LANES), lambda p, b: (b, 0, off + p))
    tab = lambda t: pl.BlockSpec((None,) + t.shape[1:], lambda p, b: (p, 0, 0, 0))
    out = lambda: pl.BlockSpec((None, S, LANES), lambda p, b: (b, 0, p))
    base = 4 * WIDTH_A // LANES
    return pl.pallas_call(
        _attn_kernel,
        grid=(nhp, B),
        in_specs=[blk(0), blk(nhp), blk(2 * nhp), blk(3 * nhp), tab(b1), tab(b4), tab(b16),
                  blk(base), blk(base + nhp), blk(base + 2 * nhp), blk(base + 3 * nhp), tab(bias_b)],
        out_specs=[out(), out()],
        out_shape=[jax.ShapeDtypeStruct((B, S, WIDTH_A), jnp.bfloat16),
                   jax.ShapeDtypeStruct((B, S, WIDTH_B), jnp.bfloat16)],
        scratch_shapes=[
            pltpu.VMEM((S, LANES), jnp.float32),
            pltpu.VMEM((S, LANES), jnp.float32),
            pltpu.VMEM((3, PAIR, S, LANES), jnp.bfloat16),
            pltpu.VMEM((2, S, LANES), jnp.bfloat16),
            pltpu.VMEM((2, S, LANES), jnp.bfloat16),
            pltpu.VMEM(b1.shape[1:], jnp.float32),
            pltpu.VMEM(b4.shape[1:], jnp.float32),
            pltpu.VMEM(b16.shape[1:], jnp.float32),
            pltpu.VMEM((2, S, LANES), jnp.float32),
            pltpu.VMEM((2, S, LANES), jnp.float32),
            pltpu.VMEM((2, S, LANES), jnp.float32),
            pltpu.VMEM((PAIR, S, LANES), jnp.bfloat16),
        ],
        compiler_params=pltpu.CompilerParams(
            dimension_semantics=("arbitrary", "arbitrary"), vmem_limit_bytes=VMEM_LIMIT),
        name="attention",
    )(h, h, h, h, b1, b4, b16, h, h, h, h, bias_b)


def _out_kernel(ya_ref, yb_ref, wa_ref, wb_ref, x_ref, gain_ref, bias_ref, o_ref, w_scr):
    @pl.when(pl.program_id(0) == 0)
    def _():
        w_scr[0] = wa_ref[...].astype(jnp.bfloat16)
        w_scr[1] = wb_ref[...].astype(jnp.bfloat16)

    half = o_ref.shape[0] // 2
    for rows in (slice(0, half), slice(half, 2 * half)):
        out = jnp.dot(ya_ref[rows, :], w_scr[0], preferred_element_type=jnp.float32)
        out = out + jnp.dot(yb_ref[rows, :], w_scr[1], preferred_element_type=jnp.float32)
        z = DEEPNORM_ALPHA * x_ref[rows, :] + out
        mu = jnp.mean(z, axis=-1, keepdims=True)
        zc = z - mu
        var = jnp.mean(zc * zc, axis=-1, keepdims=True)
        o_ref[rows, :] = zc * lax.rsqrt(var + LN_EPS) * gain_ref[...] + bias_ref[...]


def _output_projection(ya, yb, w, x2, gain, bias, tm=512):
    M, D = x2.shape
    assert WIDTH_A == WIDTH_B
    return pl.pallas_call(
        _out_kernel,
        grid=(M // tm,),
        in_specs=[pl.BlockSpec((tm, WIDTH_A), lambda i: (i, 0)),
                  pl.BlockSpec((tm, WIDTH_B), lambda i: (i, 0)),
                  pl.BlockSpec((WIDTH_A, D), lambda i: (0, 0), pipeline_mode=pl.Buffered(1)),
                  pl.BlockSpec((WIDTH_B, D), lambda i: (1, 0), pipeline_mode=pl.Buffered(1)),
                  pl.BlockSpec((tm, D), lambda i: (i, 0)),
                  pl.BlockSpec((1, D), lambda i: (0, 0)),
                  pl.BlockSpec((1, D), lambda i: (0, 0))],
        out_specs=pl.BlockSpec((tm, D), lambda i: (i, 0)),
        out_shape=jax.ShapeDtypeStruct((M, D), jnp.float32),
        scratch_shapes=[pltpu.VMEM((2, WIDTH_A, D), jnp.bfloat16)],
        compiler_params=pltpu.CompilerParams(
            dimension_semantics=("arbitrary",), vmem_limit_bytes=VMEM_LIMIT),
        name="out_proj_ln",
    )(ya, yb, w, w, x2, gain, bias)


def _t5_bucket_index(rel):
    half = NUM_BUCKETS // 2
    max_exact = half // 2
    n = np.abs(rel)
    large = max_exact + (np.log(np.maximum(n, 1) / max_exact)
                         / np.log(T5_MAX_DISTANCE / max_exact)
                         * (half - max_exact)).astype(np.int64)
    large = np.minimum(large, half - 1)
    return ((rel > 0).astype(np.int64) * half + np.where(n < max_exact, n, large)).astype(np.int32)


BAND_LANES = 4 * LANES
BAND_ZERO = BAND_LANES // 2


def _dilated_band_buckets():
    lane = np.arange(BAND_LANES)
    out = np.empty((len(DILATIONS), 2, SUBLANES, BAND_LANES), np.int32)
    for c, d in enumerate(DILATIONS):
        for j, start in enumerate((0, SIDE)):
            rel = lane - BAND_ZERO - start
            out[c, j] = np.where(np.abs(rel) <= SIDE, _t5_bucket_index(rel * d), -1)[None, :]
    return out


def _dilated_bias_kernel(t5_ref, band_ref, b1_ref, b4_ref, b16_ref):
    p = pl.program_id(0)
    for c, out_ref in enumerate((b1_ref, b4_ref, b16_ref)):
        nv, _, nk = out_ref.shape
        for hh in range(PAIR):
            rolled = []
            for j in range(min(nv, 2)):
                idx = band_ref[c, j]
                profile = jnp.full(idx.shape, NEG_INF, jnp.float32)
                for bkt in range(NUM_BUCKETS):
                    profile = jnp.where(idx == bkt, t5_ref[bkt, p * PAIR + hh] * LOG2E, profile)
                rows = jnp.concatenate([profile] * (QBLK // SUBLANES), axis=0)
                rolled.append(pltpu.roll(rows, 0, 1, stride=1, stride_axis=0))
            own = slice(hh * QBLK, (hh + 1) * QBLK)
            out_ref[0, own, :] = rolled[0][:, BAND_ZERO:BAND_ZERO + nk]
            if nv > 1:
                out_ref[1, own, :] = rolled[1][:, BAND_ZERO:BAND_ZERO + nk]
                out_ref[2, own, :] = rolled[0][:, BAND_ZERO - 2 * SIDE:BAND_ZERO - 2 * SIDE + nk]


def _dilated_bias(t5_table):
    band = jnp.asarray(_dilated_band_buckets())
    nhp = N_HEADS_A // PAIR
    out_shapes = []
    for d in DILATIONS:
        L = SEQ // d
        nv, nk = (1, L) if L < KBLK else (3, KBLK)
        out_shapes.append(jax.ShapeDtypeStruct((nhp, nv, PAIR * QBLK, nk), jnp.float32))
    return pl.pallas_call(
        _dilated_bias_kernel,
        grid=(nhp,),
        in_specs=[pl.BlockSpec(memory_space=pltpu.SMEM),
                  pl.BlockSpec(band.shape, lambda p: (0, 0, 0, 0))],
        out_specs=[pl.BlockSpec((None,) + o.shape[1:], lambda p: (p, 0, 0, 0)) for o in out_shapes],
        out_shape=out_shapes,
        compiler_params=pltpu.CompilerParams(dimension_semantics=("arbitrary",)),
        name="dilated_bias",
    )(t5_table, band)


RPB_ROWS = 2 * NA_ROWS - 1
RPB_COLS = 2 * NA_COLS - 1
NBR_VARIANT_ROWS = (tuple(range(NA_ROWS // 2)) + (NA_ROWS // 2,)
                    + tuple(range(GRID_ROWS - NA_ROWS // 2 + 1, GRID_ROWS)))


def _nbr_bias_kernel(rpb_ref, out_ref, m_scr):
    p = pl.program_id(0)
    shape = (GRID_W, LANES)
    qc = lax.broadcasted_iota(jnp.int32, shape, 0)
    lane = lax.broadcasted_iota(jnp.int32, shape, 1)
    kc = lane & (GRID_W - 1)
    qs = jnp.clip(qc - NA_COLS // 2, 0, GRID_W - NA_COLS)
    in_window = (kc >= qs) & (kc < qs + NA_COLS)
    first = lane < GRID_W
    u = lax.broadcasted_iota(jnp.int32, (SUBLANES, LANES), 1)
    delta = jnp.where(u < NA_COLS, u, jnp.where(u >= LANES - NA_COLS, u - LANES, u - GRID_W))
    col = jnp.where(jnp.abs(delta) < NA_COLS, delta + NA_COLS - 1, -1)
    for hh in range(PAIR):
        head = p * PAIR + hh

        for ri in range(RPB_ROWS):
            base = (head * RPB_ROWS + ri) * RPB_COLS
            profile = jnp.full(col.shape, NEG_INF, jnp.float32)
            for t in range(RPB_COLS):
                profile = jnp.where(col == t, rpb_ref[base + t] * LOG2E, profile)
            rows = jnp.concatenate([profile] * (GRID_W // SUBLANES), axis=0)
            rolled = pltpu.roll(rows, 0, 1, stride=1, stride_axis=0)
            m_scr[ri] = jnp.where(in_window, rolled, NEG_INF)
        for v, r in enumerate(NBR_VARIANT_ROWS):
            rs = min(max(r - NA_ROWS // 2, 0), GRID_ROWS - NA_ROWS)
            for j2 in range(NA_ROWS * GRID_W // LANES):
                ri = rs + 2 * j2 - r + NA_ROWS - 1
                out_ref[v, hh * GRID_W:(hh + 1) * GRID_W, j2 * LANES:(j2 + 1) * LANES] = \
                    jnp.where(first, m_scr[ri], m_scr[ri + 1])


def _neighbourhood_bias(rpb):
    nhp = N_HEADS_B // PAIR
    nv = len(NBR_VARIANT_ROWS)
    out = jax.ShapeDtypeStruct((nhp, nv, PAIR * GRID_W, NA_ROWS * GRID_W), jnp.float32)
    return pl.pallas_call(
        _nbr_bias_kernel,
        grid=(nhp,),
        in_specs=[pl.BlockSpec(memory_space=pltpu.SMEM)],
        out_specs=pl.BlockSpec((None,) + out.shape[1:], lambda p: (p, 0, 0, 0)),
        out_shape=out,
        scratch_shapes=[pltpu.VMEM((RPB_ROWS, GRID_W, LANES), jnp.float32)],
        compiler_params=pltpu.CompilerParams(dimension_semantics=("arbitrary",)),
        name="nbr_bias",
    )(rpb.reshape(-1))


def kernel(x, w_in, w_out, t5_bias, na_rpb, ln_gain, ln_bias):
    B, S, D = x.shape
    depth = w_in.shape[0]
    b1, b4, b16 = _dilated_bias(t5_bias.astype(jnp.float32))
    for layer in range(depth):
        h = _input_projection(x.reshape(B * S, D), w_in[layer])
        h = h.reshape(B, S, IN_WIDTH)
        ya, yb = _attention(h, b1, b4, b16, _neighbourhood_bias(na_rpb[layer].astype(jnp.float32)))
        out = _output_projection(ya.reshape(B * S, WIDTH_A), yb.reshape(B * S, WIDTH_B),
                                 w_out[layer], x.reshape(B * S, D),
                                 ln_gain[layer].reshape(1, D), ln_bias[layer].reshape(1, D))
        x = out.reshape(B, S, D)
    return x
```

```python
import functools

import numpy as np
import jax
import jax.numpy as jnp
from jax import lax
from jax.experimental import pallas as pl
from jax.experimental.pallas import tpu as pltpu

D_MODEL = 2048
SEQ = 2048
HEAD_DIM = 64
N_HEADS_A = 16
N_HEADS_B = 16
WIDTH_A = N_HEADS_A * HEAD_DIM
WIDTH_B = N_HEADS_B * HEAD_DIM
IN_WIDTH = 4 * WIDTH_A + 4 * WIDTH_B
DILATIONS = (1, 4, 16)
SIDE = 64
NUM_BUCKETS = 32
T5_MAX_DISTANCE = 1024
GRID_W = 64
GRID_ROWS = SEQ // GRID_W
NA_ROWS = 8
NA_COLS = 16
DEEPNORM_ALPHA = 2.0 ** 0.25
LN_EPS = 1e-5
NEG_INF = -1e30
SHIFT_MARGIN = 1.25
SHIFT_SAMPLES = 8
SHIFT_SAMPLE_ROWS = 32
MIN_DENOMINATOR = 2.0 ** -100
MAX_DENOMINATOR = 2.0 ** 100
LOG2E = 1.4426950408889634
Q_SCALE = HEAD_DIM ** -0.5 * LOG2E

LANES = 128
SUBLANES = 8
PAIR = LANES // HEAD_DIM
QBLK = 128
KBLK = QBLK + 2 * SIDE
FAST_AHEAD = (2, 2, 4)
VMEM_LIMIT = 56 * 1024 * 1024


def _proj_kernel(x_hbm, w_ref, o_ref, xb_scr, stage_scr, sem):
    m, n = pl.program_id(0), pl.program_id(1)
    n_tiles, n_steps = pl.num_programs(0), pl.num_programs(1)
    tm = xb_scr.shape[1]
    rows = stage_scr.shape[0]

    def chunk_copy(tile, chunk):
        return pltpu.make_async_copy(x_hbm.at[pl.ds(tile * tm + chunk * rows, rows), :], stage_scr, sem)

    @pl.when(jnp.logical_and(m == 0, n == 0))
    def _():
        for c in range(tm // rows):
            copy = chunk_copy(0, c)
            copy.start()
            copy.wait()
            xb_scr[0, c * rows:(c + 1) * rows, :] = stage_scr[...].astype(jnp.bfloat16)

    copy = chunk_copy(jnp.minimum(m + 1, n_tiles - 1), n)
    copy.start()
    o_ref[...] = jnp.dot(xb_scr[m % 2], w_ref[...].astype(jnp.bfloat16),
                         preferred_element_type=jnp.float32).astype(o_ref.dtype)
    copy.wait()
    xb_scr[(m + 1) % 2, pl.ds(pl.multiple_of(n * rows, rows), rows), :] = stage_scr[...].astype(jnp.bfloat16)


def _input_projection(x2, wb, tm=2048, tn=1024):
    M, D = x2.shape
    N = wb.shape[1]
    steps = N // tn
    return pl.pallas_call(
        _proj_kernel,
        grid=(M // tm, steps),
        in_specs=[pl.BlockSpec(memory_space=pl.ANY),
                  pl.BlockSpec((D, tn), lambda m, n: (0, n))],
        out_specs=pl.BlockSpec((tm, tn), lambda m, n: (m, n)),
        out_shape=jax.ShapeDtypeStruct((M, N), jnp.bfloat16),
        scratch_shapes=[pltpu.VMEM((2, tm, D), jnp.bfloat16),
                        pltpu.VMEM((tm // steps, D), jnp.float32),
                        pltpu.SemaphoreType.DMA(())],
        compiler_params=pltpu.CompilerParams(
            dimension_semantics=("arbitrary", "arbitrary"), vmem_limit_bytes=VMEM_LIMIT),
        name="in_proj",
    )(x2, wb)


def _head0_lanes(rows):
    return lax.broadcasted_iota(jnp.int32, (rows, LANES), 1) < HEAD_DIM


def _softmax_block(lhs, kb, vb, bias, lane_lt):
    s = lax.dot_general(lhs, kb, (((1,), (1,)), ((), ())),
                        preferred_element_type=jnp.float32) + bias
    return _softmax_pv(s, vb, lane_lt)


def _softmax_pv(s, vb, lane_lt):
    nq = s.shape[0] // 2

    def half(sh):
        m = jnp.max(sh, axis=-1, keepdims=True)
        return jnp.exp2(sh - m).astype(jnp.bfloat16), m

    p0, m0 = half(s[:nq])
    p1, m1 = half(s[nq:])
    v_ones = jnp.concatenate([vb, jnp.ones_like(vb)], axis=1)
    pv = jnp.dot(jnp.concatenate([p0, p1], axis=0), v_ones, preferred_element_type=jnp.float32)
    return (jnp.where(lane_lt, pv[:nq, :LANES], pv[nq:, :LANES]), jnp.where(lane_lt, m0, m1),
            jnp.where(lane_lt, pv[:nq, LANES:], pv[nq:, LANES:]))


def _pair_norms2(xb):
    x = xb.astype(jnp.float32)
    row = lax.broadcasted_iota(jnp.int32, (LANES, 2 * LANES), 0)
    col = lax.broadcasted_iota(jnp.int32, (LANES, 2 * LANES), 1)
    pick = ((row < HEAD_DIM) == (col < LANES)).astype(jnp.bfloat16)
    return jnp.dot((x * x).astype(jnp.bfloat16), pick, preferred_element_type=jnp.float32)


def _head_max(t):
    while t.ndim > 2:
        t = jnp.max(t, axis=0)
    n = t.shape[0] // PAIR
    full = lambda x: jnp.max(jnp.max(x, axis=0, keepdims=True), axis=1, keepdims=True)
    return full(t[:n]), full(t[n:])


def _softmax_shift(q_ref, k_ref, tops):
    n_rows = q_ref.shape[0]

    def max_norm2(ref):
        rows = jnp.concatenate([ref[r0:r0 + SHIFT_SAMPLE_ROWS, :]
                                for r0 in range(0, n_rows, n_rows // SHIFT_SAMPLES)], axis=0)
        return jnp.max(_pair_norms2(rows), axis=0, keepdims=True)

    bound = jnp.sqrt(max_norm2(q_ref) * max_norm2(k_ref)) * (Q_SCALE * SHIFT_MARGIN)
    return [jnp.max(bound[:, hh * LANES:(hh + 1) * LANES], axis=1, keepdims=True)
            + functools.reduce(jnp.maximum, [t[hh] for t in tops]) for hh in range(PAIR)]


def _out_of_range(dens):
    lo = jnp.min(functools.reduce(jnp.minimum, dens))
    hi = jnp.max(functools.reduce(jnp.maximum, dens))
    return jnp.logical_not(jnp.logical_and(lo >= MIN_DENOMINATOR, hi <= MAX_DENOMINATOR))


def _attn_a_kernel(q_ref, k_ref, v_ref, g_ref, b1_ref, b4_ref, b16_ref, y_ref,
                   f32_scr, f4_scr, qm_scr, kp_scr, vp_scr, s1_scr, s4_scr, s16_scr,
                   num_scr, m_scr, den_scr, before_check=None):
    S = q_ref.shape[0]
    CH = 256
    bias_refs = (b1_ref, b4_ref, b16_ref)
    shifted_refs = (s1_scr, s4_scr, s16_scr)

    STEP = DILATIONS[1] // DILATIONS[0]
    assert all(b == a * STEP for a, b in zip(DILATIONS, DILATIONS[1:]))

    def regroup(emit, keep_natural):
        src, dst = f32_scr, f4_scr
        if keep_natural:
            for row0 in range(0, S, CH):
                emit(0, row0, src[row0:row0 + CH, :])
        for c in range(1, len(DILATIONS)):
            groups = DILATIONS[c - 1]
            L = S // groups
            ch = min(CH, L // STEP)
            for g in range(groups):
                for s in range(STEP):
                    for j0 in range(0, L // STEP, ch):
                        blk = src[pl.ds(g * L + s + j0 * STEP, ch, stride=STEP), :]
                        row0 = (g + groups * s) * (L // STEP) + j0
                        emit(c, row0, blk)
                        if c + 1 < len(DILATIONS):
                            dst[row0:row0 + ch, :] = blk
            src, dst = dst, src

    def emit_to(dst):
        def emit(c, row0, blk):
            dst[c - 1, row0:row0 + blk.shape[0], :] = blk.astype(jnp.bfloat16)
        return emit

    def emit_q(c, row0, blk):
        lt = _head0_lanes(blk.shape[0])
        rows = slice(row0, row0 + blk.shape[0])
        qm_scr[c, 0, rows, :] = jnp.where(lt, blk, 0.0).astype(jnp.bfloat16)
        qm_scr[c, 1, rows, :] = jnp.where(lt, 0.0, blk).astype(jnp.bfloat16)

    f32_scr[...] = q_ref[...].astype(jnp.float32) * Q_SCALE
    regroup(emit_q, True)
    f32_scr[...] = k_ref[...].astype(jnp.float32)
    regroup(emit_to(kp_scr), False)
    f32_scr[...] = v_ref[...].astype(jnp.float32)
    regroup(emit_to(vp_scr), False)

    shift = _softmax_shift(q_ref, k_ref, [_head_max(r[...]) for r in bias_refs])
    for bias_ref, shifted_ref in zip(bias_refs, shifted_refs):
        for v in range(bias_ref.shape[0]):
            for hh in range(PAIR):
                rows = slice(hh * QBLK, (hh + 1) * QBLK)
                shifted_ref[v, rows, :] = bias_ref[v, rows, :] - shift[hh]

    lane_q = _head0_lanes(QBLK)

    def block_operands(c, d, i):
        L = S // d
        nblk = L // QBLK
        nk = min(KBLK, L)
        r, bi = divmod(i, nblk)
        q0 = i * QBLK
        lhs = jnp.concatenate([qm_scr[c, 0, q0:q0 + QBLK, :],
                               qm_scr[c, 1, q0:q0 + QBLK, :]], axis=0)
        if nk < KBLK:
            k0, variant = r * L, 0
        else:
            k0 = r * L + min(max(bi * QBLK - SIDE, 0), L - KBLK)
            variant = 0 if bi == 0 else (2 if bi == nblk - 1 else 1)
        keys = slice(k0, k0 + nk)
        kb = k_ref[keys, :] if d == 1 else kp_scr[c - 1, keys, :]
        vb = v_ref[keys, :] if d == 1 else vp_scr[c - 1, keys, :]
        return lhs, kb, vb, variant

    def natural_rows(d, i):
        r, bi = divmod(i, S // d // QBLK)
        return pl.ds(r + bi * QBLK * d, QBLK, stride=d)

    def gate_store(rows, num, den):
        g = g_ref[rows, :].astype(jnp.float32)
        y_ref[rows, :] = (num * g / (den * (1.0 + jnp.exp(-g)))).astype(y_ref.dtype)

    def fast_probs(c, d, i):
        lhs, kb, vb, variant = block_operands(c, d, i)
        s = lax.dot_general(lhs, kb, (((1,), (1,)), ((), ())), preferred_element_type=jnp.float32)
        return jnp.exp2(s + shifted_refs[c][variant]).astype(jnp.bfloat16), vb

    def fast_pv(p, vb):
        pv = jnp.dot(p, jnp.concatenate([vb, jnp.ones_like(vb)], axis=1),
                     preferred_element_type=jnp.float32)
        return (jnp.where(lane_q, pv[:QBLK, :LANES], pv[QBLK:, :LANES]),
                jnp.where(lane_q, pv[:QBLK, LANES:], pv[QBLK:, LANES:]))

    nblocks = S // QBLK

    def fast_branch(c, ahead, sink):
        d = DILATIONS[c]
        pending = [fast_probs(c, d, i) for i in range(min(ahead, nblocks))]
        for i in range(nblocks):
            if i + ahead < nblocks:
                pending.append(fast_probs(c, d, i + ahead))
            sink(d, i, *fast_pv(*pending.pop(0)))

    def coarser_rows(c, i):
        L = S // DILATIONS[c]
        r, bi = divmod(i, L // QBLK)
        g, s = r % DILATIONS[c - 1], r // DILATIONS[c - 1]
        return pl.ds(g * L * STEP + s + bi * QBLK * STEP, QBLK, stride=STEP)

    den_mins = []

    def fast_sink(c):
        def sink(d, i, num, den):
            own = slice(i * QBLK, (i + 1) * QBLK)
            if c + 1 < len(DILATIONS):
                num = num + num_scr[c, own, :]
                den = den + den_scr[c, own, :]
            if c > 0:
                rows = coarser_rows(c, i)
                num_scr[c - 1, rows, :] = num
                den_scr[c - 1, rows, :] = den
            else:
                gate_store(own, num, den)
                den_mins.append(den)
        return sink

    for c in reversed(range(len(DILATIONS))):
        fast_branch(c, FAST_AHEAD[c], fast_sink(c))

    if before_check is not None:
        before_check()

    @pl.when(_out_of_range(den_mins))
    def _():
        def scores(c, d, i):
            lhs, kb, vb, variant = block_operands(c, d, i)
            s = lax.dot_general(lhs, kb, (((1,), (1,)), ((), ())),
                                preferred_element_type=jnp.float32) + bias_refs[c][variant]
            return s, vb

        def run_branch(c, ahead, sink):
            d = DILATIONS[c]
            pending = [scores(c, d, i) for i in range(min(ahead, nblocks))]
            for i in range(nblocks):
                if i + ahead < nblocks:
                    pending.append(scores(c, d, i + ahead))
                s, vb = pending.pop(0)
                sink(d, i, *_softmax_pv(s, vb, lane_q))

        def stash(c):
            def sink(d, i, num, m, den):
                rows = natural_rows(d, i)
                num_scr[c - 1, rows, :] = num
                m_scr[c - 1, rows, :] = m
                den_scr[c - 1, rows, :] = den
            return sink

        def combine(d, i, num, m, den):
            rows = slice(i * QBLK, (i + 1) * QBLK)
            m4, m16 = m_scr[0, rows, :], m_scr[1, rows, :]
            top = jnp.maximum(jnp.maximum(m, m4), m16)
            w1, w4, w16 = jnp.exp2(m - top), jnp.exp2(m4 - top), jnp.exp2(m16 - top)
            gate_store(rows,
                       w1 * num + w4 * num_scr[0, rows, :] + w16 * num_scr[1, rows, :],
                       w1 * den + w4 * den_scr[0, rows, :] + w16 * den_scr[1, rows, :])

        run_branch(2, 6, stash(2))
        run_branch(1, 2, stash(1))
        run_branch(0, 2, combine)


def _attn_b_kernel(q_ref, k_ref, v_ref, g_ref, bias_ref, y_ref, qm_scr):
    S = q_ref.shape[0]
    CH = 256
    lane_lt = _head0_lanes(CH)

    for row0 in range(0, S, CH):
        rows = slice(row0, row0 + CH)
        blk = q_ref[rows, :].astype(jnp.float32) * Q_SCALE
        qm_scr[0, rows, :] = jnp.where(lane_lt, blk, 0.0).astype(jnp.bfloat16)
        qm_scr[1, rows, :] = jnp.where(lane_lt, 0.0, blk).astype(jnp.bfloat16)

    lane_q = _head0_lanes(GRID_W)
    nk = NA_ROWS * GRID_W
    last = GRID_ROWS - NA_ROWS

    for r in range(GRID_ROWS):
        rows = slice(r * GRID_W, (r + 1) * GRID_W)
        k0 = min(max(r - NA_ROWS // 2, 0), last) * GRID_W
        variant = r if r < NA_ROWS // 2 else (NA_ROWS // 2 if r <= last + NA_ROWS // 2 else r - last)
        lhs = jnp.concatenate([qm_scr[0, rows, :], qm_scr[1, rows, :]], axis=0)
        num, _, den = _softmax_block(lhs, k_ref[k0:k0 + nk, :], v_ref[k0:k0 + nk, :],
                                     bias_ref[variant], lane_q)
        g = g_ref[rows, :].astype(jnp.float32)
        y_ref[rows, :] = (num * g / (den * (1.0 + jnp.exp(-g)))).astype(y_ref.dtype)


def _attn_kernel(qa_ref, ka_ref, va_ref, ga_ref, b1_ref, b4_ref, b16_ref,
                 qb_ref, kb_ref, vb_ref, gb_ref, bb_ref, ya_ref, yb_ref, *scratch):
    _attn_a_kernel(qa_ref, ka_ref, va_ref, ga_ref, b1_ref, b4_ref, b16_ref, ya_ref, *scratch[:-1],
                   before_check=functools.partial(_attn_b_kernel, qb_ref, kb_ref, vb_ref, gb_ref, bb_ref,
                                                  yb_ref, scratch[-1]))


def _attention(h, b1, b4, b16, bias_b):
    B, S, _ = h.shape
    assert N_HEADS_A == N_HEADS_B
    nhp = N_HEADS_A // PAIR
    blk = lambda off: pl.BlockSpec((None, S, LANES), lambda p, b: (b, 0, off + p))
    tab = lambda t: pl.BlockSpec((None,) + t.shape[1:], lambda p, b: (p, 0, 0, 0))
    out = lambda: pl.BlockSpec((None, S, LANES), lambda p, b: (b, 0, p))
    base = 4 * WIDTH_A // LANES
    return pl.pallas_call(
        _attn_kernel,
        grid=(nhp, B),
        in_specs=[blk(0), blk(nhp), blk(2 * nhp), blk(3 * nhp), tab(b1), tab(b4), tab(b16),
                  blk(base), blk(base + nhp), blk(base + 2 * nhp), blk(base + 3 * nhp), tab(bias_b)],
        out_specs=[out(), out()],
        out_shape=[jax.ShapeDtypeStruct((B, S, WIDTH_A), jnp.bfloat16),
                   jax.ShapeDtypeStruct((B, S, WIDTH_B), jnp.bfloat16)],
        scratch_shapes=[
            pltpu.VMEM((S, LANES), jnp.float32),
            pltpu.VMEM((S, LANES), jnp.float32),
            pltpu.VMEM((3, PAIR, S, LANES), jnp.bfloat16),
            pltpu.VMEM((2, S, LANES), jnp.bfloat16),
            pltpu.VMEM((2, S, LANES), jnp.bfloat16),
            pltpu.VMEM(b1.shape[1:], jnp.float32),
            pltpu.VMEM(b4.shape[1:], jnp.float32),
            pltpu.VMEM(b16.shape[1:], jnp.float32),
            pltpu.VMEM((2, S, LANES), jnp.float32),
            pltpu.VMEM((2, S, LANES), jnp.float32),
            pltpu.VMEM((2, S, LANES), jnp.float32),
            pltpu.VMEM((PAIR, S, LANES), jnp.bfloat16),
        ],
        compiler_params=pltpu.CompilerParams(
            dimension_semantics=("arbitrary", "arbitrary"), vmem_limit_bytes=VMEM_LIMIT),
        name="attention",
    )(h, h, h, h, b1, b4, b16, h, h, h, h, bias_b)


def _out_kernel(ya_ref, yb_ref, wa_ref, wb_ref, x_ref, gain_ref, bias_ref, o_ref, w_scr):
    @pl.when(pl.program_id(0) == 0)
    def _():
        w_scr[0] = wa_ref[...].astype(jnp.bfloat16)
        w_scr[1] = wb_ref[...].astype(jnp.bfloat16)

    half = o_ref.shape[0] // 2
    for rows in (slice(0, half), slice(half, 2 * half)):
        out = jnp.dot(ya_ref[rows, :], w_scr[0], preferred_element_type=jnp.float32)
        out = out + jnp.dot(yb_ref[rows, :], w_scr[1], preferred_element_type=jnp.float32)
        z = DEEPNORM_ALPHA * x_ref[rows, :] + out
        mu = jnp.mean(z, axis=-1, keepdims=True)
        zc = z - mu
        var = jnp.mean(zc * zc, axis=-1, keepdims=True)
        o_ref[rows, :] = zc * lax.rsqrt(var + LN_EPS) * gain_ref[...] + bias_ref[...]


def _output_projection(ya, yb, w, x2, gain, bias, tm=512):
    M, D = x2.shape
    assert WIDTH_A == WIDTH_B
    return pl.pallas_call(
        _out_kernel,
        grid=(M // tm,),
        in_specs=[pl.BlockSpec((tm, WIDTH_A), lambda i: (i, 0)),
                  pl.BlockSpec((tm, WIDTH_B), lambda i: (i, 0)),
                  pl.BlockSpec((WIDTH_A, D), lambda i: (0, 0), pipeline_mode=pl.Buffered(1)),
                  pl.BlockSpec((WIDTH_B, D), lambda i: (1, 0), pipeline_mode=pl.Buffered(1)),
                  pl.BlockSpec((tm, D), lambda i: (i, 0)),
                  pl.BlockSpec((1, D), lambda i: (0, 0)),
                  pl.BlockSpec((1, D), lambda i: (0, 0))],
        out_specs=pl.BlockSpec((tm, D), lambda i: (i, 0)),
        out_shape=jax.ShapeDtypeStruct((M, D), jnp.float32),
        scratch_shapes=[pltpu.VMEM((2, WIDTH_A, D), jnp.bfloat16)],
        compiler_params=pltpu.CompilerParams(
            dimension_semantics=("arbitrary",), vmem_limit_bytes=VMEM_LIMIT),
        name="out_proj_ln",
    )(ya, yb, w, w, x2, gain, bias)


def _t5_bucket_index(rel):
    half = NUM_BUCKETS // 2
    max_exact = half // 2
    n = np.abs(rel)
    large = max_exact + (np.log(np.maximum(n, 1) / max_exact)
                         / np.log(T5_MAX_DISTANCE / max_exact)
                         * (half - max_exact)).astype(np.int64)
    large = np.minimum(large, half - 1)
    return ((rel > 0).astype(np.int64) * half + np.where(n < max_exact, n, large)).astype(np.int32)


BAND_LANES = 4 * LANES
BAND_ZERO = BAND_LANES // 2


def _dilated_band_buckets():
    lane = np.arange(BAND_LANES)
    out = np.empty((len(DILATIONS), 2, SUBLANES, BAND_LANES), np.int32)
    for c, d in enumerate(DILATIONS):
        for j, start in enumerate((0, SIDE)):
            rel = lane - BAND_ZERO - start
            out[c, j] = np.where(np.abs(rel) <= SIDE, _t5_bucket_index(rel * d), -1)[None, :]
    return out


def _dilated_bias_kernel(t5_ref, band_ref, b1_ref, b4_ref, b16_ref):
    p = pl.program_id(0)
    for c, out_ref in enumerate((b1_ref, b4_ref, b16_ref)):
        nv, _, nk = out_ref.shape
        for hh in range(PAIR):
            rolled = []
            for j in range(min(nv, 2)):
                idx = band_ref[c, j]
                profile = jnp.full(idx.shape, NEG_INF, jnp.float32)
                for bkt in range(NUM_BUCKETS):
                    profile = jnp.where(idx == bkt, t5_ref[bkt, p * PAIR + hh] * LOG2E, profile)
                rows = jnp.concatenate([profile] * (QBLK // SUBLANES), axis=0)
                rolled.append(pltpu.roll(rows, 0, 1, stride=1, stride_axis=0))
            own = slice(hh * QBLK, (hh + 1) * QBLK)
            out_ref[0, own, :] = rolled[0][:, BAND_ZERO:BAND_ZERO + nk]
            if nv > 1:
                out_ref[1, own, :] = rolled[1][:, BAND_ZERO:BAND_ZERO + nk]
                out_ref[2, own, :] = rolled[0][:, BAND_ZERO - 2 * SIDE:BAND_ZERO - 2 * SIDE + nk]


def _dilated_bias(t5_table):
    band = jnp.asarray(_dilated_band_buckets())
    nhp = N_HEADS_A // PAIR
    out_shapes = []
    for d in DILATIONS:
        L = SEQ // d
        nv, nk = (1, L) if L < KBLK else (3, KBLK)
        out_shapes.append(jax.ShapeDtypeStruct((nhp, nv, PAIR * QBLK, nk), jnp.float32))
    return pl.pallas_call(
        _dilated_bias_kernel,
        grid=(nhp,),
        in_specs=[pl.BlockSpec(memory_space=pltpu.SMEM),
                  pl.BlockSpec(band.shape, lambda p: (0, 0, 0, 0))],
        out_specs=[pl.BlockSpec((None,) + o.shape[1:], lambda p: (p, 0, 0, 0)) for o in out_shapes],
        out_shape=out_shapes,
        compiler_params=pltpu.CompilerParams(dimension_semantics=("arbitrary",)),
        name="dilated_bias",
    )(t5_table, band)


RPB_ROWS = 2 * NA_ROWS - 1
RPB_COLS = 2 * NA_COLS - 1
NBR_VARIANT_ROWS = (tuple(range(NA_ROWS // 2)) + (NA_ROWS // 2,)
                    + tuple(range(GRID_ROWS - NA_ROWS // 2 + 1, GRID_ROWS)))


def _nbr_bias_kernel(rpb_ref, out_ref, m_scr):
    p = pl.program_id(0)
    shape = (GRID_W, LANES)
    qc = lax.broadcasted_iota(jnp.int32, shape, 0)
    lane = lax.broadcasted_iota(jnp.int32, shape, 1)
    kc = lane & (GRID_W - 1)
    qs = jnp.clip(qc - NA_COLS // 2, 0, GRID_W - NA_COLS)
    in_window = (kc >= qs) & (kc < qs + NA_COLS)
    first = lane < GRID_W
    u = lax.broadcasted_iota(jnp.int32, (SUBLANES, LANES), 1)
    delta = jnp.where(u < NA_COLS, u, jnp.where(u >= LANES - NA_COLS, u - LANES, u - GRID_W))
    col = jnp.where(jnp.abs(delta) < NA_COLS, delta + NA_COLS - 1, -1)
    for hh in range(PAIR):
        head = p * PAIR + hh

        for ri in range(RPB_ROWS):
            base = (head * RPB_ROWS + ri) * RPB_COLS
            profile = jnp.full(col.shape, NEG_INF, jnp.float32)
            for t in range(RPB_COLS):
                profile = jnp.where(col == t, rpb_ref[base + t] * LOG2E, profile)
            rows = jnp.concatenate([profile] * (GRID_W // SUBLANES), axis=0)
            rolled = pltpu.roll(rows, 0, 1, stride=1, stride_axis=0)
            m_scr[ri] = jnp.where(in_window, rolled, NEG_INF)
        for v, r in enumerate(NBR_VARIANT_ROWS):
            rs = min(max(r - NA_ROWS // 2, 0), GRID_ROWS - NA_ROWS)
            for j2 in range(NA_ROWS * GRID_W // LANES):
                ri = rs + 2 * j2 - r + NA_ROWS - 1
                out_ref[v, hh * GRID_W:(hh + 1) * GRID_W, j2 * LANES:(j2 + 1) * LANES] = \
                    jnp.where(first, m_scr[ri], m_scr[ri + 1])


def _neighbourhood_bias(rpb):
    nhp = N_HEADS_B // PAIR
    nv = len(NBR_VARIANT_ROWS)
    out = jax.ShapeDtypeStruct((nhp, nv, PAIR * GRID_W, NA_ROWS * GRID_W), jnp.float32)
    return pl.pallas_call(
        _nbr_bias_kernel,
        grid=(nhp,),
        in_specs=[pl.BlockSpec(memory_space=pltpu.SMEM)],
        out_specs=pl.BlockSpec((None,) + out.shape[1:], lambda p: (p, 0, 0, 0)),
        out_shape=out,
        scratch_shapes=[pltpu.VMEM((RPB_ROWS, GRID_W, LANES), jnp.float32)],
        compiler_params=pltpu.CompilerParams(dimension_semantics=("arbitrary",)),
        name="nbr_bias",
    )(rpb.reshape(-1))


def kernel(x, w_in, w_out, t5_bias, na_rpb, ln_gain, ln_bias):
    B, S, D = x.shape
    depth = w_in.shape[0]
    b1, b4, b16 = _dilated_bias(t5_bias.astype(jnp.float32))
    for layer in range(depth):
        h = _input_projection(x.reshape(B * S, D), w_in[layer])
        h = h.reshape(B, S, IN_WIDTH)
        ya, yb = _attention(h, b1, b4, b16, _neighbourhood_bias(na_rpb[layer].astype(jnp.float32)))
        out = _output_projection(ya.reshape(B * S, WIDTH_A), yb.reshape(B * S, WIDTH_B),
                                 w_out[layer], x.reshape(B * S, D),
                                 ln_gain[layer].reshape(1, D), ln_bias[layer].reshape(1, D))
        x = out.reshape(B, S, D)
    return x
```

```python
import functools

import numpy as np
import jax
import jax.numpy as jnp
from jax import lax
from jax.experimental import pallas as pl
from jax.experimental.pallas import tpu as pltpu

D_MODEL = 2048
SEQ = 2048
HEAD_DIM = 64
N_HEADS_A = 16
N_HEADS_B = 16
WIDTH_A = N_HEADS_A * HEAD_DIM
WIDTH_B = N_HEADS_B * HEAD_DIM
IN_WIDTH = 4 * WIDTH_A + 4 * WIDTH_B
DILATIONS = (1, 4, 16)
SIDE = 64
NUM_BUCKETS = 32
T5_MAX_DISTANCE = 1024
GRID_W = 64
GRID_ROWS = SEQ // GRID_W
NA_ROWS = 8
NA_COLS = 16
DEEPNORM_ALPHA = 2.0 ** 0.25
LN_EPS = 1e-5
NEG_INF = -1e30
SHIFT_MARGIN = 1.25
SHIFT_SAMPLES = 8
SHIFT_SAMPLE_ROWS = 32
MIN_DENOMINATOR = 2.0 ** -100
MAX_DENOMINATOR = 2.0 ** 100
LOG2E = 1.4426950408889634
Q_SCALE = HEAD_DIM ** -0.5 * LOG2E

LANES = 128
SUBLANES = 8
PAIR = LANES // HEAD_DIM
QBLK = 128
KBLK = QBLK + 2 * SIDE
FAST_AHEAD = (2, 2, 4)
VMEM_LIMIT = 56 * 1024 * 1024


def _proj_kernel(x_ref, w_ref, o_ref, xb_scr):
    @pl.when(pl.program_id(1) == 0)
    def _():
        xb_scr[...] = x_ref[...].astype(jnp.bfloat16)
    o_ref[...] = jnp.dot(xb_scr[...], w_ref[...].astype(jnp.bfloat16),
                         preferred_element_type=jnp.float32).astype(o_ref.dtype)


def _input_projection(x2, wb, tm=2048, tn=512):
    M, D = x2.shape
    N = wb.shape[1]
    last = M // tm - 1
    x_tile = lambda m, n: (jnp.minimum(m + jnp.minimum(n, 1), last), 0)
    return pl.pallas_call(
        _proj_kernel,
        grid=(M // tm, N // tn),
        in_specs=[pl.BlockSpec((tm, D), x_tile),
                  pl.BlockSpec((D, tn), lambda m, n: (0, n))],
        out_specs=pl.BlockSpec((tm, tn), lambda m, n: (m, n)),
        out_shape=jax.ShapeDtypeStruct((M, N), jnp.bfloat16),
        scratch_shapes=[pltpu.VMEM((tm, D), jnp.bfloat16)],
        compiler_params=pltpu.CompilerParams(
            dimension_semantics=("arbitrary", "arbitrary"), vmem_limit_bytes=VMEM_LIMIT),
        name="in_proj",
    )(x2, wb)


def _head0_lanes(rows):
    return lax.broadcasted_iota(jnp.int32, (rows, LANES), 1) < HEAD_DIM


def _softmax_block(lhs, kb, vb, bias, lane_lt):
    s = lax.dot_general(lhs, kb, (((1,), (1,)), ((), ())),
                        preferred_element_type=jnp.float32) + bias
    return _softmax_pv(s, vb, lane_lt)


def _softmax_pv(s, vb, lane_lt):
    nq = s.shape[0] // 2

    def half(sh):
        m = jnp.max(sh, axis=-1, keepdims=True)
        return jnp.exp2(sh - m).astype(jnp.bfloat16), m

    p0, m0 = half(s[:nq])
    p1, m1 = half(s[nq:])
    v_ones = jnp.concatenate([vb, jnp.ones_like(vb)], axis=1)
    pv = jnp.dot(jnp.concatenate([p0, p1], axis=0), v_ones, preferred_element_type=jnp.float32)
    return (jnp.where(lane_lt, pv[:nq, :LANES], pv[nq:, :LANES]), jnp.where(lane_lt, m0, m1),
            jnp.where(lane_lt, pv[:nq, LANES:], pv[nq:, LANES:]))


def _pair_norms2(xb):
    x = xb.astype(jnp.float32)
    row = lax.broadcasted_iota(jnp.int32, (LANES, 2 * LANES), 0)
    col = lax.broadcasted_iota(jnp.int32, (LANES, 2 * LANES), 1)
    pick = ((row < HEAD_DIM) == (col < LANES)).astype(jnp.bfloat16)
    return jnp.dot((x * x).astype(jnp.bfloat16), pick, preferred_element_type=jnp.float32)


def _head_max(t):
    while t.ndim > 2:
        t = jnp.max(t, axis=0)
    n = t.shape[0] // PAIR
    full = lambda x: jnp.max(jnp.max(x, axis=0, keepdims=True), axis=1, keepdims=True)
    return full(t[:n]), full(t[n:])


def _softmax_shift(q_ref, k_ref, tops):
    n_rows = q_ref.shape[0]

    def max_norm2(ref):
        rows = jnp.concatenate([ref[r0:r0 + SHIFT_SAMPLE_ROWS, :]
                                for r0 in range(0, n_rows, n_rows // SHIFT_SAMPLES)], axis=0)
        return jnp.max(_pair_norms2(rows), axis=0, keepdims=True)

    bound = jnp.sqrt(max_norm2(q_ref) * max_norm2(k_ref)) * (Q_SCALE * SHIFT_MARGIN)
    return [jnp.max(bound[:, hh * LANES:(hh + 1) * LANES], axis=1, keepdims=True)
            + functools.reduce(jnp.maximum, [t[hh] for t in tops]) for hh in range(PAIR)]


def _out_of_range(dens):
    lo = jnp.min(functools.reduce(jnp.minimum, dens))
    hi = jnp.max(functools.reduce(jnp.maximum, dens))
    return jnp.logical_not(jnp.logical_and(lo >= MIN_DENOMINATOR, hi <= MAX_DENOMINATOR))


def _attn_a_kernel(q_ref, k_ref, v_ref, g_ref, b1_ref, b4_ref, b16_ref, y_ref,
                   f32_scr, f4_scr, qm_scr, kp_scr, vp_scr, s1_scr, s4_scr, s16_scr,
                   num_scr, m_scr, den_scr, before_check=None):
    S = q_ref.shape[0]
    CH = 256
    bias_refs = (b1_ref, b4_ref, b16_ref)
    shifted_refs = (s1_scr, s4_scr, s16_scr)

    STEP = DILATIONS[1] // DILATIONS[0]
    assert all(b == a * STEP for a, b in zip(DILATIONS, DILATIONS[1:]))

    def regroup(emit, keep_natural):
        src, dst = f32_scr, f4_scr
        if keep_natural:
            for row0 in range(0, S, CH):
                emit(0, row0, src[row0:row0 + CH, :])
        for c in range(1, len(DILATIONS)):
            groups = DILATIONS[c - 1]
            L = S // groups
            ch = min(CH, L // STEP)
            for g in range(groups):
                for s in range(STEP):
                    for j0 in range(0, L // STEP, ch):
                        blk = src[pl.ds(g * L + s + j0 * STEP, ch, stride=STEP), :]
                        row0 = (g + groups * s) * (L // STEP) + j0
                        emit(c, row0, blk)
                        if c + 1 < len(DILATIONS):
                            dst[row0:row0 + ch, :] = blk
            src, dst = dst, src

    def emit_to(dst):
        def emit(c, row0, blk):
            dst[c - 1, row0:row0 + blk.shape[0], :] = blk.astype(jnp.bfloat16)
        return emit

    def emit_q(c, row0, blk):
        lt = _head0_lanes(blk.shape[0])
        rows = slice(row0, row0 + blk.shape[0])
        qm_scr[c, 0, rows, :] = jnp.where(lt, blk, 0.0).astype(jnp.bfloat16)
        qm_scr[c, 1, rows, :] = jnp.where(lt, 0.0, blk).astype(jnp.bfloat16)

    f32_scr[...] = q_ref[...].astype(jnp.float32) * Q_SCALE
    regroup(emit_q, True)
    f32_scr[...] = k_ref[...].astype(jnp.float32)
    regroup(emit_to(kp_scr), False)
    f32_scr[...] = v_ref[...].astype(jnp.float32)
    regroup(emit_to(vp_scr), False)

    shift = _softmax_shift(q_ref, k_ref, [_head_max(r[...]) for r in bias_refs])
    for bias_ref, shifted_ref in zip(bias_refs, shifted_refs):
        for v in range(bias_ref.shape[0]):
            for hh in range(PAIR):
                rows = slice(hh * QBLK, (hh + 1) * QBLK)
                shifted_ref[v, rows, :] = bias_ref[v, rows, :] - shift[hh]

    lane_q = _head0_lanes(QBLK)

    def block_operands(c, d, i):
        L = S // d
        nblk = L // QBLK
        nk = min(KBLK, L)
        r, bi = divmod(i, nblk)
        q0 = i * QBLK
        lhs = jnp.concatenate([qm_scr[c, 0, q0:q0 + QBLK, :],
                               qm_scr[c, 1, q0:q0 + QBLK, :]], axis=0)
        if nk < KBLK:
            k0, variant = r * L, 0
        else:
            k0 = r * L + min(max(bi * QBLK - SIDE, 0), L - KBLK)
            variant = 0 if bi == 0 else (2 if bi == nblk - 1 else 1)
        keys = slice(k0, k0 + nk)
        kb = k_ref[keys, :] if d == 1 else kp_scr[c - 1, keys, :]
        vb = v_ref[keys, :] if d == 1 else vp_scr[c - 1, keys, :]
        return lhs, kb, vb, variant

    def natural_rows(d, i):
        r, bi = divmod(i, S // d // QBLK)
        return pl.ds(r + bi * QBLK * d, QBLK, stride=d)

    def gate_store(rows, num, den):
        g = g_ref[rows, :].astype(jnp.float32)
        y_ref[rows, :] = (num * g / (den * (1.0 + jnp.exp(-g)))).astype(y_ref.dtype)

    def fast_probs(c, d, i):
        lhs, kb, vb, variant = block_operands(c, d, i)
        s = lax.dot_general(lhs, kb, (((1,), (1,)), ((), ())), preferred_element_type=jnp.float32)
        return jnp.exp2(s + shifted_refs[c][variant]).astype(jnp.bfloat16), vb

    def fast_pv(p, vb):
        pv = jnp.dot(p, jnp.concatenate([vb, jnp.ones_like(vb)], axis=1),
                     preferred_element_type=jnp.float32)
        return (jnp.where(lane_q, pv[:QBLK, :LANES], pv[QBLK:, :LANES]),
                jnp.where(lane_q, pv[:QBLK, LANES:], pv[QBLK:, LANES:]))

    nblocks = S // QBLK

    def fast_branch(c, ahead, sink):
        d = DILATIONS[c]
        pending = [fast_probs(c, d, i) for i in range(min(ahead, nblocks))]
        for i in range(nblocks):
            if i + ahead < nblocks:
                pending.append(fast_probs(c, d, i + ahead))
            sink(d, i, *fast_pv(*pending.pop(0)))

    def coarser_rows(c, i):
        L = S // DILATIONS[c]
        r, bi = divmod(i, L // QBLK)
        g, s = r % DILATIONS[c - 1], r // DILATIONS[c - 1]
        return pl.ds(g * L * STEP + s + bi * QBLK * STEP, QBLK, stride=STEP)

    den_mins = []

    def fast_sink(c):
        def sink(d, i, num, den):
            own = slice(i * QBLK, (i + 1) * QBLK)
            if c + 1 < len(DILATIONS):
                num = num + num_scr[c, own, :]
                den = den + den_scr[c, own, :]
            if c > 0:
                rows = coarser_rows(c, i)
                num_scr[c - 1, rows, :] = num
                den_scr[c - 1, rows, :] = den
            else:
                gate_store(own, num, den)
                den_mins.append(den)
        return sink

    for c in reversed(range(len(DILATIONS))):
        fast_branch(c, FAST_AHEAD[c], fast_sink(c))

    if before_check is not None:
        before_check()

    @pl.when(_out_of_range(den_mins))
    def _():
        def scores(c, d, i):
            lhs, kb, vb, variant = block_operands(c, d, i)
            s = lax.dot_general(lhs, kb, (((1,), (1,)), ((), ())),
                                preferred_element_type=jnp.float32) + bias_refs[c][variant]
            return s, vb

        def run_branch(c, ahead, sink):
            d = DILATIONS[c]
            pending = [scores(c, d, i) for i in range(min(ahead, nblocks))]
            for i in range(nblocks):
                if i + ahead < nblocks:
                    pending.append(scores(c, d, i + ahead))
                s, vb = pending.pop(0)
                sink(d, i, *_softmax_pv(s, vb, lane_q))

        def stash(c):
            def sink(d, i, num, m, den):
                rows = natural_rows(d, i)
                num_scr[c - 1, rows, :] = num
                m_scr[c - 1, rows, :] = m
                den_scr[c - 1, rows, :] = den
            return sink

        def combine(d, i, num, m, den):
            rows = slice(i * QBLK, (i + 1) * QBLK)
            m4, m16 = m_scr[0, rows, :], m_scr[1, rows, :]
            top = jnp.maximum(jnp.maximum(m, m4), m16)
            w1, w4, w16 = jnp.exp2(m - top), jnp.exp2(m4 - top), jnp.exp2(m16 - top)
            gate_store(rows,
                       w1 * num + w4 * num_scr[0, rows, :] + w16 * num_scr[1, rows, :],
                       w1 * den + w4 * den_scr[0, rows, :] + w16 * den_scr[1, rows, :])

        run_branch(2, 6, stash(2))
        run_branch(1, 2, stash(1))
        run_branch(0, 2, combine)


def _attn_b_kernel(q_ref, k_ref, v_ref, g_ref, bias_ref, y_ref, qm_scr):
    S = q_ref.shape[0]
    CH = 256
    lane_lt = _head0_lanes(CH)

    for row0 in range(0, S, CH):
        rows = slice(row0, row0 + CH)
        blk = q_ref[rows, :].astype(jnp.float32) * Q_SCALE
        qm_scr[0, rows, :] = jnp.where(lane_lt, blk, 0.0).astype(jnp.bfloat16)
        qm_scr[1, rows, :] = jnp.where(lane_lt, 0.0, blk).astype(jnp.bfloat16)

    lane_q = _head0_lanes(GRID_W)
    nk = NA_ROWS * GRID_W
    last = GRID_ROWS - NA_ROWS

    for r in range(GRID_ROWS):
        rows = slice(r * GRID_W, (r + 1) * GRID_W)
        k0 = min(max(r - NA_ROWS // 2, 0), last) * GRID_W
        variant = r if r < NA_ROWS // 2 else (NA_ROWS // 2 if r <= last + NA_ROWS // 2 else r - last)
        lhs = jnp.concatenate([qm_scr[0, rows, :], qm_scr[1, rows, :]], axis=0)
        num, _, den = _softmax_block(lhs, k_ref[k0:k0 + nk, :], v_ref[k0:k0 + nk, :],
                                     bias_ref[variant], lane_q)
        g = g_ref[rows, :].astype(jnp.float32)
        y_ref[rows, :] = (num * g / (den * (1.0 + jnp.exp(-g)))).astype(y_ref.dtype)


def _attn_kernel(qa_ref, ka_ref, va_ref, ga_ref, b1_ref, b4_ref, b16_ref,
                 qb_ref, kb_ref, vb_ref, gb_ref, bb_ref, ya_ref, yb_ref, *scratch):
    _attn_a_kernel(qa_ref, ka_ref, va_ref, ga_ref, b1_ref, b4_ref, b16_ref, ya_ref, *scratch[:-1],
                   before_check=functools.partial(_attn_b_kernel, qb_ref, kb_ref, vb_ref, gb_ref, bb_ref,
                                                  yb_ref, scratch[-1]))


def _attention(h, b1, b4, b16, bias_b):
    B, S, _ = h.shape
    assert N_HEADS_A == N_HEADS_B
    nhp = N_HEADS_A // PAIR
    blk = lambda off: pl.BlockSpec((None, S, LANES), lambda p, b: (b, 0, off + p))
    tab = lambda t: pl.BlockSpec((None,) + t.shape[1:], lambda p, b: (p, 0, 0, 0))
    out = lambda: pl.BlockSpec((None, S, LANES), lambda p, b: (b, 0, p))
    base = 4 * WIDTH_A // LANES
    return pl.pallas_call(
        _attn_kernel,
        grid=(nhp, B),
        in_specs=[blk(0), blk(nhp), blk(2 * nhp), blk(3 * nhp), tab(b1), tab(b4), tab(b16),
                  blk(base), blk(base + nhp), blk(base + 2 * nhp), blk(base + 3 * nhp), tab(bias_b)],
        out_specs=[out(), out()],
        out_shape=[jax.ShapeDtypeStruct((B, S, WIDTH_A), jnp.bfloat16),
                   jax.ShapeDtypeStruct((B, S, WIDTH_B), jnp.bfloat16)],
        scratch_shapes=[
            pltpu.VMEM((S, LANES), jnp.float32),
            pltpu.VMEM((S, LANES), jnp.float32),
            pltpu.VMEM((3, PAIR, S, LANES), jnp.bfloat16),
            pltpu.VMEM((2, S, LANES), jnp.bfloat16),
            pltpu.VMEM((2, S, LANES), jnp.bfloat16),
            pltpu.VMEM(b1.shape[1:], jnp.float32),
            pltpu.VMEM(b4.shape[1:], jnp.float32),
            pltpu.VMEM(b16.shape[1:], jnp.float32),
            pltpu.VMEM((2, S, LANES), jnp.float32),
            pltpu.VMEM((2, S, LANES), jnp.float32),
            pltpu.VMEM((2, S, LANES), jnp.float32),
            pltpu.VMEM((PAIR, S, LANES), jnp.bfloat16),
        ],
        compiler_params=pltpu.CompilerParams(
            dimension_semantics=("arbitrary", "arbitrary"), vmem_limit_bytes=VMEM_LIMIT),
        name="attention",
    )(h, h, h, h, b1, b4, b16, h, h, h, h, bias_b)


def _out_kernel(ya_ref, yb_ref, wa_ref, wb_ref, x_ref, gain_ref, bias_ref, o_ref, w_scr):
    @pl.when(pl.program_id(0) == 0)
    def _():
        w_scr[0] = wa_ref[...].astype(jnp.bfloat16)
        w_scr[1] = wb_ref[...].astype(jnp.bfloat16)

    half = o_ref.shape[0] // 2
    for rows in (slice(0, half), slice(half, 2 * half)):
        out = jnp.dot(ya_ref[rows, :], w_scr[0], preferred_element_type=jnp.float32)
        out = out + jnp.dot(yb_ref[rows, :], w_scr[1], preferred_element_type=jnp.float32)
        z = DEEPNORM_ALPHA * x_ref[rows, :] + out
        mu = jnp.mean(z, axis=-1, keepdims=True)
        zc = z - mu
        var = jnp.mean(zc * zc, axis=-1, keepdims=True)
        o_ref[rows, :] = zc * lax.rsqrt(var + LN_EPS) * gain_ref[...] + bias_ref[...]


def _output_projection(ya, yb, w, x2, gain, bias, tm=512):
    M, D = x2.shape
    assert WIDTH_A == WIDTH_B
    return pl.pallas_call(
        _out_kernel,
        grid=(M // tm,),
        in_specs=[pl.BlockSpec((tm, WIDTH_A), lambda i: (i, 0)),
                  pl.BlockSpec((tm, WIDTH_B), lambda i: (i, 0)),
                  pl.BlockSpec((WIDTH_A, D), lambda i: (0, 0), pipeline_mode=pl.Buffered(1)),
                  pl.BlockSpec((WIDTH_B, D), lambda i: (1, 0), pipeline_mode=pl.Buffered(1)),
                  pl.BlockSpec((tm, D), lambda i: (i, 0)),
                  pl.BlockSpec((1, D), lambda i: (0, 0)),
                  pl.BlockSpec((1, D), lambda i: (0, 0))],
        out_specs=pl.BlockSpec((tm, D), lambda i: (i, 0)),
        out_shape=jax.ShapeDtypeStruct((M, D), jnp.float32),
        scratch_shapes=[pltpu.VMEM((2, WIDTH_A, D), jnp.bfloat16)],
        compiler_params=pltpu.CompilerParams(
            dimension_semantics=("arbitrary",), vmem_limit_bytes=VMEM_LIMIT),
        name="out_proj_ln",
    )(ya, yb, w, w, x2, gain, bias)


def _t5_bucket_index(rel):
    half = NUM_BUCKETS // 2
    max_exact = half // 2
    n = np.abs(rel)
    large = max_exact + (np.log(np.maximum(n, 1) / max_exact)
                         / np.log(T5_MAX_DISTANCE / max_exact)
                         * (half - max_exact)).astype(np.int64)
    large = np.minimum(large, half - 1)
    return ((rel > 0).astype(np.int64) * half + np.where(n < max_exact, n, large)).astype(np.int32)


BAND_LANES = 4 * LANES
BAND_ZERO = BAND_LANES // 2


def _dilated_band_buckets():
    lane = np.arange(BAND_LANES)
    out = np.empty((len(DILATIONS), 2, SUBLANES, BAND_LANES), np.int32)
    for c, d in enumerate(DILATIONS):
        for j, start in enumerate((0, SIDE)):
            rel = lane - BAND_ZERO - start
            out[c, j] = np.where(np.abs(rel) <= SIDE, _t5_bucket_index(rel * d), -1)[None, :]
    return out


def _dilated_bias_kernel(t5_ref, band_ref, b1_ref, b4_ref, b16_ref):
    p = pl.program_id(0)
    for c, out_ref in enumerate((b1_ref, b4_ref, b16_ref)):
        nv, _, nk = out_ref.shape
        for hh in range(PAIR):
            rolled = []
            for j in range(min(nv, 2)):
                idx = band_ref[c, j]
                profile = jnp.full(idx.shape, NEG_INF, jnp.float32)
                for bkt in range(NUM_BUCKETS):
                    profile = jnp.where(idx == bkt, t5_ref[bkt, p * PAIR + hh] * LOG2E, profile)
                rows = jnp.concatenate([profile] * (QBLK // SUBLANES), axis=0)
                rolled.append(pltpu.roll(rows, 0, 1, stride=1, stride_axis=0))
            own = slice(hh * QBLK, (hh + 1) * QBLK)
            out_ref[0, own, :] = rolled[0][:, BAND_ZERO:BAND_ZERO + nk]
            if nv > 1:
                out_ref[1, own, :] = rolled[1][:, BAND_ZERO:BAND_ZERO + nk]
                out_ref[2, own, :] = rolled[0][:, BAND_ZERO - 2 * SIDE:BAND_ZERO - 2 * SIDE + nk]


def _dilated_bias(t5_table):
    band = jnp.asarray(_dilated_band_buckets())
    nhp = N_HEADS_A // PAIR
    out_shapes = []
    for d in DILATIONS:
        L = SEQ // d
        nv, nk = (1, L) if L < KBLK else (3, KBLK)
        out_shapes.append(jax.ShapeDtypeStruct((nhp, nv, PAIR * QBLK, nk), jnp.float32))
    return pl.pallas_call(
        _dilated_bias_kernel,
        grid=(nhp,),
        in_specs=[pl.BlockSpec(memory_space=pltpu.SMEM),
                  pl.BlockSpec(band.shape, lambda p: (0, 0, 0, 0))],
        out_specs=[pl.BlockSpec((None,) + o.shape[1:], lambda p: (p, 0, 0, 0)) for o in out_shapes],
        out_shape=out_shapes,
        compiler_params=pltpu.CompilerParams(dimension_semantics=("arbitrary",)),
        name="dilated_bias",
    )(t5_table, band)


RPB_ROWS = 2 * NA_ROWS - 1
RPB_COLS = 2 * NA_COLS - 1
NBR_VARIANT_ROWS = (tuple(range(NA_ROWS // 2)) + (NA_ROWS // 2,)
                    + tuple(range(GRID_ROWS - NA_ROWS // 2 + 1, GRID_ROWS)))


def _nbr_bias_kernel(rpb_ref, out_ref, m_scr):
    p = pl.program_id(0)
    shape = (GRID_W, LANES)
    qc = lax.broadcasted_iota(jnp.int32, shape, 0)
    lane = lax.broadcasted_iota(jnp.int32, shape, 1)
    kc = lane & (GRID_W - 1)
    qs = jnp.clip(qc - NA_COLS // 2, 0, GRID_W - NA_COLS)
    in_window = (kc >= qs) & (kc < qs + NA_COLS)
    first = lane < GRID_W
    u = lax.broadcasted_iota(jnp.int32, (SUBLANES, LANES), 1)
    delta = jnp.where(u < NA_COLS, u, jnp.where(u >= LANES - NA_COLS, u - LANES, u - GRID_W))
    col = jnp.where(jnp.abs(delta) < NA_COLS, delta + NA_COLS - 1, -1)
    for hh in range(PAIR):
        head = p * PAIR + hh

        for ri in range(RPB_ROWS):
            base = (head * RPB_ROWS + ri) * RPB_COLS
            profile = jnp.full(col.shape, NEG_INF, jnp.float32)
            for t in range(RPB_COLS):
                profile = jnp.where(col == t, rpb_ref[base + t] * LOG2E, profile)
            rows = jnp.concatenate([profile] * (GRID_W // SUBLANES), axis=0)
            rolled = pltpu.roll(rows, 0, 1, stride=1, stride_axis=0)
            m_scr[ri] = jnp.where(in_window, rolled, NEG_INF)
        for v, r in enumerate(NBR_VARIANT_ROWS):
            rs = min(max(r - NA_ROWS // 2, 0), GRID_ROWS - NA_ROWS)
            for j2 in range(NA_ROWS * GRID_W // LANES):
                ri = rs + 2 * j2 - r + NA_ROWS - 1
                out_ref[v, hh * GRID_W:(hh + 1) * GRID_W, j2 * LANES:(j2 + 1) * LANES] = \
                    jnp.where(first, m_scr[ri], m_scr[ri + 1])


def _neighbourhood_bias(rpb):
    nhp = N_HEADS_B // PAIR
    nv = len(NBR_VARIANT_ROWS)
    out = jax.ShapeDtypeStruct((nhp, nv, PAIR * GRID_W, NA_ROWS * GRID_W), jnp.float32)
    return pl.pallas_call(
        _nbr_bias_kernel,
        grid=(nhp,),
        in_specs=[pl.BlockSpec(memory_space=pltpu.SMEM)],
        out_specs=pl.BlockSpec((None,) + out.shape[1:], lambda p: (p, 0, 0, 0)),
        out_shape=out,
        scratch_shapes=[pltpu.VMEM((RPB_ROWS, GRID_W, LANES), jnp.float32)],
        compiler_params=pltpu.CompilerParams(dimension_semantics=("arbitrary",)),
        name="nbr_bias",
    )(rpb.reshape(-1))


def kernel(x, w_in, w_out, t5_bias, na_rpb, ln_gain, ln_bias):
    B, S, D = x.shape
    depth = w_in.shape[0]
    b1, b4, b16 = _dilated_bias(t5_bias.astype(jnp.float32))
    for layer in range(depth):
        h = _input_projection(x.reshape(B * S, D), w_in[layer])
        h = h.reshape(B, S, IN_WIDTH)
        ya, yb = _attention(h, b1, b4, b16, _neighbourhood_bias(na_rpb[layer].astype(jnp.float32)))
        out = _output_projection(ya.reshape(B * S, WIDTH_A), yb.reshape(B * S, WIDTH_B),
                                 w_out[layer], x.reshape(B * S, D),
                                 ln_gain[layer].reshape(1, D), ln_bias[layer].reshape(1, D))
        x = out.reshape(B, S, D)
    return x
```

```python
import functools

import numpy as np
import jax
import jax.numpy as jnp
from jax import lax
from jax.experimental import pallas as pl
from jax.experimental.pallas import tpu as pltpu

D_MODEL = 2048
SEQ = 2048
HEAD_DIM = 64
N_HEADS_A = 16
N_HEADS_B = 16
WIDTH_A = N_HEADS_A * HEAD_DIM
WIDTH_B = N_HEADS_B * HEAD_DIM
IN_WIDTH = 4 * WIDTH_A + 4 * WIDTH_B
DILATIONS = (1, 4, 16)
SIDE = 64
NUM_BUCKETS = 32
T5_MAX_DISTANCE = 1024
GRID_W = 64
GRID_ROWS = SEQ // GRID_W
NA_ROWS = 8
NA_COLS = 16
DEEPNORM_ALPHA = 2.0 ** 0.25
LN_EPS = 1e-5
NEG_INF = -1e30
SHIFT_MARGIN = 1.25
SHIFT_SAMPLES = 8
SHIFT_SAMPLE_ROWS = 32
MIN_DENOMINATOR = 2.0 ** -100
MAX_DENOMINATOR = 2.0 ** 100
LOG2E = 1.4426950408889634
Q_SCALE = HEAD_DIM ** -0.5 * LOG2E

LANES = 128
SUBLANES = 8
PAIR = LANES // HEAD_DIM
QBLK = 128
KBLK = QBLK + 2 * SIDE
FAST_AHEAD = (2, 2, 4)
VMEM_LIMIT = 56 * 1024 * 1024


def _proj_kernel(x_ref, w_ref, o_ref, xb_scr):
    @pl.when(pl.program_id(1) == 0)
    def _():
        xb_scr[...] = x_ref[...].astype(jnp.bfloat16)
    res = jnp.dot(xb_scr[...], w_ref[...].astype(jnp.bfloat16),
                  preferred_element_type=jnp.float32).astype(o_ref.dtype)
    for j in range(o_ref.shape[0]):
        o_ref[j] = res[:, j * LANES:(j + 1) * LANES]


def _input_projection(x2, wb, tm, tn=512):
    M, D = x2.shape
    N = wb.shape[1]
    last = M // tm - 1
    x_tile = lambda m, n: (jnp.minimum(m + jnp.minimum(n, 1), last), 0)
    return pl.pallas_call(
        _proj_kernel,
        grid=(M // tm, N // tn),
        in_specs=[pl.BlockSpec((tm, D), x_tile),
                  pl.BlockSpec((D, tn), lambda m, n: (0, n))],
        out_specs=pl.BlockSpec((None, tn // LANES, tm, LANES), lambda m, n: (m, n, 0, 0)),
        out_shape=jax.ShapeDtypeStruct((M // tm, N // LANES, tm, LANES), jnp.bfloat16),
        scratch_shapes=[pltpu.VMEM((tm, D), jnp.bfloat16)],
        compiler_params=pltpu.CompilerParams(
            dimension_semantics=("arbitrary", "arbitrary"), vmem_limit_bytes=VMEM_LIMIT),
        name="in_proj",
    )(x2, wb)


def _head0_lanes(rows):
    return lax.broadcasted_iota(jnp.int32, (rows, LANES), 1) < HEAD_DIM


def _softmax_block(lhs, kb, vb, bias, lane_lt):
    s = lax.dot_general(lhs, kb, (((1,), (1,)), ((), ())),
                        preferred_element_type=jnp.float32) + bias
    return _softmax_pv(s, vb, lane_lt)


def _softmax_pv(s, vb, lane_lt):
    nq = s.shape[0] // 2

    def half(sh):
        m = jnp.max(sh, axis=-1, keepdims=True)
        return jnp.exp2(sh - m).astype(jnp.bfloat16), m

    p0, m0 = half(s[:nq])
    p1, m1 = half(s[nq:])
    v_ones = jnp.concatenate([vb, jnp.ones_like(vb)], axis=1)
    pv = jnp.dot(jnp.concatenate([p0, p1], axis=0), v_ones, preferred_element_type=jnp.float32)
    return (jnp.where(lane_lt, pv[:nq, :LANES], pv[nq:, :LANES]), jnp.where(lane_lt, m0, m1),
            jnp.where(lane_lt, pv[:nq, LANES:], pv[nq:, LANES:]))


def _pair_norms2(xb):
    x = xb.astype(jnp.float32)
    row = lax.broadcasted_iota(jnp.int32, (LANES, 2 * LANES), 0)
    col = lax.broadcasted_iota(jnp.int32, (LANES, 2 * LANES), 1)
    pick = ((row < HEAD_DIM) == (col < LANES)).astype(jnp.bfloat16)
    return jnp.dot((x * x).astype(jnp.bfloat16), pick, preferred_element_type=jnp.float32)


def _head_max(t):
    while t.ndim > 2:
        t = jnp.max(t, axis=0)
    n = t.shape[0] // PAIR
    full = lambda x: jnp.max(jnp.max(x, axis=0, keepdims=True), axis=1, keepdims=True)
    return full(t[:n]), full(t[n:])


def _softmax_shift(q_ref, k_ref, tops):
    n_rows = q_ref.shape[0]

    def max_norm2(ref):
        rows = jnp.concatenate([ref[r0:r0 + SHIFT_SAMPLE_ROWS, :]
                                for r0 in range(0, n_rows, n_rows // SHIFT_SAMPLES)], axis=0)
        return jnp.max(_pair_norms2(rows), axis=0, keepdims=True)

    bound = jnp.sqrt(max_norm2(q_ref) * max_norm2(k_ref)) * (Q_SCALE * SHIFT_MARGIN)
    return [jnp.max(bound[:, hh * LANES:(hh + 1) * LANES], axis=1, keepdims=True)
            + functools.reduce(jnp.maximum, [t[hh] for t in tops]) for hh in range(PAIR)]


def _out_of_range(dens):
    lo = jnp.min(functools.reduce(jnp.minimum, dens))
    hi = jnp.max(functools.reduce(jnp.maximum, dens))
    return jnp.logical_not(jnp.logical_and(lo >= MIN_DENOMINATOR, hi <= MAX_DENOMINATOR))


def _attn_a_kernel(q_ref, k_ref, v_ref, g_ref, b1_ref, b4_ref, b16_ref, y_ref,
                   f32_scr, f4_scr, qm_scr, kp_scr, vp_scr, s1_scr, s4_scr, s16_scr,
                   num_scr, m_scr, den_scr, before_check=None):
    S = q_ref.shape[0]
    CH = 256
    bias_refs = (b1_ref, b4_ref, b16_ref)
    shifted_refs = (s1_scr, s4_scr, s16_scr)

    STEP = DILATIONS[1] // DILATIONS[0]
    assert all(b == a * STEP for a, b in zip(DILATIONS, DILATIONS[1:]))

    def regroup(emit, keep_natural):
        src, dst = f32_scr, f4_scr
        if keep_natural:
            for row0 in range(0, S, CH):
                emit(0, row0, src[row0:row0 + CH, :])
        for c in range(1, len(DILATIONS)):
            groups = DILATIONS[c - 1]
            L = S // groups
            ch = min(CH, L // STEP)
            for g in range(groups):
                for s in range(STEP):
                    for j0 in range(0, L // STEP, ch):
                        blk = src[pl.ds(g * L + s + j0 * STEP, ch, stride=STEP), :]
                        row0 = (g + groups * s) * (L // STEP) + j0
                        emit(c, row0, blk)
                        if c + 1 < len(DILATIONS):
                            dst[row0:row0 + ch, :] = blk
            src, dst = dst, src

    def emit_to(dst):
        def emit(c, row0, blk):
            dst[c - 1, row0:row0 + blk.shape[0], :] = blk.astype(jnp.bfloat16)
        return emit

    def emit_q(c, row0, blk):
        lt = _head0_lanes(blk.shape[0])
        rows = slice(row0, row0 + blk.shape[0])
        qm_scr[c, 0, rows, :] = jnp.where(lt, blk, 0.0).astype(jnp.bfloat16)
        qm_scr[c, 1, rows, :] = jnp.where(lt, 0.0, blk).astype(jnp.bfloat16)

    f32_scr[...] = q_ref[...].astype(jnp.float32) * Q_SCALE
    regroup(emit_q, True)
    f32_scr[...] = k_ref[...].astype(jnp.float32)
    regroup(emit_to(kp_scr), False)
    f32_scr[...] = v_ref[...].astype(jnp.float32)
    regroup(emit_to(vp_scr), False)

    shift = _softmax_shift(q_ref, k_ref, [_head_max(r[...]) for r in bias_refs])
    for bias_ref, shifted_ref in zip(bias_refs, shifted_refs):
        for v in range(bias_ref.shape[0]):
            for hh in range(PAIR):
                rows = slice(hh * QBLK, (hh + 1) * QBLK)
                shifted_ref[v, rows, :] = bias_ref[v, rows, :] - shift[hh]

    lane_q = _head0_lanes(QBLK)

    def block_operands(c, d, i):
        L = S // d
        nblk = L // QBLK
        nk = min(KBLK, L)
        r, bi = divmod(i, nblk)
        q0 = i * QBLK
        lhs = jnp.concatenate([qm_scr[c, 0, q0:q0 + QBLK, :],
                               qm_scr[c, 1, q0:q0 + QBLK, :]], axis=0)
        if nk < KBLK:
            k0, variant = r * L, 0
        else:
            k0 = r * L + min(max(bi * QBLK - SIDE, 0), L - KBLK)
            variant = 0 if bi == 0 else (2 if bi == nblk - 1 else 1)
        keys = slice(k0, k0 + nk)
        kb = k_ref[keys, :] if d == 1 else kp_scr[c - 1, keys, :]
        vb = v_ref[keys, :] if d == 1 else vp_scr[c - 1, keys, :]
        return lhs, kb, vb, variant

    def natural_rows(d, i):
        r, bi = divmod(i, S // d // QBLK)
        return pl.ds(r + bi * QBLK * d, QBLK, stride=d)

    def gate_store(rows, num, den):
        g = g_ref[rows, :].astype(jnp.float32)
        y_ref[rows, :] = (num * g / (den * (1.0 + jnp.exp(-g)))).astype(y_ref.dtype)

    def fast_probs(c, d, i):
        lhs, kb, vb, variant = block_operands(c, d, i)
        s = lax.dot_general(lhs, kb, (((1,), (1,)), ((), ())), preferred_element_type=jnp.float32)
        return jnp.exp2(s + shifted_refs[c][variant]).astype(jnp.bfloat16), vb

    def fast_pv(p, vb):
        pv = jnp.dot(p, jnp.concatenate([vb, jnp.ones_like(vb)], axis=1),
                     preferred_element_type=jnp.float32)
        return (jnp.where(lane_q, pv[:QBLK, :LANES], pv[QBLK:, :LANES]),
                jnp.where(lane_q, pv[:QBLK, LANES:], pv[QBLK:, LANES:]))

    nblocks = S // QBLK

    def fast_branch(c, ahead, sink):
        d = DILATIONS[c]
        pending = [fast_probs(c, d, i) for i in range(min(ahead, nblocks))]
        for i in range(nblocks):
            if i + ahead < nblocks:
                pending.append(fast_probs(c, d, i + ahead))
            sink(d, i, *fast_pv(*pending.pop(0)))

    def coarser_rows(c, i):
        L = S // DILATIONS[c]
        r, bi = divmod(i, L // QBLK)
        g, s = r % DILATIONS[c - 1], r // DILATIONS[c - 1]
        return pl.ds(g * L * STEP + s + bi * QBLK * STEP, QBLK, stride=STEP)

    den_mins = []

    def fast_sink(c):
        def sink(d, i, num, den):
            own = slice(i * QBLK, (i + 1) * QBLK)
            if c + 1 < len(DILATIONS):
                num = num + num_scr[c, own, :]
                den = den + den_scr[c, own, :]
            if c > 0:
                rows = coarser_rows(c, i)
                num_scr[c - 1, rows, :] = num
                den_scr[c - 1, rows, :] = den
            else:
                gate_store(own, num, den)
                den_mins.append(den)
        return sink

    for c in reversed(range(len(DILATIONS))):
        fast_branch(c, FAST_AHEAD[c], fast_sink(c))

    if before_check is not None:
        before_check()

    @pl.when(_out_of_range(den_mins))
    def _():
        def scores(c, d, i):
            lhs, kb, vb, variant = block_operands(c, d, i)
            s = lax.dot_general(lhs, kb, (((1,), (1,)), ((), ())),
                                preferred_element_type=jnp.float32) + bias_refs[c][variant]
            return s, vb

        def run_branch(c, ahead, sink):
            d = DILATIONS[c]
            pending = [scores(c, d, i) for i in range(min(ahead, nblocks))]
            for i in range(nblocks):
                if i + ahead < nblocks:
                    pending.append(scores(c, d, i + ahead))
                s, vb = pending.pop(0)
                sink(d, i, *_softmax_pv(s, vb, lane_q))

        def stash(c):
            def sink(d, i, num, m, den):
                rows = natural_rows(d, i)
                num_scr[c - 1, rows, :] = num
                m_scr[c - 1, rows, :] = m
                den_scr[c - 1, rows, :] = den
            return sink

        def combine(d, i, num, m, den):
            rows = slice(i * QBLK, (i + 1) * QBLK)
            m4, m16 = m_scr[0, rows, :], m_scr[1, rows, :]
            top = jnp.maximum(jnp.maximum(m, m4), m16)
            w1, w4, w16 = jnp.exp2(m - top), jnp.exp2(m4 - top), jnp.exp2(m16 - top)
            gate_store(rows,
                       w1 * num + w4 * num_scr[0, rows, :] + w16 * num_scr[1, rows, :],
                       w1 * den + w4 * den_scr[0, rows, :] + w16 * den_scr[1, rows, :])

        run_branch(2, 6, stash(2))
        run_branch(1, 2, stash(1))
        run_branch(0, 2, combine)


def _attn_b_kernel(q_ref, k_ref, v_ref, g_ref, bias_ref, y_ref, qm_scr):
    S = q_ref.shape[0]
    CH = 256
    lane_lt = _head0_lanes(CH)

    for row0 in range(0, S, CH):
        rows = slice(row0, row0 + CH)
        blk = q_ref[rows, :].astype(jnp.float32) * Q_SCALE
        qm_scr[0, rows, :] = jnp.where(lane_lt, blk, 0.0).astype(jnp.bfloat16)
        qm_scr[1, rows, :] = jnp.where(lane_lt, 0.0, blk).astype(jnp.bfloat16)

    lane_q = _head0_lanes(GRID_W)
    nk = NA_ROWS * GRID_W
    last = GRID_ROWS - NA_ROWS

    for r in range(GRID_ROWS):
        rows = slice(r * GRID_W, (r + 1) * GRID_W)
        k0 = min(max(r - NA_ROWS // 2, 0), last) * GRID_W
        variant = r if r < NA_ROWS // 2 else (NA_ROWS // 2 if r <= last + NA_ROWS // 2 else r - last)
        lhs = jnp.concatenate([qm_scr[0, rows, :], qm_scr[1, rows, :]], axis=0)
        num, _, den = _softmax_block(lhs, k_ref[k0:k0 + nk, :], v_ref[k0:k0 + nk, :],
                                     bias_ref[variant], lane_q)
        g = g_ref[rows, :].astype(jnp.float32)
        y_ref[rows, :] = (num * g / (den * (1.0 + jnp.exp(-g)))).astype(y_ref.dtype)


def _attn_kernel(qa_ref, ka_ref, va_ref, ga_ref, b1_ref, b4_ref, b16_ref,
                 qb_ref, kb_ref, vb_ref, gb_ref, bb_ref, ya_ref, yb_ref, *scratch):
    _attn_a_kernel(qa_ref, ka_ref, va_ref, ga_ref, b1_ref, b4_ref, b16_ref, ya_ref, *scratch[:-1],
                   before_check=functools.partial(_attn_b_kernel, qb_ref, kb_ref, vb_ref, gb_ref, bb_ref,
                                                  yb_ref, scratch[-1]))


def _attention(h, b1, b4, b16, bias_b):
    B, _, S, _ = h.shape
    assert N_HEADS_A == N_HEADS_B
    nhp = N_HEADS_A // PAIR
    blk = lambda off: pl.BlockSpec((None, None, S, LANES), lambda p, b: (b, off + p, 0, 0))
    tab = lambda t: pl.BlockSpec((None,) + t.shape[1:], lambda p, b: (p, 0, 0, 0))
    out = lambda: pl.BlockSpec((None, S, LANES), lambda p, b: (b, 0, p))
    base = 4 * WIDTH_A // LANES
    return pl.pallas_call(
        _attn_kernel,
        grid=(nhp, B),
        in_specs=[blk(0), blk(nhp), blk(2 * nhp), blk(3 * nhp), tab(b1), tab(b4), tab(b16),
                  blk(base), blk(base + nhp), blk(base + 2 * nhp), blk(base + 3 * nhp), tab(bias_b)],
        out_specs=[out(), out()],
        out_shape=[jax.ShapeDtypeStruct((B, S, WIDTH_A), jnp.bfloat16),
                   jax.ShapeDtypeStruct((B, S, WIDTH_B), jnp.bfloat16)],
        scratch_shapes=[
            pltpu.VMEM((S, LANES), jnp.float32),
            pltpu.VMEM((S, LANES), jnp.float32),
            pltpu.VMEM((3, PAIR, S, LANES), jnp.bfloat16),
            pltpu.VMEM((2, S, LANES), jnp.bfloat16),
            pltpu.VMEM((2, S, LANES), jnp.bfloat16),
            pltpu.VMEM(b1.shape[1:], jnp.float32),
            pltpu.VMEM(b4.shape[1:], jnp.float32),
            pltpu.VMEM(b16.shape[1:], jnp.float32),
            pltpu.VMEM((2, S, LANES), jnp.float32),
            pltpu.VMEM((2, S, LANES), jnp.float32),
            pltpu.VMEM((2, S, LANES), jnp.float32),
            pltpu.VMEM((PAIR, S, LANES), jnp.bfloat16),
        ],
        compiler_params=pltpu.CompilerParams(
            dimension_semantics=("arbitrary", "arbitrary"), vmem_limit_bytes=VMEM_LIMIT),
        name="attention",
    )(h, h, h, h, b1, b4, b16, h, h, h, h, bias_b)


def _out_kernel(ya_ref, yb_ref, wa_ref, wb_ref, x_ref, gain_ref, bias_ref, o_ref, w_scr):
    @pl.when(pl.program_id(0) == 0)
    def _():
        w_scr[0] = wa_ref[...].astype(jnp.bfloat16)
        w_scr[1] = wb_ref[...].astype(jnp.bfloat16)

    half = o_ref.shape[0] // 2
    for rows in (slice(0, half), slice(half, 2 * half)):
        out = jnp.dot(ya_ref[rows, :], w_scr[0], preferred_element_type=jnp.float32)
        out = out + jnp.dot(yb_ref[rows, :], w_scr[1], preferred_element_type=jnp.float32)
        z = DEEPNORM_ALPHA * x_ref[rows, :] + out
        mu = jnp.mean(z, axis=-1, keepdims=True)
        zc = z - mu
        var = jnp.mean(zc * zc, axis=-1, keepdims=True)
        o_ref[rows, :] = zc * lax.rsqrt(var + LN_EPS) * gain_ref[...] + bias_ref[...]


def _output_projection(ya, yb, w, x2, gain, bias, tm=512):
    M, D = x2.shape
    assert WIDTH_A == WIDTH_B
    return pl.pallas_call(
        _out_kernel,
        grid=(M // tm,),
        in_specs=[pl.BlockSpec((tm, WIDTH_A), lambda i: (i, 0)),
                  pl.BlockSpec((tm, WIDTH_B), lambda i: (i, 0)),
                  pl.BlockSpec((WIDTH_A, D), lambda i: (0, 0), pipeline_mode=pl.Buffered(1)),
                  pl.BlockSpec((WIDTH_B, D), lambda i: (1, 0), pipeline_mode=pl.Buffered(1)),
                  pl.BlockSpec((tm, D), lambda i: (i, 0)),
                  pl.BlockSpec((1, D), lambda i: (0, 0)),
                  pl.BlockSpec((1, D), lambda i: (0, 0))],
        out_specs=pl.BlockSpec((tm, D), lambda i: (i, 0)),
        out_shape=jax.ShapeDtypeStruct((M, D), jnp.float32),
        scratch_shapes=[pltpu.VMEM((2, WIDTH_A, D), jnp.bfloat16)],
        compiler_params=pltpu.CompilerParams(
            dimension_semantics=("arbitrary",), vmem_limit_bytes=VMEM_LIMIT),
        name="out_proj_ln",
    )(ya, yb, w, w, x2, gain, bias)


def _t5_bucket_index(rel):
    half = NUM_BUCKETS // 2
    max_exact = half // 2
    n = np.abs(rel)
    large = max_exact + (np.log(np.maximum(n, 1) / max_exact)
                         / np.log(T5_MAX_DISTANCE / max_exact)
                         * (half - max_exact)).astype(np.int64)
    large = np.minimum(large, half - 1)
    return ((rel > 0).astype(np.int64) * half + np.where(n < max_exact, n, large)).astype(np.int32)


BAND_LANES = 4 * LANES
BAND_ZERO = BAND_LANES // 2


def _dilated_band_buckets():
    lane = np.arange(BAND_LANES)
    out = np.empty((len(DILATIONS), 2, SUBLANES, BAND_LANES), np.int32)
    for c, d in enumerate(DILATIONS):
        for j, start in enumerate((0, SIDE)):
            rel = lane - BAND_ZERO - start
            out[c, j] = np.where(np.abs(rel) <= SIDE, _t5_bucket_index(rel * d), -1)[None, :]
    return out


def _dilated_bias_kernel(t5_ref, band_ref, b1_ref, b4_ref, b16_ref):
    p = pl.program_id(0)
    for c, out_ref in enumerate((b1_ref, b4_ref, b16_ref)):
        nv, _, nk = out_ref.shape
        for hh in range(PAIR):
            rolled = []
            for j in range(min(nv, 2)):
                idx = band_ref[c, j]
                profile = jnp.full(idx.shape, NEG_INF, jnp.float32)
                for bkt in range(NUM_BUCKETS):
                    profile = jnp.where(idx == bkt, t5_ref[bkt, p * PAIR + hh] * LOG2E, profile)
                rows = jnp.concatenate([profile] * (QBLK // SUBLANES), axis=0)
                rolled.append(pltpu.roll(rows, 0, 1, stride=1, stride_axis=0))
            own = slice(hh * QBLK, (hh + 1) * QBLK)
            out_ref[0, own, :] = rolled[0][:, BAND_ZERO:BAND_ZERO + nk]
            if nv > 1:
                out_ref[1, own, :] = rolled[1][:, BAND_ZERO:BAND_ZERO + nk]
                out_ref[2, own, :] = rolled[0][:, BAND_ZERO - 2 * SIDE:BAND_ZERO - 2 * SIDE + nk]


def _dilated_bias(t5_table):
    band = jnp.asarray(_dilated_band_buckets())
    nhp = N_HEADS_A // PAIR
    out_shapes = []
    for d in DILATIONS:
        L = SEQ // d
        nv, nk = (1, L) if L < KBLK else (3, KBLK)
        out_shapes.append(jax.ShapeDtypeStruct((nhp, nv, PAIR * QBLK, nk), jnp.float32))
    return pl.pallas_call(
        _dilated_bias_kernel,
        grid=(nhp,),
        in_specs=[pl.BlockSpec(memory_space=pltpu.SMEM),
                  pl.BlockSpec(band.shape, lambda p: (0, 0, 0, 0))],
        out_specs=[pl.BlockSpec((None,) + o.shape[1:], lambda p: (p, 0, 0, 0)) for o in out_shapes],
        out_shape=out_shapes,
        compiler_params=pltpu.CompilerParams(dimension_semantics=("arbitrary",)),
        name="dilated_bias",
    )(t5_table, band)


RPB_ROWS = 2 * NA_ROWS - 1
RPB_COLS = 2 * NA_COLS - 1
NBR_VARIANT_ROWS = (tuple(range(NA_ROWS // 2)) + (NA_ROWS // 2,)
                    + tuple(range(GRID_ROWS - NA_ROWS // 2 + 1, GRID_ROWS)))


def _nbr_bias_kernel(rpb_ref, out_ref, m_scr):
    p = pl.program_id(0)
    shape = (GRID_W, LANES)
    qc = lax.broadcasted_iota(jnp.int32, shape, 0)
    lane = lax.broadcasted_iota(jnp.int32, shape, 1)
    kc = lane & (GRID_W - 1)
    qs = jnp.clip(qc - NA_COLS // 2, 0, GRID_W - NA_COLS)
    in_window = (kc >= qs) & (kc < qs + NA_COLS)
    first = lane < GRID_W
    u = lax.broadcasted_iota(jnp.int32, (SUBLANES, LANES), 1)
    delta = jnp.where(u < NA_COLS, u, jnp.where(u >= LANES - NA_COLS, u - LANES, u - GRID_W))
    col = jnp.where(jnp.abs(delta) < NA_COLS, delta + NA_COLS - 1, -1)
    for hh in range(PAIR):
        head = p * PAIR + hh

        for ri in range(RPB_ROWS):
            base = (head * RPB_ROWS + ri) * RPB_COLS
            profile = jnp.full(col.shape, NEG_INF, jnp.float32)
            for t in range(RPB_COLS):
                profile = jnp.where(col == t, rpb_ref[base + t] * LOG2E, profile)
            rows = jnp.concatenate([profile] * (GRID_W // SUBLANES), axis=0)
            rolled = pltpu.roll(rows, 0, 1, stride=1, stride_axis=0)
            m_scr[ri] = jnp.where(in_window, rolled, NEG_INF)
        for v, r in enumerate(NBR_VARIANT_ROWS):
            rs = min(max(r - NA_ROWS // 2, 0), GRID_ROWS - NA_ROWS)
            for j2 in range(NA_ROWS * GRID_W // LANES):
                ri = rs + 2 * j2 - r + NA_ROWS - 1
                out_ref[v, hh * GRID_W:(hh + 1) * GRID_W, j2 * LANES:(j2 + 1) * LANES] = \
                    jnp.where(first, m_scr[ri], m_scr[ri + 1])


def _neighbourhood_bias(rpb):
    nhp = N_HEADS_B // PAIR
    nv = len(NBR_VARIANT_ROWS)
    out = jax.ShapeDtypeStruct((nhp, nv, PAIR * GRID_W, NA_ROWS * GRID_W), jnp.float32)
    return pl.pallas_call(
        _nbr_bias_kernel,
        grid=(nhp,),
        in_specs=[pl.BlockSpec(memory_space=pltpu.SMEM)],
        out_specs=pl.BlockSpec((None,) + out.shape[1:], lambda p: (p, 0, 0, 0)),
        out_shape=out,
        scratch_shapes=[pltpu.VMEM((RPB_ROWS, GRID_W, LANES), jnp.float32)],
        compiler_params=pltpu.CompilerParams(dimension_semantics=("arbitrary",)),
        name="nbr_bias",
    )(rpb.reshape(-1))


def kernel(x, w_in, w_out, t5_bias, na_rpb, ln_gain, ln_bias):
    B, S, D = x.shape
    depth = w_in.shape[0]
    b1, b4, b16 = _dilated_bias(t5_bias.astype(jnp.float32))
    for layer in range(depth):
        h = _input_projection(x.reshape(B * S, D), w_in[layer], tm=S)
        ya, yb = _attention(h, b1, b4, b16, _neighbourhood_bias(na_rpb[layer].astype(jnp.float32)))
        out = _output_projection(ya.reshape(B * S, WIDTH_A), yb.reshape(B * S, WIDTH_B),
                                 w_out[layer], x.reshape(B * S, D),
                                 ln_gain[layer].reshape(1, D), ln_bias[layer].reshape(1, D))
        x = out.reshape(B, S, D)
    return x
```

```python
import functools

import numpy as np
import jax
import jax.numpy as jnp
from jax import lax
from jax.experimental import pallas as pl
from jax.experimental.pallas import tpu as pltpu

D_MODEL = 2048
SEQ = 2048
HEAD_DIM = 64
N_HEADS_A = 16
N_HEADS_B = 16
WIDTH_A = N_HEADS_A * HEAD_DIM
WIDTH_B = N_HEADS_B * HEAD_DIM
IN_WIDTH = 4 * WIDTH_A + 4 * WIDTH_B
DILATIONS = (1, 4, 16)
SIDE = 64
NUM_BUCKETS = 32
T5_MAX_DISTANCE = 1024
GRID_W = 64
GRID_ROWS = SEQ // GRID_W
NA_ROWS = 8
NA_COLS = 16
DEEPNORM_ALPHA = 2.0 ** 0.25
LN_EPS = 1e-5
NEG_INF = -1e30
SHIFT_MARGIN = 1.25
SHIFT_SAMPLES = 8
SHIFT_SAMPLE_ROWS = 32
MIN_DENOMINATOR = 2.0 ** -100
MAX_DENOMINATOR = 2.0 ** 100
LOG2E = 1.4426950408889634
Q_SCALE = HEAD_DIM ** -0.5 * LOG2E

LANES = 128
SUBLANES = 8
PAIR = LANES // HEAD_DIM
QBLK = 128
KBLK = QBLK + 2 * SIDE
FAST_AHEAD = (2, 2, 4)
VMEM_LIMIT = 56 * 1024 * 1024


def _proj_kernel(x_ref, w_ref, o_ref, xb_scr):
    @pl.when(pl.program_id(1) == 0)
    def _():
        xb_scr[...] = x_ref[...].astype(jnp.bfloat16)
    o_ref[...] = jnp.dot(xb_scr[...], w_ref[...].astype(jnp.bfloat16),
                         preferred_element_type=jnp.float32).astype(o_ref.dtype)


def _input_projection(x2, wb, tm=2048, tn=512):
    M, D = x2.shape
    N = wb.shape[1]
    last = M // tm - 1
    x_tile = lambda m, n: (jnp.minimum(m + jnp.minimum(n, 1), last), 0)
    return pl.pallas_call(
        _proj_kernel,
        grid=(M // tm, N // tn),
        in_specs=[pl.BlockSpec((tm, D), x_tile),
                  pl.BlockSpec((D, tn), lambda m, n: (0, n))],
        out_specs=pl.BlockSpec((tm, tn), lambda m, n: (m, n)),
        out_shape=jax.ShapeDtypeStruct((M, N), jnp.bfloat16),
        scratch_shapes=[pltpu.VMEM((tm, D), jnp.bfloat16)],
        compiler_params=pltpu.CompilerParams(
            dimension_semantics=("arbitrary", "arbitrary"), vmem_limit_bytes=VMEM_LIMIT),
        name="in_proj",
    )(x2, wb)


def _head0_lanes(rows):
    return lax.broadcasted_iota(jnp.int32, (rows, LANES), 1) < HEAD_DIM


def _softmax_block(lhs, kb, vb, bias, lane_lt):
    s = lax.dot_general(lhs, kb, (((1,), (1,)), ((), ())),
                        preferred_element_type=jnp.float32) + bias
    return _softmax_pv(s, vb, lane_lt)


def _softmax_pv(s, vb, lane_lt):
    nq = s.shape[0] // 2

    def half(sh):
        m = jnp.max(sh, axis=-1, keepdims=True)
        return jnp.exp2(sh - m).astype(jnp.bfloat16), m

    p0, m0 = half(s[:nq])
    p1, m1 = half(s[nq:])
    v_ones = jnp.concatenate([vb, jnp.ones_like(vb)], axis=1)
    pv = jnp.dot(jnp.concatenate([p0, p1], axis=0), v_ones, preferred_element_type=jnp.float32)
    return (jnp.where(lane_lt, pv[:nq, :LANES], pv[nq:, :LANES]), jnp.where(lane_lt, m0, m1),
            jnp.where(lane_lt, pv[:nq, LANES:], pv[nq:, LANES:]))


def _pair_norms2(xb):
    x = xb.astype(jnp.float32)
    row = lax.broadcasted_iota(jnp.int32, (LANES, 2 * LANES), 0)
    col = lax.broadcasted_iota(jnp.int32, (LANES, 2 * LANES), 1)
    pick = ((row < HEAD_DIM) == (col < LANES)).astype(jnp.bfloat16)
    return jnp.dot((x * x).astype(jnp.bfloat16), pick, preferred_element_type=jnp.float32)


def _head_max(t):
    while t.ndim > 2:
        t = jnp.max(t, axis=0)
    n = t.shape[0] // PAIR
    full = lambda x: jnp.max(jnp.max(x, axis=0, keepdims=True), axis=1, keepdims=True)
    return full(t[:n]), full(t[n:])


def _softmax_shift(q_ref, k_ref, tops):
    n_rows = q_ref.shape[0]

    def max_norm2(ref):
        rows = jnp.concatenate([ref[r0:r0 + SHIFT_SAMPLE_ROWS, :]
                                for r0 in range(0, n_rows, n_rows // SHIFT_SAMPLES)], axis=0)
        return jnp.max(_pair_norms2(rows), axis=0, keepdims=True)

    bound = jnp.sqrt(max_norm2(q_ref) * max_norm2(k_ref)) * (Q_SCALE * SHIFT_MARGIN)
    return [jnp.max(bound[:, hh * LANES:(hh + 1) * LANES], axis=1, keepdims=True)
            + functools.reduce(jnp.maximum, [t[hh] for t in tops]) for hh in range(PAIR)]


def _out_of_range(dens):
    lo = jnp.min(functools.reduce(jnp.minimum, dens))
    hi = jnp.max(functools.reduce(jnp.maximum, dens))
    return jnp.logical_not(jnp.logical_and(lo >= MIN_DENOMINATOR, hi <= MAX_DENOMINATOR))


def _attn_a_kernel(q_ref, k_ref, v_ref, g_ref, b1_ref, b4_ref, b16_ref, y_ref,
                   f32_scr, f4_scr, qm_scr, kp_scr, vp_scr, s1_scr, s4_scr, s16_scr,
                   num_scr, m_scr, den_scr, before_check=None):
    S = q_ref.shape[0]
    CH = 256
    bias_refs = (b1_ref, b4_ref, b16_ref)
    shifted_refs = (s1_scr, s4_scr, s16_scr)

    STEP = DILATIONS[1] // DILATIONS[0]
    assert all(b == a * STEP for a, b in zip(DILATIONS, DILATIONS[1:]))

    def regroup(emit, keep_natural):
        src, dst = f32_scr, f4_scr
        if keep_natural:
            for row0 in range(0, S, CH):
                emit(0, row0, src[row0:row0 + CH, :])
        for c in range(1, len(DILATIONS)):
            groups = DILATIONS[c - 1]
            L = S // groups
            ch = min(CH, L // STEP)
            for g in range(groups):
                for s in range(STEP):
                    for j0 in range(0, L // STEP, ch):
                        blk = src[pl.ds(g * L + s + j0 * STEP, ch, stride=STEP), :]
                        row0 = (g + groups * s) * (L // STEP) + j0
                        emit(c, row0, blk)
                        if c + 1 < len(DILATIONS):
                            dst[row0:row0 + ch, :] = blk
            src, dst = dst, src

    def emit_to(dst):
        def emit(c, row0, blk):
            dst[c - 1, row0:row0 + blk.shape[0], :] = blk.astype(jnp.bfloat16)
        return emit

    def emit_q(c, row0, blk):
        lt = _head0_lanes(blk.shape[0])
        rows = slice(row0, row0 + blk.shape[0])
        qm_scr[c, 0, rows, :] = jnp.where(lt, blk, 0.0).astype(jnp.bfloat16)
        qm_scr[c, 1, rows, :] = jnp.where(lt, 0.0, blk).astype(jnp.bfloat16)

    f32_scr[...] = q_ref[...].astype(jnp.float32) * Q_SCALE
    regroup(emit_q, True)
    f32_scr[...] = k_ref[...].astype(jnp.float32)
    regroup(emit_to(kp_scr), False)
    f32_scr[...] = v_ref[...].astype(jnp.float32)
    regroup(emit_to(vp_scr), False)

    shift = _softmax_shift(q_ref, k_ref, [_head_max(r[...]) for r in bias_refs])
    for bias_ref, shifted_ref in zip(bias_refs, shifted_refs):
        for v in range(bias_ref.shape[0]):
            for hh in range(PAIR):
                rows = slice(hh * QBLK, (hh + 1) * QBLK)
                shifted_ref[v, rows, :] = bias_ref[v, rows, :] - shift[hh]

    lane_q = _head0_lanes(QBLK)

    def block_operands(c, d, i):
        L = S // d
        nblk = L // QBLK
        nk = min(KBLK, L)
        r, bi = divmod(i, nblk)
        q0 = i * QBLK
        lhs = jnp.concatenate([qm_scr[c, 0, q0:q0 + QBLK, :],
                               qm_scr[c, 1, q0:q0 + QBLK, :]], axis=0)
        if nk < KBLK:
            k0, variant = r * L, 0
        else:
            k0 = r * L + min(max(bi * QBLK - SIDE, 0), L - KBLK)
            variant = 0 if bi == 0 else (2 if bi == nblk - 1 else 1)
        keys = slice(k0, k0 + nk)
        kb = k_ref[keys, :] if d == 1 else kp_scr[c - 1, keys, :]
        vb = v_ref[keys, :] if d == 1 else vp_scr[c - 1, keys, :]
        return lhs, kb, vb, variant

    def natural_rows(d, i):
        r, bi = divmod(i, S // d // QBLK)
        return pl.ds(r + bi * QBLK * d, QBLK, stride=d)

    def gate_store(rows, num, den):
        g = g_ref[rows, :].astype(jnp.float32)
        y_ref[rows, :] = (num * g / (den * (1.0 + jnp.exp(-g)))).astype(y_ref.dtype)

    def fast_probs(c, d, i):
        lhs, kb, vb, variant = block_operands(c, d, i)
        s = lax.dot_general(lhs, kb, (((1,), (1,)), ((), ())), preferred_element_type=jnp.float32)
        return jnp.exp2(s + shifted_refs[c][variant]).astype(jnp.bfloat16), vb

    def fast_pv(p, vb):
        pv = jnp.dot(p, jnp.concatenate([vb, jnp.ones_like(vb)], axis=1),
                     preferred_element_type=jnp.float32)
        return (jnp.where(lane_q, pv[:QBLK, :LANES], pv[QBLK:, :LANES]),
                jnp.where(lane_q, pv[:QBLK, LANES:], pv[QBLK:, LANES:]))

    nblocks = S // QBLK

    def fast_branch(c, ahead, sink):
        d = DILATIONS[c]
        pending = [fast_probs(c, d, i) for i in range(min(ahead, nblocks))]
        for i in range(nblocks):
            if i + ahead < nblocks:
                pending.append(fast_probs(c, d, i + ahead))
            sink(d, i, *fast_pv(*pending.pop(0)))

    def coarser_rows(c, i):
        L = S // DILATIONS[c]
        r, bi = divmod(i, L // QBLK)
        g, s = r % DILATIONS[c - 1], r // DILATIONS[c - 1]
        return pl.ds(g * L * STEP + s + bi * QBLK * STEP, QBLK, stride=STEP)

    den_mins = []

    def fast_sink(c):
        def sink(d, i, num, den):
            own = slice(i * QBLK, (i + 1) * QBLK)
            if c + 1 < len(DILATIONS):
                num = num + num_scr[c, own, :]
                den = den + den_scr[c, own, :]
            if c > 0:
                rows = coarser_rows(c, i)
                num_scr[c - 1, rows, :] = num
                den_scr[c - 1, rows, :] = den
            else:
                gate_store(own, num, den)
                den_mins.append(den)
        return sink

    for c in reversed(range(len(DILATIONS))):
        fast_branch(c, FAST_AHEAD[c], fast_sink(c))

    if before_check is not None:
        before_check()

    @pl.when(_out_of_range(den_mins))
    def _():
        def scores(c, d, i):
            lhs, kb, vb, variant = block_operands(c, d, i)
            s = lax.dot_general(lhs, kb, (((1,), (1,)), ((), ())),
                                preferred_element_type=jnp.float32) + bias_refs[c][variant]
            return s, vb

        def run_branch(c, ahead, sink):
            d = DILATIONS[c]
            pending = [scores(c, d, i) for i in range(min(ahead, nblocks))]
            for i in range(nblocks):
                if i + ahead < nblocks:
                    pending.append(scores(c, d, i + ahead))
                s, vb = pending.pop(0)
                sink(d, i, *_softmax_pv(s, vb, lane_q))

        def stash(c):
            def sink(d, i, num, m, den):
                rows = natural_rows(d, i)
                num_scr[c - 1, rows, :] = num
                m_scr[c - 1, rows, :] = m
                den_scr[c - 1, rows, :] = den
            return sink

        def combine(d, i, num, m, den):
            rows = slice(i * QBLK, (i + 1) * QBLK)
            m4, m16 = m_scr[0, rows, :], m_scr[1, rows, :]
            top = jnp.maximum(jnp.maximum(m, m4), m16)
            w1, w4, w16 = jnp.exp2(m - top), jnp.exp2(m4 - top), jnp.exp2(m16 - top)
            gate_store(rows,
                       w1 * num + w4 * num_scr[0, rows, :] + w16 * num_scr[1, rows, :],
                       w1 * den + w4 * den_scr[0, rows, :] + w16 * den_scr[1, rows, :])

        run_branch(2, 6, stash(2))
        run_branch(1, 2, stash(1))
        run_branch(0, 2, combine)


def _attn_b_kernel(q_ref, k_ref, v_ref, g_ref, bias_ref, y_ref, qm_scr):
    S = q_ref.shape[0]
    CH = 256
    lane_lt = _head0_lanes(CH)

    for row0 in range(0, S, CH):
        rows = slice(row0, row0 + CH)
        blk = q_ref[rows, :].astype(jnp.float32) * Q_SCALE
        qm_scr[0, rows, :] = jnp.where(lane_lt, blk, 0.0).astype(jnp.bfloat16)
        qm_scr[1, rows, :] = jnp.where(lane_lt, 0.0, blk).astype(jnp.bfloat16)

    lane_q = _head0_lanes(GRID_W)
    nk = NA_ROWS * GRID_W
    last = GRID_ROWS - NA_ROWS

    for r in range(GRID_ROWS):
        rows = slice(r * GRID_W, (r + 1) * GRID_W)
        k0 = min(max(r - NA_ROWS // 2, 0), last) * GRID_W
        variant = r if r < NA_ROWS // 2 else (NA_ROWS // 2 if r <= last + NA_ROWS // 2 else r - last)
        lhs = jnp.concatenate([qm_scr[0, rows, :], qm_scr[1, rows, :]], axis=0)
        num, _, den = _softmax_block(lhs, k_ref[k0:k0 + nk, :], v_ref[k0:k0 + nk, :],
                                     bias_ref[variant], lane_q)
        g = g_ref[rows, :].astype(jnp.float32)
        y_ref[rows, :] = (num * g / (den * (1.0 + jnp.exp(-g)))).astype(y_ref.dtype)


def _attn_kernel(qa_ref, ka_ref, va_ref, ga_ref, b1_ref, b4_ref, b16_ref,
                 qb_ref, kb_ref, vb_ref, gb_ref, bb_ref, ya_ref, yb_ref, *scratch):
    _attn_a_kernel(qa_ref, ka_ref, va_ref, ga_ref, b1_ref, b4_ref, b16_ref, ya_ref, *scratch[:-1],
                   before_check=functools.partial(_attn_b_kernel, qb_ref, kb_ref, vb_ref, gb_ref, bb_ref,
                                                  yb_ref, scratch[-1]))


def _attention(h, b1, b4, b16, bias_b):
    B, S, _ = h.shape
    assert N_HEADS_A == N_HEADS_B
    nhp = N_HEADS_A // PAIR
    blk = lambda off: pl.BlockSpec((None, S, LANES), lambda p, b: (b, 0, off + p))
    tab = lambda t: pl.BlockSpec((None,) + t.shape[1:], lambda p, b: (p, 0, 0, 0))
    out = lambda: pl.BlockSpec((None, S, LANES), lambda p, b: (b, 0, p))
    base = 4 * WIDTH_A // LANES
    return pl.pallas_call(
        _attn_kernel,
        grid=(nhp, B),
        in_specs=[blk(0), blk(nhp), blk(2 * nhp), blk(3 * nhp), tab(b1), tab(b4), tab(b16),
                  blk(base), blk(base + nhp), blk(base + 2 * nhp), blk(base + 3 * nhp), tab(bias_b)],
        out_specs=[out(), out()],
        out_shape=[jax.ShapeDtypeStruct((B, S, WIDTH_A), jnp.bfloat16),
                   jax.ShapeDtypeStruct((B, S, WIDTH_B), jnp.bfloat16)],
        scratch_shapes=[
            pltpu.VMEM((S, LANES), jnp.float32),
            pltpu.VMEM((S, LANES), jnp.float32),
            pltpu.VMEM((3, PAIR, S, LANES), jnp.bfloat16),
            pltpu.VMEM((2, S, LANES), jnp.bfloat16),
            pltpu.VMEM((2, S, LANES), jnp.bfloat16),
            pltpu.VMEM(b1.shape[1:], jnp.float32),
            pltpu.VMEM(b4.shape[1:], jnp.float32),
            pltpu.VMEM(b16.shape[1:], jnp.float32),
            pltpu.VMEM((2, S, LANES), jnp.float32),
            pltpu.VMEM((2, S, LANES), jnp.float32),
            pltpu.VMEM((2, S, LANES), jnp.float32),
            pltpu.VMEM((PAIR, S, LANES), jnp.bfloat16),
        ],
        compiler_params=pltpu.CompilerParams(
            dimension_semantics=("arbitrary", "arbitrary"), vmem_limit_bytes=VMEM_LIMIT),
        name="attention",
    )(h, h, h, h, b1, b4, b16, h, h, h, h, bias_b)


def _out_kernel(ya_ref, yb_ref, wa_ref, wb_ref, x_ref, gain_ref, bias_ref, o_ref, w_scr):
    @pl.when(pl.program_id(0) == 0)
    def _():
        w_scr[0] = wa_ref[...].astype(jnp.bfloat16)
        w_scr[1] = wb_ref[...].astype(jnp.bfloat16)

    half = o_ref.shape[0] // 2
    for rows in (slice(0, half), slice(half, 2 * half)):
        out = jnp.dot(ya_ref[rows, :], w_scr[0], preferred_element_type=jnp.float32)
        out = out + jnp.dot(yb_ref[rows, :], w_scr[1], preferred_element_type=jnp.float32)
        z = DEEPNORM_ALPHA * x_ref[rows, :] + out
        mu = jnp.mean(z, axis=-1, keepdims=True)
        zc = z - mu
        var = jnp.mean(zc * zc, axis=-1, keepdims=True)
        o_ref[rows, :] = zc * lax.rsqrt(var + LN_EPS) * gain_ref[...] + bias_ref[...]


def _output_projection(ya, yb, w, x2, gain, bias, tm=512):
    M, D = x2.shape
    assert WIDTH_A == WIDTH_B
    return pl.pallas_call(
        _out_kernel,
        grid=(M // tm,),
        in_specs=[pl.BlockSpec((tm, WIDTH_A), lambda i: (i, 0)),
                  pl.BlockSpec((tm, WIDTH_B), lambda i: (i, 0)),
                  pl.BlockSpec((WIDTH_A, D), lambda i: (0, 0), pipeline_mode=pl.Buffered(1)),
                  pl.BlockSpec((WIDTH_B, D), lambda i: (1, 0), pipeline_mode=pl.Buffered(1)),
                  pl.BlockSpec((tm, D), lambda i: (i, 0)),
                  pl.BlockSpec((1, D), lambda i: (0, 0)),
                  pl.BlockSpec((1, D), lambda i: (0, 0))],
        out_specs=pl.BlockSpec((tm, D), lambda i: (i, 0)),
        out_shape=jax.ShapeDtypeStruct((M, D), jnp.float32),
        scratch_shapes=[pltpu.VMEM((2, WIDTH_A, D), jnp.bfloat16)],
        compiler_params=pltpu.CompilerParams(
            dimension_semantics=("arbitrary",), vmem_limit_bytes=VMEM_LIMIT),
        name="out_proj_ln",
    )(ya, yb, w, w, x2, gain, bias)


def _t5_bucket_index(rel):
    half = NUM_BUCKETS // 2
    max_exact = half // 2
    n = np.abs(rel)
    large = max_exact + (np.log(np.maximum(n, 1) / max_exact)
                         / np.log(T5_MAX_DISTANCE / max_exact)
                         * (half - max_exact)).astype(np.int64)
    large = np.minimum(large, half - 1)
    return ((rel > 0).astype(np.int64) * half + np.where(n < max_exact, n, large)).astype(np.int32)


BAND_LANES = 4 * LANES
BAND_ZERO = BAND_LANES // 2


def _dilated_band_buckets():
    lane = np.arange(BAND_LANES)
    out = np.empty((len(DILATIONS), 2, SUBLANES, BAND_LANES), np.int32)
    for c, d in enumerate(DILATIONS):
        for j, start in enumerate((0, SIDE)):
            rel = lane - BAND_ZERO - start
            out[c, j] = np.where(np.abs(rel) <= SIDE, _t5_bucket_index(rel * d), -1)[None, :]
    return out


def _dilated_bias_kernel(t5_ref, band_ref, b1_ref, b4_ref, b16_ref):
    p = pl.program_id(0)
    for c, out_ref in enumerate((b1_ref, b4_ref, b16_ref)):
        nv, _, nk = out_ref.shape
        for hh in range(PAIR):
            rolled = []
            for j in range(min(nv, 2)):
                idx = band_ref[c, j]
                profile = jnp.full(idx.shape, NEG_INF, jnp.float32)
                for bkt in range(NUM_BUCKETS):
                    profile = jnp.where(idx == bkt, t5_ref[bkt, p * PAIR + hh] * LOG2E, profile)
                rows = jnp.concatenate([profile] * (QBLK // SUBLANES), axis=0)
                rolled.append(pltpu.roll(rows, 0, 1, stride=1, stride_axis=0))
            own = slice(hh * QBLK, (hh + 1) * QBLK)
            out_ref[0, own, :] = rolled[0][:, BAND_ZERO:BAND_ZERO + nk]
            if nv > 1:
                out_ref[1, own, :] = rolled[1][:, BAND_ZERO:BAND_ZERO + nk]
                out_ref[2, own, :] = rolled[0][:, BAND_ZERO - 2 * SIDE:BAND_ZERO - 2 * SIDE + nk]


RPB_ROWS = 2 * NA_ROWS - 1
RPB_COLS = 2 * NA_COLS - 1
NBR_VARIANT_ROWS = (tuple(range(NA_ROWS // 2)) + (NA_ROWS // 2,)
                    + tuple(range(GRID_ROWS - NA_ROWS // 2 + 1, GRID_ROWS)))


def _nbr_bias_kernel(rpb_ref, out_ref, m_scr):
    p = pl.program_id(0)
    shape = (GRID_W, LANES)
    qc = lax.broadcasted_iota(jnp.int32, shape, 0)
    lane = lax.broadcasted_iota(jnp.int32, shape, 1)
    kc = lane & (GRID_W - 1)
    qs = jnp.clip(qc - NA_COLS // 2, 0, GRID_W - NA_COLS)
    in_window = (kc >= qs) & (kc < qs + NA_COLS)
    first = lane < GRID_W
    u = lax.broadcasted_iota(jnp.int32, (SUBLANES, LANES), 1)
    delta = jnp.where(u < NA_COLS, u, jnp.where(u >= LANES - NA_COLS, u - LANES, u - GRID_W))
    col = jnp.where(jnp.abs(delta) < NA_COLS, delta + NA_COLS - 1, -1)
    for hh in range(PAIR):
        head = p * PAIR + hh

        for ri in range(RPB_ROWS):
            base = (head * RPB_ROWS + ri) * RPB_COLS
            profile = jnp.full(col.shape, NEG_INF, jnp.float32)
            for t in range(RPB_COLS):
                profile = jnp.where(col == t, rpb_ref[base + t] * LOG2E, profile)
            rows = jnp.concatenate([profile] * (GRID_W // SUBLANES), axis=0)
            rolled = pltpu.roll(rows, 0, 1, stride=1, stride_axis=0)
            m_scr[ri] = jnp.where(in_window, rolled, NEG_INF)
        for v, r in enumerate(NBR_VARIANT_ROWS):
            rs = min(max(r - NA_ROWS // 2, 0), GRID_ROWS - NA_ROWS)
            for j2 in range(NA_ROWS * GRID_W // LANES):
                ri = rs + 2 * j2 - r + NA_ROWS - 1
                out_ref[v, hh * GRID_W:(hh + 1) * GRID_W, j2 * LANES:(j2 + 1) * LANES] = \
                    jnp.where(first, m_scr[ri], m_scr[ri + 1])


def _bias_kernel(t5_ref, rpb_ref, band_ref, b1_ref, b4_ref, b16_ref, bb_ref, m_scr):
    _dilated_bias_kernel(t5_ref, band_ref, b1_ref, b4_ref, b16_ref)
    _nbr_bias_kernel(rpb_ref, bb_ref, m_scr)


def _bias_tables(t5_table, rpb):
    assert N_HEADS_A == N_HEADS_B
    band = jnp.asarray(_dilated_band_buckets())
    nhp = N_HEADS_A // PAIR
    out_shapes = []
    for d in DILATIONS:
        L = SEQ // d
        nv, nk = (1, L) if L < KBLK else (3, KBLK)
        out_shapes.append(jax.ShapeDtypeStruct((nhp, nv, PAIR * QBLK, nk), jnp.float32))
    out_shapes.append(jax.ShapeDtypeStruct(
        (nhp, len(NBR_VARIANT_ROWS), PAIR * GRID_W, NA_ROWS * GRID_W), jnp.float32))
    return pl.pallas_call(
        _bias_kernel,
        grid=(nhp,),
        in_specs=[pl.BlockSpec(memory_space=pltpu.SMEM),
                  pl.BlockSpec(memory_space=pltpu.SMEM),
                  pl.BlockSpec(band.shape, lambda p: (0, 0, 0, 0))],
        out_specs=[pl.BlockSpec((None,) + o.shape[1:], lambda p: (p, 0, 0, 0)) for o in out_shapes],
        out_shape=out_shapes,
        scratch_shapes=[pltpu.VMEM((RPB_ROWS, GRID_W, LANES), jnp.float32)],
        compiler_params=pltpu.CompilerParams(dimension_semantics=("arbitrary",)),
        name="bias_tables",
    )(t5_table, rpb.reshape(-1), band)


def kernel(x, w_in, w_out, t5_bias, na_rpb, ln_gain, ln_bias):
    B, S, D = x.shape
    depth = w_in.shape[0]
    for layer in range(depth):
        tables = _bias_tables(t5_bias.astype(jnp.float32), na_rpb[layer].astype(jnp.float32))
        h = _input_projection(x.reshape(B * S, D), w_in[layer])
        h = h.reshape(B, S, IN_WIDTH)
        ya, yb = _attention(h, *tables)
        out = _output_projection(ya.reshape(B * S, WIDTH_A), yb.reshape(B * S, WIDTH_B),
                                 w_out[layer], x.reshape(B * S, D),
                                 ln_gain[layer].reshape(1, D), ln_bias[layer].reshape(1, D))
        x = out.reshape(B, S, D)
    return x
```

```python
import functools

import numpy as np
import jax
import jax.numpy as jnp
from jax import lax
from jax.experimental import pallas as pl
from jax.experimental.pallas import tpu as pltpu

D_MODEL = 2048
SEQ = 2048
HEAD_DIM = 64
N_HEADS_A = 16
N_HEADS_B = 16
WIDTH_A = N_HEADS_A * HEAD_DIM
WIDTH_B = N_HEADS_B * HEAD_DIM
IN_WIDTH = 4 * WIDTH_A + 4 * WIDTH_B
DILATIONS = (1, 4, 16)
SIDE = 64
NUM_BUCKETS = 32
T5_MAX_DISTANCE = 1024
GRID_W = 64
GRID_ROWS = SEQ // GRID_W
NA_ROWS = 8
NA_COLS = 16
DEEPNORM_ALPHA = 2.0 ** 0.25
LN_EPS = 1e-5
NEG_INF = -1e30
SHIFT_MARGIN = 1.25
SHIFT_SAMPLES = 8
SHIFT_SAMPLE_ROWS = 32
MIN_DENOMINATOR = 2.0 ** -100
MAX_DENOMINATOR = 2.0 ** 100
LOG2E = 1.4426950408889634
Q_SCALE = HEAD_DIM ** -0.5 * LOG2E

LANES = 128
SUBLANES = 8
PAIR = LANES // HEAD_DIM
QBLK = 128
KBLK = QBLK + 2 * SIDE
FAST_AHEAD = (2, 2, 4)
VMEM_LIMIT = 56 * 1024 * 1024


def _proj_kernel(x_ref, w_ref, o_ref, xb_scr):
    @pl.when(pl.program_id(1) == 0)
    def _():
        xb_scr[...] = x_ref[...].astype(jnp.bfloat16)
    o_ref[...] = jnp.dot(xb_scr[...], w_ref[...].astype(jnp.bfloat16),
                         preferred_element_type=jnp.float32).astype(o_ref.dtype)


def _input_projection(x2, wb, tm=2048, tn=512):
    M, D = x2.shape
    N = wb.shape[1]
    last = M // tm - 1
    x_tile = lambda m, n: (jnp.minimum(m + jnp.minimum(n, 1), last), 0)
    return pl.pallas_call(
        _proj_kernel,
        grid=(M // tm, N // tn),
        in_specs=[pl.BlockSpec((tm, D), x_tile),
                  pl.BlockSpec((D, tn), lambda m, n: (0, n))],
        out_specs=pl.BlockSpec((tm, tn), lambda m, n: (m, n)),
        out_shape=jax.ShapeDtypeStruct((M, N), jnp.bfloat16),
        scratch_shapes=[pltpu.VMEM((tm, D), jnp.bfloat16)],
        compiler_params=pltpu.CompilerParams(
            dimension_semantics=("arbitrary", "arbitrary"), vmem_limit_bytes=VMEM_LIMIT),
        name="in_proj",
    )(x2, wb)


def _head0_lanes(rows):
    return lax.broadcasted_iota(jnp.int32, (rows, LANES), 1) < HEAD_DIM


def _softmax_block(lhs, kb, vb, bias, lane_lt):
    s = lax.dot_general(lhs, kb, (((1,), (1,)), ((), ())),
                        preferred_element_type=jnp.float32) + bias
    return _softmax_pv(s, vb, lane_lt)


def _softmax_pv(s, vb, lane_lt):
    nq = s.shape[0] // 2

    def half(sh):
        m = jnp.max(sh, axis=-1, keepdims=True)
        return jnp.exp2(sh - m).astype(jnp.bfloat16), m

    p0, m0 = half(s[:nq])
    p1, m1 = half(s[nq:])
    v_ones = jnp.concatenate([vb, jnp.ones_like(vb)], axis=1)
    pv = jnp.dot(jnp.concatenate([p0, p1], axis=0), v_ones, preferred_element_type=jnp.float32)
    return (jnp.where(lane_lt, pv[:nq, :LANES], pv[nq:, :LANES]), jnp.where(lane_lt, m0, m1),
            jnp.where(lane_lt, pv[:nq, LANES:], pv[nq:, LANES:]))


def _pair_norms2(xb):
    x = xb.astype(jnp.float32)
    row = lax.broadcasted_iota(jnp.int32, (LANES, 2 * LANES), 0)
    col = lax.broadcasted_iota(jnp.int32, (LANES, 2 * LANES), 1)
    pick = ((row < HEAD_DIM) == (col < LANES)).astype(jnp.bfloat16)
    return jnp.dot((x * x).astype(jnp.bfloat16), pick, preferred_element_type=jnp.float32)


def _head_max(t):
    while t.ndim > 2:
        t = jnp.max(t, axis=0)
    n = t.shape[0] // PAIR
    full = lambda x: jnp.max(jnp.max(x, axis=0, keepdims=True), axis=1, keepdims=True)
    return full(t[:n]), full(t[n:])


def _softmax_shift(q_ref, k_ref, tops):
    n_rows = q_ref.shape[0]

    def max_norm2(ref):
        rows = jnp.concatenate([ref[r0:r0 + SHIFT_SAMPLE_ROWS, :]
                                for r0 in range(0, n_rows, n_rows // SHIFT_SAMPLES)], axis=0)
        return jnp.max(_pair_norms2(rows), axis=0, keepdims=True)

    bound = jnp.sqrt(max_norm2(q_ref) * max_norm2(k_ref)) * (Q_SCALE * SHIFT_MARGIN)
    return [jnp.max(bound[:, hh * LANES:(hh + 1) * LANES], axis=1, keepdims=True)
            + functools.reduce(jnp.maximum, [t[hh] for t in tops]) for hh in range(PAIR)]


def _out_of_range(dens):
    lo = jnp.min(functools.reduce(jnp.minimum, dens))
    hi = jnp.max(functools.reduce(jnp.maximum, dens))
    return jnp.logical_not(jnp.logical_and(lo >= MIN_DENOMINATOR, hi <= MAX_DENOMINATOR))


def _attn_a_kernel(q_ref, k_ref, v_ref, g_ref, b1_ref, b4_ref, b16_ref, y_ref,
                   f32_scr, f4_scr, qm_scr, kp_scr, vp_scr, s1_scr, s4_scr, s16_scr,
                   num_scr, m_scr, den_scr, before_check=None):
    S = q_ref.shape[0]
    CH = 256
    bias_refs = (b1_ref, b4_ref, b16_ref)
    shifted_refs = (s1_scr, s4_scr, s16_scr)

    STEP = DILATIONS[1] // DILATIONS[0]
    assert all(b == a * STEP for a, b in zip(DILATIONS, DILATIONS[1:]))

    def regroup(emit, keep_natural):
        src, dst = f32_scr, f4_scr
        if keep_natural:
            for row0 in range(0, S, CH):
                emit(0, row0, src[row0:row0 + CH, :])
        for c in range(1, len(DILATIONS)):
            groups = DILATIONS[c - 1]
            L = S // groups
            ch = min(CH, L // STEP)
            for g in range(groups):
                for s in range(STEP):
                    for j0 in range(0, L // STEP, ch):
                        blk = src[pl.ds(g * L + s + j0 * STEP, ch, stride=STEP), :]
                        row0 = (g + groups * s) * (L // STEP) + j0
                        emit(c, row0, blk)
                        if c + 1 < len(DILATIONS):
                            dst[row0:row0 + ch, :] = blk
            src, dst = dst, src

    def emit_to(dst):
        def emit(c, row0, blk):
            dst[c - 1, row0:row0 + blk.shape[0], :] = blk.astype(jnp.bfloat16)
        return emit

    def emit_q(c, row0, blk):
        lt = _head0_lanes(blk.shape[0])
        rows = slice(row0, row0 + blk.shape[0])
        qm_scr[c, 0, rows, :] = jnp.where(lt, blk, 0.0).astype(jnp.bfloat16)
        qm_scr[c, 1, rows, :] = jnp.where(lt, 0.0, blk).astype(jnp.bfloat16)

    f32_scr[...] = q_ref[...].astype(jnp.float32) * Q_SCALE
    regroup(emit_q, True)
    f32_scr[...] = k_ref[...].astype(jnp.float32)
    regroup(emit_to(kp_scr), False)
    f32_scr[...] = v_ref[...].astype(jnp.float32)
    regroup(emit_to(vp_scr), False)

    shift = _softmax_shift(q_ref, k_ref, [_head_max(r[...]) for r in bias_refs])
    for bias_ref, shifted_ref in zip(bias_refs, shifted_refs):
        for v in range(bias_ref.shape[0]):
            for hh in range(PAIR):
                rows = slice(hh * QBLK, (hh + 1) * QBLK)
                shifted_ref[v, rows, :] = bias_ref[v, rows, :] - shift[hh]

    lane_q = _head0_lanes(QBLK)

    def block_operands(c, d, i):
        L = S // d
        nblk = L // QBLK
        nk = min(KBLK, L)
        r, bi = divmod(i, nblk)
        q0 = i * QBLK
        lhs = jnp.concatenate([qm_scr[c, 0, q0:q0 + QBLK, :],
                               qm_scr[c, 1, q0:q0 + QBLK, :]], axis=0)
        if nk < KBLK:
            k0, variant = r * L, 0
        else:
            k0 = r * L + min(max(bi * QBLK - SIDE, 0), L - KBLK)
            variant = 0 if bi == 0 else (2 if bi == nblk - 1 else 1)
        keys = slice(k0, k0 + nk)
        kb = k_ref[keys, :] if d == 1 else kp_scr[c - 1, keys, :]
        vb = v_ref[keys, :] if d == 1 else vp_scr[c - 1, keys, :]
        return lhs, kb, vb, variant

    def natural_rows(d, i):
        r, bi = divmod(i, S // d // QBLK)
        return pl.ds(r + bi * QBLK * d, QBLK, stride=d)

    def gate_store(rows, num, den):
        g = g_ref[rows, :].astype(jnp.float32)
        y_ref[rows, :] = (num * g / (den * (1.0 + jnp.exp(-g)))).astype(y_ref.dtype)

    def fast_probs(c, d, i):
        lhs, kb, vb, variant = block_operands(c, d, i)
        s = lax.dot_general(lhs, kb, (((1,), (1,)), ((), ())), preferred_element_type=jnp.float32)
        return jnp.exp2(s + shifted_refs[c][variant]).astype(jnp.bfloat16), vb

    def fast_pv(p, vb):
        pv = jnp.dot(p, jnp.concatenate([vb, jnp.ones_like(vb)], axis=1),
                     preferred_element_type=jnp.float32)
        return (jnp.where(lane_q, pv[:QBLK, :LANES], pv[QBLK:, :LANES]),
                jnp.where(lane_q, pv[:QBLK, LANES:], pv[QBLK:, LANES:]))

    nblocks = S // QBLK

    def fast_branch(c, ahead, sink):
        d = DILATIONS[c]
        pending = [fast_probs(c, d, i) for i in range(min(ahead, nblocks))]
        for i in range(nblocks):
            if i + ahead < nblocks:
                pending.append(fast_probs(c, d, i + ahead))
            sink(d, i, *fast_pv(*pending.pop(0)))

    def coarser_rows(c, i):
        L = S // DILATIONS[c]
        r, bi = divmod(i, L // QBLK)
        g, s = r % DILATIONS[c - 1], r // DILATIONS[c - 1]
        return pl.ds(g * L * STEP + s + bi * QBLK * STEP, QBLK, stride=STEP)

    den_mins = []

    def fast_sink(c):
        def sink(d, i, num, den):
            own = slice(i * QBLK, (i + 1) * QBLK)
            if c + 1 < len(DILATIONS):
                num = num + num_scr[c, own, :]
                den = den + den_scr[c, own, :]
            if c > 0:
                rows = coarser_rows(c, i)
                num_scr[c - 1, rows, :] = num
                den_scr[c - 1, rows, :] = den
            else:
                gate_store(own, num, den)
                den_mins.append(den)
        return sink

    for c in reversed(range(len(DILATIONS))):
        fast_branch(c, FAST_AHEAD[c], fast_sink(c))

    if before_check is not None:
        before_check()

    @pl.when(_out_of_range(den_mins))
    def _():
        def scores(c, d, i):
            lhs, kb, vb, variant = block_operands(c, d, i)
            s = lax.dot_general(lhs, kb, (((1,), (1,)), ((), ())),
                                preferred_element_type=jnp.float32) + bias_refs[c][variant]
            return s, vb

        def run_branch(c, ahead, sink):
            d = DILATIONS[c]
            pending = [scores(c, d, i) for i in range(min(ahead, nblocks))]
            for i in range(nblocks):
                if i + ahead < nblocks:
                    pending.append(scores(c, d, i + ahead))
                s, vb = pending.pop(0)
                sink(d, i, *_softmax_pv(s, vb, lane_q))

        def stash(c):
            def sink(d, i, num, m, den):
                rows = natural_rows(d, i)
                num_scr[c - 1, rows, :] = num
                m_scr[c - 1, rows, :] = m
                den_scr[c - 1, rows, :] = den
            return sink

        def combine(d, i, num, m, den):
            rows = slice(i * QBLK, (i + 1) * QBLK)
            m4, m16 = m_scr[0, rows, :], m_scr[1, rows, :]
            top = jnp.maximum(jnp.maximum(m, m4), m16)
            w1, w4, w16 = jnp.exp2(m - top), jnp.exp2(m4 - top), jnp.exp2(m16 - top)
            gate_store(rows,
                       w1 * num + w4 * num_scr[0, rows, :] + w16 * num_scr[1, rows, :],
                       w1 * den + w4 * den_scr[0, rows, :] + w16 * den_scr[1, rows, :])

        run_branch(2, 6, stash(2))
        run_branch(1, 2, stash(1))
        run_branch(0, 2, combine)


def _attn_b_kernel(q_ref, k_ref, v_ref, g_ref, bias_ref, y_ref, qm_scr):
    S = q_ref.shape[0]
    CH = 256
    lane_lt = _head0_lanes(CH)

    for row0 in range(0, S, CH):
        rows = slice(row0, row0 + CH)
        blk = q_ref[rows, :].astype(jnp.float32) * Q_SCALE
        qm_scr[0, rows, :] = jnp.where(lane_lt, blk, 0.0).astype(jnp.bfloat16)
        qm_scr[1, rows, :] = jnp.where(lane_lt, 0.0, blk).astype(jnp.bfloat16)

    lane_q = _head0_lanes(GRID_W)
    nk = NA_ROWS * GRID_W
    last = GRID_ROWS - NA_ROWS

    for r in range(GRID_ROWS):
        rows = slice(r * GRID_W, (r + 1) * GRID_W)
        k0 = min(max(r - NA_ROWS // 2, 0), last) * GRID_W
        variant = r if r < NA_ROWS // 2 else (NA_ROWS // 2 if r <= last + NA_ROWS // 2 else r - last)
        lhs = jnp.concatenate([qm_scr[0, rows, :], qm_scr[1, rows, :]], axis=0)
        num, _, den = _softmax_block(lhs, k_ref[k0:k0 + nk, :], v_ref[k0:k0 + nk, :],
                                     bias_ref[variant], lane_q)
        g = g_ref[rows, :].astype(jnp.float32)
        y_ref[rows, :] = (num * g / (den * (1.0 + jnp.exp(-g)))).astype(y_ref.dtype)


def _attn_kernel(qa_ref, ka_ref, va_ref, ga_ref, b1_ref, b4_ref, b16_ref,
                 qb_ref, kb_ref, vb_ref, gb_ref, bb_ref, ya_ref, yb_ref, *scratch):
    _attn_a_kernel(qa_ref, ka_ref, va_ref, ga_ref, b1_ref, b4_ref, b16_ref, ya_ref, *scratch[:-1],
                   before_check=functools.partial(_attn_b_kernel, qb_ref, kb_ref, vb_ref, gb_ref, bb_ref,
                                                  yb_ref, scratch[-1]))


def _attention(h, b1, b4, b16, bias_b):
    B, S, _ = h.shape
    assert N_HEADS_A == N_HEADS_B
    nhp = N_HEADS_A // PAIR
    blk = lambda off: pl.BlockSpec((None, S, LANES), lambda p, b: (b, 0, off + p))
    tab = lambda t: pl.BlockSpec((None,) + t.shape[1:], lambda p, b: (p, 0, 0, 0))
    out = lambda: pl.BlockSpec((None, S, LANES), lambda p, b: (b, 0, p))
    base = 4 * WIDTH_A // LANES
    return pl.pallas_call(
        _attn_kernel,
        grid=(nhp, B),
        in_specs=[blk(0), blk(nhp), blk(2 * nhp), blk(3 * nhp), tab(b1), tab(b4), tab(b16),
                  blk(base), blk(base + nhp), blk(base + 2 * nhp), blk(base + 3 * nhp), tab(bias_b)],
        out_specs=[out(), out()],
        out_shape=[jax.ShapeDtypeStruct((B, S, WIDTH_A), jnp.bfloat16),
                   jax.ShapeDtypeStruct((B, S, WIDTH_B), jnp.bfloat16)],
        scratch_shapes=[
            pltpu.VMEM((S, LANES), jnp.float32),
            pltpu.VMEM((S, LANES), jnp.float32),
            pltpu.VMEM((3, PAIR, S, LANES), jnp.bfloat16),
            pltpu.VMEM((2, S, LANES), jnp.bfloat16),
            pltpu.VMEM((2, S, LANES), jnp.bfloat16),
            pltpu.VMEM(b1.shape[1:], jnp.float32),
            pltpu.VMEM(b4.shape[1:], jnp.float32),
            pltpu.VMEM(b16.shape[1:], jnp.float32),
            pltpu.VMEM((2, S, LANES), jnp.float32),
            pltpu.VMEM((2, S, LANES), jnp.float32),
            pltpu.VMEM((2, S, LANES), jnp.float32),
            pltpu.VMEM((PAIR, S, LANES), jnp.bfloat16),
        ],
        compiler_params=pltpu.CompilerParams(
            dimension_semantics=("arbitrary", "arbitrary"), vmem_limit_bytes=VMEM_LIMIT),
        name="attention",
    )(h, h, h, h, b1, b4, b16, h, h, h, h, bias_b)


OUT_PARTS = 4


def _out_kernel(ya_ref, yb_ref, wa_ref, wb_ref, x_ref, gain_ref, bias_ref, o_ref, w_scr):
    @pl.when(pl.program_id(0) == 0)
    def _():
        w_scr[0] = wa_ref[...].astype(jnp.bfloat16)
        w_scr[1] = wb_ref[...].astype(jnp.bfloat16)

    part = o_ref.shape[0] // OUT_PARTS
    for rows in (slice(i * part, (i + 1) * part) for i in range(OUT_PARTS)):
        out = jnp.dot(ya_ref[rows, :], w_scr[0], preferred_element_type=jnp.float32)
        out = out + jnp.dot(yb_ref[rows, :], w_scr[1], preferred_element_type=jnp.float32)
        z = DEEPNORM_ALPHA * x_ref[rows, :] + out
        mu = jnp.mean(z, axis=-1, keepdims=True)
        zc = z - mu
        var = jnp.mean(zc * zc, axis=-1, keepdims=True)
        o_ref[rows, :] = zc * lax.rsqrt(var + LN_EPS) * gain_ref[...] + bias_ref[...]


def _output_projection(ya, yb, w, x2, gain, bias, tm=512):
    M, D = x2.shape
    assert WIDTH_A == WIDTH_B
    return pl.pallas_call(
        _out_kernel,
        grid=(M // tm,),
        in_specs=[pl.BlockSpec((tm, WIDTH_A), lambda i: (i, 0)),
                  pl.BlockSpec((tm, WIDTH_B), lambda i: (i, 0)),
                  pl.BlockSpec((WIDTH_A, D), lambda i: (0, 0), pipeline_mode=pl.Buffered(1)),
                  pl.BlockSpec((WIDTH_B, D), lambda i: (1, 0), pipeline_mode=pl.Buffered(1)),
                  pl.BlockSpec((tm, D), lambda i: (i, 0)),
                  pl.BlockSpec((1, D), lambda i: (0, 0)),
                  pl.BlockSpec((1, D), lambda i: (0, 0))],
        out_specs=pl.BlockSpec((tm, D), lambda i: (i, 0)),
        out_shape=jax.ShapeDtypeStruct((M, D), jnp.float32),
        scratch_shapes=[pltpu.VMEM((2, WIDTH_A, D), jnp.bfloat16)],
        compiler_params=pltpu.CompilerParams(
            dimension_semantics=("arbitrary",), vmem_limit_bytes=VMEM_LIMIT),
        name="out_proj_ln",
    )(ya, yb, w, w, x2, gain, bias)


def _t5_bucket_index(rel):
    half = NUM_BUCKETS // 2
    max_exact = half // 2
    n = np.abs(rel)
    large = max_exact + (np.log(np.maximum(n, 1) / max_exact)
                         / np.log(T5_MAX_DISTANCE / max_exact)
                         * (half - max_exact)).astype(np.int64)
    large = np.minimum(large, half - 1)
    return ((rel > 0).astype(np.int64) * half + np.where(n < max_exact, n, large)).astype(np.int32)


BAND_LANES = 4 * LANES
BAND_ZERO = BAND_LANES // 2


def _dilated_band_buckets():
    lane = np.arange(BAND_LANES)
    out = np.empty((len(DILATIONS), 2, SUBLANES, BAND_LANES), np.int32)
    for c, d in enumerate(DILATIONS):
        for j, start in enumerate((0, SIDE)):
            rel = lane - BAND_ZERO - start
            out[c, j] = np.where(np.abs(rel) <= SIDE, _t5_bucket_index(rel * d), -1)[None, :]
    return out


def _dilated_bias_kernel(t5_ref, band_ref, b1_ref, b4_ref, b16_ref):
    p = pl.program_id(0)
    for c, out_ref in enumerate((b1_ref, b4_ref, b16_ref)):
        nv, _, nk = out_ref.shape
        for hh in range(PAIR):
            rolled = []
            for j in range(min(nv, 2)):
                idx = band_ref[c, j]
                profile = jnp.full(idx.shape, NEG_INF, jnp.float32)
                for bkt in range(NUM_BUCKETS):
                    profile = jnp.where(idx == bkt, t5_ref[bkt, p * PAIR + hh] * LOG2E, profile)
                rows = jnp.concatenate([profile] * (QBLK // SUBLANES), axis=0)
                rolled.append(pltpu.roll(rows, 0, 1, stride=1, stride_axis=0))
            own = slice(hh * QBLK, (hh + 1) * QBLK)
            out_ref[0, own, :] = rolled[0][:, BAND_ZERO:BAND_ZERO + nk]
            if nv > 1:
                out_ref[1, own, :] = rolled[1][:, BAND_ZERO:BAND_ZERO + nk]
                out_ref[2, own, :] = rolled[0][:, BAND_ZERO - 2 * SIDE:BAND_ZERO - 2 * SIDE + nk]


RPB_ROWS = 2 * NA_ROWS - 1
RPB_COLS = 2 * NA_COLS - 1
NBR_VARIANT_ROWS = (tuple(range(NA_ROWS // 2)) + (NA_ROWS // 2,)
                    + tuple(range(GRID_ROWS - NA_ROWS // 2 + 1, GRID_ROWS)))


def _nbr_bias_kernel(rpb_ref, out_ref, m_scr):
    p = pl.program_id(0)
    shape = (GRID_W, LANES)
    qc = lax.broadcasted_iota(jnp.int32, shape, 0)
    lane = lax.broadcasted_iota(jnp.int32, shape, 1)
    kc = lane & (GRID_W - 1)
    qs = jnp.clip(qc - NA_COLS // 2, 0, GRID_W - NA_COLS)
    in_window = (kc >= qs) & (kc < qs + NA_COLS)
    first = lane < GRID_W
    u = lax.broadcasted_iota(jnp.int32, (SUBLANES, LANES), 1)
    delta = jnp.where(u < NA_COLS, u, jnp.where(u >= LANES - NA_COLS, u - LANES, u - GRID_W))
    col = jnp.where(jnp.abs(delta) < NA_COLS, delta + NA_COLS - 1, -1)
    for hh in range(PAIR):
        head = p * PAIR + hh

        for ri in range(RPB_ROWS):
            base = (head * RPB_ROWS + ri) * RPB_COLS
            profile = jnp.full(col.shape, NEG_INF, jnp.float32)
            for t in range(RPB_COLS):
                profile = jnp.where(col == t, rpb_ref[base + t] * LOG2E, profile)
            rows = jnp.concatenate([profile] * (GRID_W // SUBLANES), axis=0)
            rolled = pltpu.roll(rows, 0, 1, stride=1, stride_axis=0)
            m_scr[ri] = jnp.where(in_window, rolled, NEG_INF)
        for v, r in enumerate(NBR_VARIANT_ROWS):
            rs = min(max(r - NA_ROWS // 2, 0), GRID_ROWS - NA_ROWS)
            for j2 in range(NA_ROWS * GRID_W // LANES):
                ri = rs + 2 * j2 - r + NA_ROWS - 1
                out_ref[v, hh * GRID_W:(hh + 1) * GRID_W, j2 * LANES:(j2 + 1) * LANES] = \
                    jnp.where(first, m_scr[ri], m_scr[ri + 1])


def _bias_kernel(t5_ref, rpb_ref, band_ref, b1_ref, b4_ref, b16_ref, bb_ref, m_scr):
    _dilated_bias_kernel(t5_ref, band_ref, b1_ref, b4_ref, b16_ref)
    _nbr_bias_kernel(rpb_ref, bb_ref, m_scr)


def _bias_tables(t5_table, rpb):
    assert N_HEADS_A == N_HEADS_B
    band = jnp.asarray(_dilated_band_buckets())
    nhp = N_HEADS_A // PAIR
    out_shapes = []
    for d in DILATIONS:
        L = SEQ // d
        nv, nk = (1, L) if L < KBLK else (3, KBLK)
        out_shapes.append(jax.ShapeDtypeStruct((nhp, nv, PAIR * QBLK, nk), jnp.float32))
    out_shapes.append(jax.ShapeDtypeStruct(
        (nhp, len(NBR_VARIANT_ROWS), PAIR * GRID_W, NA_ROWS * GRID_W), jnp.float32))
    return pl.pallas_call(
        _bias_kernel,
        grid=(nhp,),
        in_specs=[pl.BlockSpec(memory_space=pltpu.SMEM),
                  pl.BlockSpec(memory_space=pltpu.SMEM),
                  pl.BlockSpec(band.shape, lambda p: (0, 0, 0, 0))],
        out_specs=[pl.BlockSpec((None,) + o.shape[1:], lambda p: (p, 0, 0, 0)) for o in out_shapes],
        out_shape=out_shapes,
        scratch_shapes=[pltpu.VMEM((RPB_ROWS, GRID_W, LANES), jnp.float32)],
        compiler_params=pltpu.CompilerParams(dimension_semantics=("arbitrary",)),
        name="bias_tables",
    )(t5_table, rpb.reshape(-1), band)


def kernel(x, w_in, w_out, t5_bias, na_rpb, ln_gain, ln_bias):
    B, S, D = x.shape
    depth = w_in.shape[0]
    for layer in range(depth):
        tables = _bias_tables(t5_bias.astype(jnp.float32), na_rpb[layer].astype(jnp.float32))
        h = _input_projection(x.reshape(B * S, D), w_in[layer])
        h = h.reshape(B, S, IN_WIDTH)
        ya, yb = _attention(h, *tables)
        out = _output_projection(ya.reshape(B * S, WIDTH_A), yb.reshape(B * S, WIDTH_B),
                                 w_out[layer], x.reshape(B * S, D),
                                 ln_gain[layer].reshape(1, D), ln_bias[layer].reshape(1, D))
        x = out.reshape(B, S, D)
    return x
```
